```python
import jax, jax.numpy as jnp
from jax import lax
import numpy as np

D_MODEL = 2048
BATCH = 4
SEQ = 2048
DEPTH = 4
DEC_BATCH = 128
DEC_SEQ = 1
PAST_LEN = 8192
PAGE_SIZE = 128

V_HEAD_DIM = 128
MLA_OUT = D_MODEL // 2
MLA_HEADS = MLA_OUT // V_HEAD_DIM
QK_NOPE_DIM = 128
QK_ROPE_DIM = 64
Q_LORA_RANK = 512
KV_LORA_RANK = 128
ROPE_THETA = 10000.0
Q_BLOCK = 128
MLA_SCALE = (QK_NOPE_DIM + QK_ROPE_DIM) ** -0.5
RWKV_WIDTH = D_MODEL // 4
RWKV_HEAD_SIZE = 64
RWKV_HEADS = RWKV_WIDTH // RWKV_HEAD_SIZE
DECAY_LORA = 96
ICLR_LORA = 96
GATE_LORA = 256
GN_EPS = 64e-5
POOL_WIDTH = D_MODEL // 4
POOL_WINDOWS = (2, 4, 8, 16)
POOL_GROUPS = len(POOL_WINDOWS)
POOL_GROUP_DIM = POOL_WIDTH // POOL_GROUPS
POOL_BUF = max(POOL_WINDOWS) - 1
A_SIZES = (Q_LORA_RANK, KV_LORA_RANK, QK_ROPE_DIM)
B_SIZES = (RWKV_WIDTH, DECAY_LORA, RWKV_WIDTH, RWKV_WIDTH, ICLR_LORA, GATE_LORA)
A_COLS = sum(A_SIZES)
B_COLS = sum(B_SIZES)
C_COLS = POOL_WIDTH
IN_COLS = A_COLS + B_COLS + C_COLS
D_MIX = MLA_OUT + RWKV_WIDTH + POOL_WIDTH
D_FF = 5632
CONV_WIDTH = 3
NORM_EPS = 1e-6

kernel_name = 'hymba_mla_rwkv7_pool_decoder'


def split_sizes(x, sizes):
    return jnp.split(x, np.cumsum(sizes)[:-1].tolist(), axis=-1)


def rms_norm(x, g, eps=NORM_EPS):
    xf = x.astype(jnp.float32)
    y = xf * lax.rsqrt(jnp.mean(xf * xf, axis=-1, keepdims=True) + eps)
    return (y * g.astype(jnp.float32)).astype(x.dtype)


def modulate(x, g, shift, scale):
    return rms_norm(x, g) * (1 + scale[:, None, :]) + shift[:, None, :]


def rope(x, pos):
    d = x.shape[-1]
    half = d // 2
    inv = jnp.power(ROPE_THETA, -jnp.arange(half, dtype=jnp.float32) * 2.0 / d)
    ang = pos.astype(jnp.float32)[:, None] * inv[None, :]
    shape = (pos.shape[0],) + (1,) * (x.ndim - 3) + (half,)
    cos = jnp.cos(ang).reshape(shape)
    sin = jnp.sin(ang).reshape(shape)
    xf = x.astype(jnp.float32)
    x1, x2 = xf[..., :half], xf[..., half:]
    return jnp.concatenate([x1 * cos - x2 * sin, x2 * cos + x1 * sin], axis=-1).astype(x.dtype)


def mla_project(pa, pos, q_lora_g, w_uq, kv_lora_g, qn_g, qr_g, kr_g):
    b, t = pa.shape[:2]
    cq, ckv, kr = split_sizes(pa, A_SIZES)
    q = (rms_norm(cq, q_lora_g) @ w_uq).reshape(b, t, MLA_HEADS, QK_NOPE_DIM + QK_ROPE_DIM)
    qn, qr = q[..., :QK_NOPE_DIM], q[..., QK_NOPE_DIM:]
    qn = rms_norm(qn, qn_g)
    qr = rope(rms_norm(qr, qr_g), pos)
    lat = rms_norm(ckv, kv_lora_g)
    kr = rope(rms_norm(kr, kr_g), pos)
    return qn, qr, lat, kr


def mla_keys(lat, w_uk, kn_g):
    return rms_norm(jnp.einsum('btr,rhd->bthd', lat, w_uk), kn_g)


def mla_scores(qn, qr, kn, kr):
    s = jnp.einsum('bqhd,bkhd->bhqk', qn, kn) + jnp.einsum('bqhd,bkd->bhqk', qr, kr)
    return s.astype(jnp.float32) * MLA_SCALE


def mla_prompt(qn, qr, lat, kr, pos, w_uk, kn_g, w_uv):
    b, t = qn.shape[:2]
    kn = mla_keys(lat, w_uk, kn_g)
    v = jnp.einsum('btr,rhd->bthd', lat, w_uv)
    nb = t // Q_BLOCK

    def blocks(a):
        return a.reshape((b, nb, Q_BLOCK) + a.shape[2:]).swapaxes(0, 1)

    def one_block(args):
        qn_i, qr_i, qpos = args
        s = mla_scores(qn_i, qr_i, kn, kr)
        s = jnp.where(pos[None, :] <= qpos[:, None], s, -jnp.inf)
        p = jax.nn.softmax(s, axis=-1).astype(v.dtype)
        return jnp.einsum('bhqk,bkhd->bqhd', p, v)

    o = lax.map(one_block, (blocks(qn), blocks(qr), pos.reshape(nb, Q_BLOCK)))
    return o.swapaxes(0, 1).reshape(b, t, MLA_OUT)


def mla_sample(qn, qr, lat, kr, pos, w_uk, kn_g, w_uv, pool_lat, pool_kr, page_table):
    b, t = qn.shape[:2]
    n_pages = page_table.shape[1]
    past = n_pages * PAGE_SIZE
    kn_new = mla_keys(lat, w_uk, kn_g)
    v_new = jnp.einsum('btr,rhd->bthd', lat, w_uv)

    def page_scores(phys):
        return mla_scores(qn, qr, mla_keys(pool_lat[phys], w_uk, kn_g), pool_kr[phys])

    s_past = lax.map(page_scores, page_table.T)
    s_past = s_past.transpose(1, 2, 3, 0, 4).reshape(b, MLA_HEADS, t, past)
    s_new = mla_scores(qn, qr, kn_new, kr)
    s_new = jnp.where(pos[None, :] <= pos[:, None], s_new, -jnp.inf)
    p = jax.nn.softmax(jnp.concatenate([s_past, s_new], axis=-1), axis=-1)
    p_past = p[..., :past].reshape(b, MLA_HEADS, t, n_pages, PAGE_SIZE).transpose(3, 0, 1, 2, 4)
    p_new = p[..., past:].astype(lat.dtype)

    def page_values(args):
        phys, p_pg = args
        lat_pg = pool_lat[phys]
        return jnp.einsum('bhtk,bkr->bhtr', p_pg.astype(lat_pg.dtype), lat_pg).astype(jnp.float32)

    o_lat = lax.map(page_values, (page_table.T, p_past)).sum(axis=0).astype(lat.dtype)
    o = jnp.einsum('bhtr,rhd->bthd', o_lat, w_uv) + jnp.einsum('bhtk,bkhd->bthd', p_new, v_new)
    return o.reshape(b, t, MLA_OUT)


def rwkv_recurrence(r, w, k, v, kk, a, s0):
    def step(s, inp):
        r_t, w_t, k_t, v_t, kk_t, a_t = inp
        sa = jnp.einsum('bhij,bhj->bhi', s, -kk_t)
        s = (s * w_t[:, :, None, :] + sa[..., None] * (kk_t * a_t)[:, :, None, :]
             + v_t[..., None] * k_t[:, :, None, :])
        return s, jnp.einsum('bhij,bhj->bhi', s, r_t)

    xs = tuple(z.swapaxes(0, 1) for z in (r, w, k, v, kk, a))
    s_final, y = lax.scan(step, s0, xs)
    return y.swapaxes(0, 1), s_final


def rwkv_mix(pb, prev, s0, lp):
    b, t = pb.shape[:2]
    xs = pb + (prev - pb) * lp['rw_mu']
    r, wd, k, v, ad, gd = split_sizes(xs, B_SIZES)
    w = -jax.nn.softplus(-(lp['rw_w0'] + jnp.tanh(wd) @ lp['rw_w_up'])) - 0.5
    decay = jnp.exp(-jnp.exp(w.astype(jnp.float32)))
    a = jax.nn.sigmoid(lp['rw_a0'] + ad @ lp['rw_a_up'])
    g = jax.nn.sigmoid(gd) @ lp['rw_g_up']
    kk = k * lp['rw_k_k']
    k = k * (1 + (a - 1) * lp['rw_k_a'])

    def heads(z):
        return z.reshape(b, t, RWKV_HEADS, RWKV_HEAD_SIZE).astype(jnp.float32)

    r, decay, k, v, kk, a = heads(r), heads(decay), heads(k), heads(v), heads(kk), heads(a)
    kk = kk / jnp.maximum(jnp.sqrt(jnp.sum(kk * kk, axis=-1, keepdims=True)), 1e-12)
    y, s_final = rwkv_recurrence(r, decay, k, v, kk, a, s0.astype(jnp.float32))
    mu = jnp.mean(y, axis=-1, keepdims=True)
    var = jnp.mean(jnp.square(y - mu), axis=-1, keepdims=True)
    y = ((y - mu) * lax.rsqrt(var + GN_EPS)).reshape(b, t, RWKV_WIDTH)
    y = y * lp['rw_lnx_g'] + lp['rw_lnx_b']
    bonus = jnp.sum(r * k * lp['rw_r_k'].astype(jnp.float32), axis=-1, keepdims=True) * v
    y = (y + bonus.reshape(b, t, RWKV_WIDTH)) * g
    return y.astype(pb.dtype), s_final.astype(s0.dtype)


def pool_mix(pc, prefix, pos, w_pool, pool_scale):
    b, t, cdim = pc.shape
    xp = jnp.concatenate([prefix.astype(pc.dtype), pc], axis=1)
    cs = jnp.cumsum(xp.astype(jnp.float32), axis=1)
    cs = jnp.concatenate([jnp.zeros((b, 1, cdim), jnp.float32), cs], axis=1)
    outs = []
    for gi, win in enumerate(POOL_WINDOWS):
        sl = slice(gi * POOL_GROUP_DIM, (gi + 1) * POOL_GROUP_DIM)
        total = cs[:, POOL_BUF + 1:, sl] - cs[:, POOL_BUF + 1 - win:POOL_BUF + 1 - win + t, sl]
        count = jnp.minimum(pos + 1, win).astype(jnp.float32)[None, :, None]
        diff = total / count - xp[:, POOL_BUF:, sl].astype(jnp.float32)
        outs.append(diff.astype(pc.dtype) @ w_pool[gi])
    return jnp.concatenate(outs, axis=-1) * pool_scale


def conv_ffn(h, prefix, w_up, conv_w, conv_b, w_down):
    t = h.shape[1]
    u = h @ w_up
    up = jnp.concatenate([prefix.astype(u.dtype), u], axis=1)
    conv = conv_b + sum(up[:, j:j + t] * conv_w[j] for j in range(CONV_WIDTH))
    val, gate = jnp.split(conv, 2, axis=-1)
    return (jax.nn.silu(gate) * val) @ w_down, up[:, -(CONV_WIDTH - 1):]


def trunk_layer(x, c, pos, attn_fn, shift_prev, wkv0, pool_prefix, conv_prefix, lp):
    mod = jax.nn.silu(c) @ lp['ada_w'] + lp['ada_b']
    sh1, sc1, g1, sh2, sc2, g2 = jnp.split(mod, 6, axis=-1)
    h = modulate(x, lp['norm_mix_g'], sh1, sc1)
    pa, pb, pc = split_sizes(h @ lp['w_in'], (A_COLS, B_COLS, C_COLS))
    qn, qr, lat, kr = mla_project(pa, pos, lp['q_lora_g'], lp['w_uq'], lp['kv_lora_g'],
                                  lp['qn_g'], lp['qr_g'], lp['kr_g'])
    ya = attn_fn(qn, qr, lat, kr)
    prev = jnp.concatenate([shift_prev[:, None].astype(pb.dtype), pb[:, :-1]], axis=1)
    yb, wkv_new = rwkv_mix(pb, prev, wkv0, lp)
    yc = pool_mix(pc, pool_prefix, pos, lp['w_pool'], lp['pool_scale'])
    x = x + g1[:, None, :] * (jnp.concatenate([ya, yb, yc], axis=-1) @ lp['w_out'])
    h2 = modulate(x, lp['norm_ffn_g'], sh2, sc2)
    yf, conv_new = conv_ffn(h2, conv_prefix, lp['ffn_w_up'], lp['ffn_conv_w'], lp['ffn_conv_b'], lp['ffn_w_down'])
    x = x + g2[:, None, :] * yf
    pool_new = jnp.concatenate([pool_prefix.astype(pc.dtype), pc], axis=1)[:, -POOL_BUF:]
    return x, (lat, kr, wkv_new, pb[:, -1], pool_new, conv_new)


def setup_inputs(seed: int = 0) -> dict:
    key = jax.random.key(seed)
    ks = iter(jax.random.split(key, 64))

    def nrm(shape, scale=1.0):
        return jax.random.normal(next(ks), shape, jnp.float32) * scale

    def gain(shape):
        return 1.0 + nrm(shape, 0.1)

    n_pages = PAST_LEN // PAGE_SIZE
    n_used = DEC_BATCH * n_pages
    n_phys = n_used + n_used // 4
    L = DEPTH
    return {
        'x_prompt': nrm((BATCH, SEQ, D_MODEL)),
        'x_sample': nrm((DEC_BATCH, DEC_SEQ, D_MODEL)),
        'cache_latent': nrm((L, n_phys, PAGE_SIZE, KV_LORA_RANK)),
        'cache_krope': nrm((L, n_phys, PAGE_SIZE, QK_ROPE_DIM)),
        'state_wkv': nrm((L, DEC_BATCH, RWKV_HEADS, RWKV_HEAD_SIZE, RWKV_HEAD_SIZE), 0.5),
        'state_shift': nrm((L, DEC_BATCH, B_COLS)),
        'state_pool': nrm((L, DEC_BATCH, POOL_BUF, POOL_WIDTH)),
        'state_conv': nrm((L, DEC_BATCH, CONV_WIDTH - 1, 2 * D_FF)),
        'page_table': jax.random.permutation(next(ks), n_phys)[:n_used].reshape(DEC_BATCH, n_pages).astype(jnp.int32),
        'c_prompt': nrm((BATCH, D_MODEL)),
        'c_sample': nrm((DEC_BATCH, D_MODEL)),
        'ada_w': nrm((L, D_MODEL, 6 * D_MODEL), 0.5 * D_MODEL ** -0.5),
        'ada_b': nrm((L, 6 * D_MODEL), 0.01),
        'norm_mix_g': gain((L, D_MODEL)),
        'norm_ffn_g': gain((L, D_MODEL)),
        'w_in': nrm((L, D_MODEL, IN_COLS), D_MODEL ** -0.5),
        'q_lora_g': gain((L, Q_LORA_RANK)),
        'w_uq': nrm((L, Q_LORA_RANK, MLA_HEADS * (QK_NOPE_DIM + QK_ROPE_DIM)), Q_LORA_RANK ** -0.5),
        'kv_lora_g': gain((L, KV_LORA_RANK)),
        'w_uk': nrm((L, KV_LORA_RANK, MLA_HEADS, QK_NOPE_DIM), KV_LORA_RANK ** -0.5),
        'w_uv': nrm((L, KV_LORA_RANK, MLA_HEADS, V_HEAD_DIM), KV_LORA_RANK ** -0.5),
        'qn_g': gain((L, QK_NOPE_DIM)),
        'qr_g': gain((L, QK_ROPE_DIM)),
        'kn_g': gain((L, QK_NOPE_DIM)),
        'kr_g': gain((L, QK_ROPE_DIM)),
        'rw_mu': jax.random.uniform(next(ks), (L, B_COLS), jnp.float32),
        'rw_w0': jax.random.uniform(next(ks), (L, RWKV_WIDTH), jnp.float32, -6.0, -0.5),
        'rw_w_up': nrm((L, DECAY_LORA, RWKV_WIDTH), 0.5 * DECAY_LORA ** -0.5),
        'rw_a0': nrm((L, RWKV_WIDTH), 0.5),
        'rw_a_up': nrm((L, ICLR_LORA, RWKV_WIDTH), ICLR_LORA ** -0.5),
        'rw_g_up': nrm((L, GATE_LORA, RWKV_WIDTH), GATE_LORA ** -0.5),
        'rw_k_k': 0.85 + nrm((L, RWKV_WIDTH), 0.05),
        'rw_k_a': 1.0 + nrm((L, RWKV_WIDTH), 0.05),
        'rw_r_k': nrm((L, RWKV_HEADS, RWKV_HEAD_SIZE), 0.1),
        'rw_lnx_g': gain((L, RWKV_WIDTH)),
        'rw_lnx_b': nrm((L, RWKV_WIDTH), 0.01),
        'w_pool': nrm((L, POOL_GROUPS, POOL_GROUP_DIM, POOL_GROUP_DIM), POOL_GROUP_DIM ** -0.5),
        'pool_scale': 0.5 + nrm((L, POOL_WIDTH), 0.1),
        'w_out': nrm((L, D_MIX, D_MODEL), D_MIX ** -0.5),
        'ffn_w_up': nrm((L, D_MODEL, 2 * D_FF), D_MODEL ** -0.5),
        'ffn_conv_w': nrm((L, CONV_WIDTH, 2 * D_FF), 0.5),
        'ffn_conv_b': nrm((L, 2 * D_FF), 0.01),
        'ffn_w_down': nrm((L, D_FF, D_MODEL), D_FF ** -0.5),
    }


def reference(x_prompt, x_sample, cache_latent, cache_krope, state_wkv, state_shift, state_pool, state_conv,
              page_table, c_prompt, c_sample, ada_w, ada_b, norm_mix_g, norm_ffn_g, w_in, q_lora_g, w_uq,
              kv_lora_g, w_uk, w_uv, qn_g, qr_g, kn_g, kr_g, rw_mu, rw_w0, rw_w_up, rw_a0, rw_a_up, rw_g_up,
              rw_k_k, rw_k_a, rw_r_k, rw_lnx_g, rw_lnx_b, w_pool, pool_scale, w_out, ffn_w_up, ffn_conv_w,
              ffn_conv_b, ffn_w_down):
    b_p, t_p = x_prompt.shape[:2]
    b_s, t_s = x_sample.shape[:2]
    past = page_table.shape[1] * PAGE_SIZE
    pos_p = jnp.arange(t_p, dtype=jnp.int32)
    pos_s = past + jnp.arange(t_s, dtype=jnp.int32)
    dt = x_prompt.dtype
    zero_shift = jnp.zeros((b_p, B_COLS), dt)
    zero_wkv = jnp.zeros((b_p, RWKV_HEADS, RWKV_HEAD_SIZE, RWKV_HEAD_SIZE), dt)
    zero_pool = jnp.zeros((b_p, POOL_BUF, POOL_WIDTH), dt)
    zero_conv = jnp.zeros((b_p, CONV_WIDTH - 1, 2 * D_FF), dt)

    xp, xs = x_prompt, x_sample
    prompt_states, sample_states = [], []
    for l in range(DEPTH):
        lp = dict(ada_w=ada_w[l], ada_b=ada_b[l], norm_mix_g=norm_mix_g[l], norm_ffn_g=norm_ffn_g[l],
                  w_in=w_in[l], q_lora_g=q_lora_g[l], w_uq=w_uq[l], kv_lora_g=kv_lora_g[l],
                  qn_g=qn_g[l], qr_g=qr_g[l], kr_g=kr_g[l],
                  rw_mu=rw_mu[l], rw_w0=rw_w0[l], rw_w_up=rw_w_up[l], rw_a0=rw_a0[l], rw_a_up=rw_a_up[l],
                  rw_g_up=rw_g_up[l], rw_k_k=rw_k_k[l], rw_k_a=rw_k_a[l], rw_r_k=rw_r_k[l],
                  rw_lnx_g=rw_lnx_g[l], rw_lnx_b=rw_lnx_b[l], w_pool=w_pool[l], pool_scale=pool_scale[l],
                  w_out=w_out[l], ffn_w_up=ffn_w_up[l], ffn_conv_w=ffn_conv_w[l], ffn_conv_b=ffn_conv_b[l],
                  ffn_w_down=ffn_w_down[l])
        w_uk_l, kn_g_l, w_uv_l = w_uk[l], kn_g[l], w_uv[l]
        pool_lat_l, pool_kr_l = cache_latent[l], cache_krope[l]

        def attn_prompt(qn, qr, lat, kr):
            return mla_prompt(qn, qr, lat, kr, pos_p, w_uk_l, kn_g_l, w_uv_l)

        def attn_sample(qn, qr, lat, kr):
            return mla_sample(qn, qr, lat, kr, pos_s, w_uk_l, kn_g_l, w_uv_l, pool_lat_l, pool_kr_l, page_table)

        xp, st_p = trunk_layer(xp, c_prompt, pos_p, attn_prompt, zero_shift, zero_wkv, zero_pool, zero_conv, lp)
        xs, st_s = trunk_layer(xs, c_sample, pos_s, attn_sample, state_shift[l], state_wkv[l],
                               state_pool[l], state_conv[l], lp)
        prompt_states.append(st_p)
        sample_states.append(st_s)

    sp = [jnp.stack(z) for z in zip(*prompt_states)]
    ss = [jnp.stack(z) for z in zip(*sample_states)]
    return (xp, xs, sp[0], ss[0], sp[1], ss[1], sp[2], ss[2], sp[3], ss[3], sp[4], ss[4], sp[5], ss[5])
```

```python
import functools

import numpy as np
import jax
import jax.numpy as jnp
from jax import lax
from jax.experimental import pallas as pl
from jax.experimental.pallas import tpu as pltpu

F32 = jnp.float32
BF16 = jnp.bfloat16

MLA_HEADS = 8
QK_NOPE = 128
QK_ROPE = 64
V_HEAD = 128
ROPE_THETA = 10000.0
MLA_SCALE = (QK_NOPE + QK_ROPE) ** -0.5
RW_HEADS = 8
RW_N = 64
DECAY_LORA = 96
ICLR_LORA = 96
GATE_LORA = 256
GN_EPS = 64e-5
NORM_EPS = 1e-6
POOL_WINDOWS = (2, 4, 8, 16)
POOL_BUF = 15
PAGE = 128

LANE = 128
QK_PAD = 256
VMEM_LIMIT = 48 * 1024 * 1024

RW_W = RW_HEADS * RW_N
PB_W = 2048
CQ_OFF = PB_W
PC_OFF = CQ_OFF + 512
CKV_OFF = PC_OFF + 512
KR_OFF = CKV_OFF + 128
P_COLS = KR_OFF + 128

NN = (((1,), (0,)), ((), ()))
NT = (((1,), (1,)), ((), ()))
TN = (((0,), (0,)), ((), ()))


def _cp(*sem):
    return pltpu.CompilerParams(dimension_semantics=sem, vmem_limit_bytes=VMEM_LIMIT)


def _pallas(body, **kw):
    call = pl.pallas_call(body, **kw)

    def run(*args):
        if any(isinstance(a, jax.core.Tracer) for a in args):
            args = [a if a.dtype == jnp.int32 else pltpu.with_memory_space_constraint(a, pltpu.HBM) for a in args]
        return call(*args)
    return run


def _tile(n, pref):
    if n <= pref:
        return n
    t = pref - pref % 8
    while t >= 8:
        if n % t == 0:
            return t
        t -= 8
    return n


def _dot(a, b, dims=NN):
    return lax.dot_general(a, b, dims, preferred_element_type=F32)


def _split2(x):
    hi = x.astype(BF16)
    lo = (x - hi.astype(F32)).astype(BF16)
    return hi, lo


def _split3(x):
    hi = x.astype(BF16)
    r1 = x - hi.astype(F32)
    mid = r1.astype(BF16)
    lo = (r1 - mid.astype(F32)).astype(BF16)
    return hi, mid, lo


def _dot3(a, b, dims=NN):
    ah, al = _split2(a)
    bh, bl = _split2(b)
    return _dot(ah, bh, dims) + (_dot(ah, bl, dims) + _dot(al, bh, dims))


def _dot_sel_l(sel, x, dims=NN):
    h, m, l = _split3(x)
    return _dot(sel, h, dims) + (_dot(sel, m, dims) + _dot(sel, l, dims))


def _dot_sel_r(x, sel, dims=NN):
    h, m, l = _split3(x)
    return _dot(h, sel, dims) + (_dot(m, sel, dims) + _dot(l, sel, dims))


def _sigmoid(x):
    return 1.0 / (1.0 + jnp.exp(-x))


def _rms(x, g, n):
    return (x * lax.rsqrt(jnp.sum(x * x, axis=-1, keepdims=True) / n + NORM_EPS)) * g


def _modnorm(x, g, sc, sh):
    return _rms(x, g, x.shape[-1]) * (1.0 + sc) + sh


def _mod_spec(tm, tn, tpg, rows, col_arg):
    if rows == 1:
        if col_arg:
            return pl.BlockSpec((1, 1, tn), lambda i, j: (i // tpg, 0, j))
        return pl.BlockSpec((1, 1, tn), lambda i, j: (i // tpg, 0, 0))
    if col_arg:
        return pl.BlockSpec((1, tm, tn), lambda i, j: (i // tpg, i % tpg, j))
    return pl.BlockSpec((1, tm, tn), lambda i, j: (i // tpg, i % tpg, 0))


def _ada_kernel(c_ref, w_ref, b_ref, o_ref):
    c = c_ref[...]
    s = c * _sigmoid(c)
    o_ref[0] = _dot(s.astype(BF16), w_ref[0].astype(BF16)) + b_ref[0]


def _ada_mod(c_all, ada_w, ada_b):
    L, D, N = ada_w.shape
    mc = c_all.shape[0]
    tn = _tile(N, 1024)
    return _pallas(
        _ada_kernel,
        out_shape=jax.ShapeDtypeStruct((L, mc, N), F32),
        grid=(L, N // tn),
        in_specs=[pl.BlockSpec((mc, D), lambda l, j: (0, 0)),
                  pl.BlockSpec((1, D, tn), lambda l, j: (l, 0, j)),
                  pl.BlockSpec((1, 1, tn), lambda l, j: (l, 0, j))],
        out_specs=pl.BlockSpec((1, mc, tn), lambda l, j: (l, 0, j)),
        compiler_params=_cp("parallel", "parallel"),
        name="ada_mod",
    )(c_all, ada_w, ada_b.reshape(L, 1, N))


def _nmm_kernel(x_ref, g_ref, sc_ref, sh_ref, w_ref, o_ref, h_ref):
    @pl.when(pl.program_id(1) == 0)
    def _():
        h_ref[...] = _modnorm(x_ref[...], g_ref[...], sc_ref[0], sh_ref[0]).astype(BF16)

    o_ref[...] = _dot(h_ref[...], w_ref[...])


def _norm_mod_matmul(x, g, sc, sh, w, rpg, tm_pref=512):
    M, D = x.shape
    N = w.shape[1]
    tm = _tile(rpg, tm_pref)
    tpg = rpg // tm
    tn = N // 2 if (N // 2) % LANE == 0 else N
    rows = sc.shape[1]
    return _pallas(
        _nmm_kernel,
        out_shape=jax.ShapeDtypeStruct((M, N), F32),
        grid=(M // tm, N // tn),
        in_specs=[pl.BlockSpec((tm, D), lambda i, j: (i, 0)),
                  pl.BlockSpec((1, D), lambda i, j: (0, 0)),
                  _mod_spec(tm, D, tpg, rows, False),
                  _mod_spec(tm, D, tpg, rows, False),
                  pl.BlockSpec((D, tn), lambda i, j: (0, j))],
        out_specs=pl.BlockSpec((tm, tn), lambda i, j: (i, j)),
        scratch_shapes=[pltpu.VMEM((tm, D), BF16)],
        compiler_params=_cp("parallel", "arbitrary"),
        name="norm_mod_matmul",
    )(x, g, sc, sh, w)


def _mmres_kernel(a_ref, w_ref, gate_ref, res_ref, o_ref):
    o_ref[...] = res_ref[...] + gate_ref[0] * _dot(a_ref[...], w_ref[...])


def _matmul_gate_res(a, w, gate, res, rpg, tm_pref=512, tn_pref=512):
    M, K = a.shape
    N = w.shape[1]
    tm = _tile(rpg, tm_pref)
    tpg = rpg // tm
    tn = _tile(N, tn_pref)
    rows = gate.shape[1]
    return _pallas(
        _mmres_kernel,
        out_shape=jax.ShapeDtypeStruct((M, N), F32),
        grid=(M // tm, N // tn),
        in_specs=[pl.BlockSpec((tm, K), lambda i, j: (i, 0)),
                  pl.BlockSpec((K, tn), lambda i, j: (0, j)),
                  _mod_spec(tm, tn, tpg, rows, True),
                  pl.BlockSpec((tm, tn), lambda i, j: (i, j))],
        out_specs=pl.BlockSpec((tm, tn), lambda i, j: (i, j)),
        compiler_params=_cp("parallel", "arbitrary"),
        name="matmul_gate_res",
    )(a, w, gate, res)


def _mla_proj_kernel(cq_ref, ckv_ref, kr_ref, cos_ref, sin_ref, qlg_ref, wuq_ref, kvg_ref, qng_ref, qrg_ref,
                     kng_ref, krg_ref, wuk_ref, wuv_ref, q_ref, k_ref, v_ref, lat_ref, kro_ref):
    cos = cos_ref[...]
    sin = sin_ref[...]
    lane = lax.broadcasted_iota(jnp.int32, cos.shape, 1)
    half = QK_ROPE // 2

    def rope(x):
        rot = jnp.where(lane < half, -pltpu.roll(x, LANE - half, 1), pltpu.roll(x, half, 1))
        return x * cos + rot * sin

    cqn = _rms(cq_ref[...], qlg_ref[...], cq_ref.shape[-1])
    q = _dot(cqn.astype(BF16), wuq_ref[...])
    lat = _rms(ckv_ref[...], kvg_ref[...], ckv_ref.shape[-1])
    lat_ref[...] = lat
    kr = rope(_rms(kr_ref[...], krg_ref[...], QK_ROPE))
    kro_ref[...] = kr
    latb = lat.astype(BF16)
    kx = _dot(latb, wuk_ref[...])
    v_ref[...] = _dot(latb, wuv_ref[...]).astype(v_ref.dtype)
    krb = kr.astype(k_ref.dtype)
    for h in range(MLA_HEADS):
        o = h * QK_PAD
        qn = _rms(q[:, o:o + QK_NOPE], qng_ref[...], QK_NOPE)
        qr = rope(_rms(q[:, o + QK_NOPE:o + QK_PAD], qrg_ref[...], QK_ROPE))
        q_ref[:, o:o + QK_NOPE] = qn.astype(q_ref.dtype)
        q_ref[:, o + QK_NOPE:o + QK_PAD] = qr.astype(q_ref.dtype)
        kn = _rms(kx[:, h * QK_NOPE:(h + 1) * QK_NOPE], kng_ref[...], QK_NOPE)
        k_ref[:, o:o + QK_NOPE] = kn.astype(k_ref.dtype)
        k_ref[:, o + QK_NOPE:o + QK_PAD] = krb


def _mla_proj(p, cos_t, sin_t, lw, rpg, q_dtype, tm_pref=256):
    M = p.shape[0]
    tm = _tile(rpg, tm_pref)
    tpg = rpg // tm
    H = MLA_HEADS
    row = lambda w: pl.BlockSpec((tm, w), lambda i: (i, 0))
    const = lambda a: pl.BlockSpec(a.shape, lambda i: (0,) * a.ndim)
    tab = pl.BlockSpec((tm, LANE), lambda i: (i % tpg, 0))
    weights = (lw["q_lora_g"], lw["w_uq"], lw["kv_lora_g"], lw["qn_g"], lw["qr_g"], lw["kn_g"], lw["kr_g"],
               lw["w_uk"], lw["w_uv"])
    return _pallas(
        _mla_proj_kernel,
        out_shape=(jax.ShapeDtypeStruct((M, H * QK_PAD), q_dtype),
                   jax.ShapeDtypeStruct((M, H * QK_PAD), BF16),
                   jax.ShapeDtypeStruct((M, H * V_HEAD), BF16),
                   jax.ShapeDtypeStruct((M, LANE), F32),
                   jax.ShapeDtypeStruct((M, LANE), F32)),
        grid=(M // tm,),
        in_specs=[pl.BlockSpec((tm, 512), lambda i: (i, CQ_OFF // 512)),
                  pl.BlockSpec((tm, LANE), lambda i: (i, CKV_OFF // LANE)),
                  pl.BlockSpec((tm, LANE), lambda i: (i, KR_OFF // LANE)),
                  tab, tab] + [const(a) for a in weights],
        out_specs=(row(H * QK_PAD), row(H * QK_PAD), row(H * V_HEAD), row(LANE), row(LANE)),
        compiler_params=_cp("parallel"),
        name="mla_proj",
    )(p, p, p, cos_t, sin_t, *weights)


def _attn_kernel(q_ref, k_ref, v_ref, o_ref, *, tq):
    qi = pl.program_id(2)
    q = q_ref[...]
    row = qi * tq + lax.broadcasted_iota(jnp.int32, (tq, tq), 0)
    col0 = lax.broadcasted_iota(jnp.int32, (tq, tq), 1)

    def body(kj, carry):
        m, l, acc = carry
        start = pl.multiple_of(kj * tq, tq)
        k = k_ref[pl.ds(start, tq), :]
        v = v_ref[pl.ds(start, tq), :]
        s = _dot(q, k, NT) * MLA_SCALE
        s = jnp.where(col0 + kj * tq <= row, s, -jnp.inf)
        m_new = jnp.maximum(m, jnp.max(s, axis=-1, keepdims=True))
        alpha = jnp.exp(m - m_new)
        e = jnp.exp(s - m_new)
        l = alpha * l + jnp.sum(e, axis=-1, keepdims=True)
        acc = alpha * acc + _dot(e.astype(BF16), v)
        return m_new, l, acc

    init = (jnp.full((tq, 1), -jnp.inf, F32), jnp.zeros((tq, 1), F32), jnp.zeros((tq, V_HEAD), F32))
    m, l, acc = lax.fori_loop(0, qi + 1, body, init)
    o_ref[...] = (acc / l).astype(o_ref.dtype)


def _attn_prompt(q, k, v, B, T, tq_pref=256):
    H = MLA_HEADS
    tq = _tile(T, tq_pref)
    nq = T // tq
    return _pallas(
        functools.partial(_attn_kernel, tq=tq),
        out_shape=jax.ShapeDtypeStruct((B * T, H * V_HEAD), BF16),
        grid=(B, H, nq),
        in_specs=[pl.BlockSpec((tq, QK_PAD), lambda b, h, i: (b * nq + i, h)),
                  pl.BlockSpec((T, QK_PAD), lambda b, h, i: (b, h)),
                  pl.BlockSpec((T, V_HEAD), lambda b, h, i: (b, h))],
        out_specs=pl.BlockSpec((tq, V_HEAD), lambda b, h, i: (b * nq + i, h)),
        compiler_params=_cp("parallel", "parallel", "arbitrary"),
        name="attn_prompt",
    )(q, k, v)


def _absorb_kernel(q_ref, kng_ref, wuk_ref, qt_ref, qr_ref):
    g = kng_ref[...]
    for h in range(MLA_HEADS):
        o = h * QK_PAD
        qn = q_ref[:, o:o + QK_NOPE] * g
        wh = wuk_ref[:, h * QK_NOPE:(h + 1) * QK_NOPE]
        qt_ref[h] = _dot3(qn, wh, NT)
        qr_ref[h] = q_ref[:, o + QK_NOPE:o + QK_PAD]


def _absorb_q(q, kn_g, w_uk):
    M = q.shape[0]
    H = MLA_HEADS
    return _pallas(
        _absorb_kernel,
        out_shape=(jax.ShapeDtypeStruct((H, M, LANE), F32), jax.ShapeDtypeStruct((H, M, LANE), F32)),
        compiler_params=pltpu.CompilerParams(vmem_limit_bytes=VMEM_LIMIT),
        name="absorb_q",
    )(q, kn_g, w_uk)


def _paged_kernel(pt_ref, qt_ref, qr_ref, latn_ref, krn_ref, wuk_ref, ind_ref, lat_hbm, kr_hbm, o_ref,
                  latbuf, krbuf, sem, *, layer, n_pages, chunk_pages):
    b = pl.program_id(0)
    nb = pl.num_programs(0)
    slot = b % 2

    def page_copies(seq, sl, p):
        pg = pt_ref[seq * n_pages + p]
        return (pltpu.make_async_copy(lat_hbm.at[layer, pg], latbuf.at[sl, p], sem.at[sl, 0]),
                pltpu.make_async_copy(kr_hbm.at[layer, pg], krbuf.at[sl, p], sem.at[sl, 1]))

    def fetch(seq, sl):
        def body(p, c):
            for cp in page_copies(seq, sl, p):
                cp.start()
            return c
        lax.fori_loop(0, n_pages, body, 0)

    @pl.when(b == 0)
    def _():
        fetch(0, 0)

    @pl.when(b + 1 < nb)
    def _():
        fetch(b + 1, 1 - slot)

    def wait_body(p, c):
        for cp in page_copies(b, slot, p):
            cp.wait()
        return c
    lax.fori_loop(0, n_pages, wait_body, 0)

    qt = qt_ref[0].astype(BF16)
    qr = qr_ref[0][:, :QK_ROPE].astype(BF16)
    wuk = wuk_ref[...]
    ind = ind_ref[...]

    def scores(lat, kr):
        latb = lat.astype(BF16)
        x = _dot(latb, wuk)
        ms = _dot(ind, (x * x).astype(BF16), NT)
        rs = lax.rsqrt(ms / QK_NOPE + NORM_EPS)
        s = _dot(qt, latb, NT) * rs + _dot(qr, kr.astype(BF16), NT)
        return s * MLA_SCALE, latb

    def update(carry, s, latb):
        m, l, acc = carry
        m_new = jnp.maximum(m, jnp.max(s, axis=-1, keepdims=True))
        alpha = jnp.exp(m - m_new)
        e = jnp.exp(s - m_new)
        l = alpha * l + jnp.sum(e, axis=-1, keepdims=True)
        acc = alpha * acc + _dot(e.astype(BF16), latb)
        return m_new, l, acc

    rows = chunk_pages * PAGE

    def chunk(c, carry):
        p0 = pl.multiple_of(c * chunk_pages, chunk_pages)
        lat = latbuf[slot, pl.ds(p0, chunk_pages)].reshape(rows, latbuf.shape[-1])
        kr = krbuf[slot, pl.ds(p0, chunk_pages)].reshape(rows, krbuf.shape[-1])
        s, latb = scores(lat, kr)
        return update(carry, s, latb)

    H = MLA_HEADS
    init = (jnp.full((H, 1), -jnp.inf, F32), jnp.zeros((H, 1), F32), jnp.zeros((H, latbuf.shape[-1]), F32))
    carry = lax.fori_loop(0, n_pages // chunk_pages, chunk, init)
    s, latb = scores(latn_ref[0], krn_ref[0][:, :QK_ROPE])
    s = jnp.where(lax.broadcasted_iota(jnp.int32, s.shape, 1) == 0, s, -jnp.inf)
    m, l, acc = update(carry, s, latb)
    o_ref[0] = acc / l


def _paged_attention(page_table, qt, qr, lat_new, kr_new, w_uk, cache_latent, cache_krope, layer):
    DB, n_pages = page_table.shape
    H = MLA_HEADS
    R = cache_latent.shape[-1]
    chunk_pages = 8 if n_pages % 8 == 0 else 1
    ind = jnp.asarray(np.kron(np.eye(H, dtype=np.float32), np.ones((1, QK_NOPE), np.float32)), BF16)
    pad8 = lambda a: jnp.pad(a[:, None, :], ((0, 0), (0, 7), (0, 0)))
    seq = lambda w: pl.BlockSpec((1, 8, w), lambda b, pt: (b, 0, 0))
    grid_spec = pltpu.PrefetchScalarGridSpec(
        num_scalar_prefetch=1,
        grid=(DB,),
        in_specs=[seq(LANE), seq(LANE), seq(LANE), seq(LANE),
                  pl.BlockSpec(w_uk.shape, lambda b, pt: (0, 0)),
                  pl.BlockSpec(ind.shape, lambda b, pt: (0, 0)),
                  pl.BlockSpec(memory_space=pl.ANY),
                  pl.BlockSpec(memory_space=pl.ANY)],
        out_specs=pl.BlockSpec((1, H, R), lambda b, pt: (b, 0, 0)),
        scratch_shapes=[pltpu.VMEM((2, n_pages, PAGE, R), F32),
                        pltpu.VMEM((2, n_pages, PAGE, QK_ROPE), F32),
                        pltpu.SemaphoreType.DMA((2, 2))],
    )
    return _pallas(
        functools.partial(_paged_kernel, layer=layer, n_pages=n_pages, chunk_pages=chunk_pages),
        out_shape=jax.ShapeDtypeStruct((DB, H, R), F32),
        grid_spec=grid_spec,
        compiler_params=_cp("arbitrary"),
        name="paged_attention",
    )(page_table.reshape(-1), qt, qr, pad8(lat_new), pad8(kr_new), w_uk, ind, cache_latent, cache_krope)


def _uv_kernel(o_ref, wuv_ref, y_ref):
    for h in range(MLA_HEADS):
        y_ref[:, h * V_HEAD:(h + 1) * V_HEAD] = _dot(
            o_ref[h].astype(BF16), wuv_ref[:, h * V_HEAD:(h + 1) * V_HEAD]).astype(y_ref.dtype)


def _uv_out(o_lat, w_uv):
    H, M, _ = o_lat.shape
    return _pallas(
        _uv_kernel,
        out_shape=jax.ShapeDtypeStruct((M, H * V_HEAD), BF16),
        compiler_params=pltpu.CompilerParams(vmem_limit_bytes=VMEM_LIMIT),
        name="uv_out",
    )(o_lat, w_uv)


def _rw_prep_body(pb, prev, mu_ref, w0_ref, wup_ref, a0_ref, aup_ref, gup_ref, kk_ref, ka_ref, seg_ref,
                  r_o, wl_o, k_o, v_o, kk_o, b_o, g_o):
    W = RW_W
    xs = pb + (prev - pb) * mu_ref[...]
    r = xs[:, 0:W]
    k = xs[:, W:2 * W]
    v = xs[:, 2 * W:3 * W]
    wd = xs[:, 3 * W:3 * W + LANE]
    ad = xs[:, 3 * W + LANE:3 * W + 2 * LANE]
    gd = xs[:, 3 * W + 2 * LANE:]
    z = -(w0_ref[...] + _dot(jnp.tanh(wd).astype(BF16), wup_ref[...]))
    softplus = jnp.maximum(z, 0.0) + jnp.log1p(jnp.exp(-jnp.abs(z)))
    w = -softplus - 0.5
    a = _sigmoid(a0_ref[...] + _dot(ad.astype(BF16), aup_ref[...]))
    g = _dot(_sigmoid(gd).astype(BF16), gup_ref[...])
    kk = k * kk_ref[...]
    k2 = k * (1.0 + (a - 1.0) * ka_ref[...])
    n2 = _dot_sel_r(kk * kk, seg_ref[...])
    kkn = kk / jnp.maximum(jnp.sqrt(n2), 1e-12)
    r_o[...] = r
    wl_o[...] = -jnp.exp(w)
    k_o[...] = k2
    v_o[...] = v
    kk_o[...] = kkn
    b_o[...] = kkn * a
    g_o[...] = g


def _rw_prep_seq_kernel(pb_ref, halo_ref, *rest, tpg):
    pb = pb_ref[...]
    i = pl.program_id(0)
    rolled = pltpu.roll(pb, 1, 0)
    row = lax.broadcasted_iota(jnp.int32, pb.shape, 0)
    first_row = jnp.where(i % tpg == 0, 0, -1)
    prev = jnp.where(row == 0, halo_ref[7:8, :], rolled)
    prev = jnp.where(row == first_row, 0.0, prev)
    _rw_prep_body(pb, prev, *rest)


def _rw_prep_step_kernel(pb_ref, prev_ref, *rest):
    _rw_prep_body(pb_ref[...], prev_ref[...], *rest)


def _seg_matrix(n, w):
    return jnp.asarray(np.kron(np.eye(n // w, dtype=np.float32), np.ones((w, w), np.float32)), BF16)


def _rw_prep(p, prev, lw, rpg, tm_pref=256):
    M = p.shape[0]
    tm = _tile(rpg, tm_pref)
    tpg = rpg // tm
    seg = _seg_matrix(RW_W, RW_N)
    weights = (lw["rw_mu"], lw["rw_w0"], lw["rw_w_up"], lw["rw_a0"], lw["rw_a_up"], lw["rw_g_up"], lw["rw_k_k"],
               lw["rw_k_a"], seg)
    const = lambda a: pl.BlockSpec(a.shape, lambda i: (0,) * a.ndim)
    out = jax.ShapeDtypeStruct((M, RW_W), F32)
    ospec = pl.BlockSpec((tm, RW_W), lambda i: (i, 0))
    main = pl.BlockSpec((tm, PB_W), lambda i: (i, 0))
    if prev is None:
        kern = functools.partial(_rw_prep_seq_kernel, tpg=tpg)
        second = pl.BlockSpec((8, PB_W), lambda i: (jnp.maximum(i * (tm // 8) - 1, 0), 0))
        second_arg = p
    else:
        kern = _rw_prep_step_kernel
        second = pl.BlockSpec((tm, PB_W), lambda i: (i, 0))
        second_arg = prev
    return _pallas(
        kern,
        out_shape=(out,) * 7,
        grid=(M // tm,),
        in_specs=[main, second] + [const(a) for a in weights],
        out_specs=(ospec,) * 7,
        compiler_params=_cp("parallel"),
        name="rwkv_prep",
    )(p, second_arg, *weights)


def _rw_chunk_kernel(r_ref, wl_ref, k_ref, v_ref, kk_ref, b_ref, y_ref, s_ref, st_ref, *, C):
    c = pl.program_id(1)

    @pl.when(c == 0)
    def _():
        st_ref[...] = jnp.zeros_like(st_ref)

    N = RW_N
    ri = lax.broadcasted_iota(jnp.int32, (C, C), 0)
    ci = lax.broadcasted_iota(jnp.int32, (C, C), 1)
    incl = ci <= ri
    strict = ci < ri
    eye_c = (ci == ri).astype(F32)
    rn = lax.broadcasted_iota(jnp.int32, (N, N), 0)
    cn = lax.broadcasted_iota(jnp.int32, (N, N), 1)
    eye_n = rn == cn

    wl = wl_ref[...]
    cs = _dot_sel_l(incl.astype(BF16), wl)
    p_in = jnp.exp(cs)
    p_ex = jnp.exp(cs - wl)
    p_inv = jnp.exp(-cs)
    p_end = p_in[C - 1:C, :]
    at = -kk_ref[...] * p_ex
    rp = r_ref[...] * p_in
    bp = b_ref[...] * p_inv
    kp = k_ref[...] * p_inv
    bpp = bp * p_end
    kpp = kp * p_end
    v = v_ref[...]

    ys = []
    for h in range(RW_HEADS):
        sl = slice(h * N, (h + 1) * N)
        at_h, rp_h, bp_h, kp_h, v_h = at[:, sl], rp[:, sl], bp[:, sl], kp[:, sl], v[:, sl]
        a_ab = jnp.where(strict, _dot3(at_h, bp_h, NT), 0.0)
        a_ak = jnp.where(strict, _dot3(at_h, kp_h, NT), 0.0)
        a_rb = jnp.where(incl, _dot3(rp_h, bp_h, NT), 0.0)
        a_rk = jnp.where(incl, _dot3(rp_h, kp_h, NT), 0.0)
        t = eye_c
        m = 1
        while m < C:
            sh = m.bit_length() - 1
            low_left = ((ri >> (sh + 1)) == (ci >> (sh + 1))) & (((ri >> sh) & 1) == 1) & (((ci >> sh) & 1) == 0)
            a_off = jnp.where(low_left, a_ab, 0.0)
            if m == 1:
                t = t + a_off
            else:
                t = t + _dot3(t, _dot3(a_off, t))
            m *= 2
        w1 = _dot3(t, at_h)
        w2 = _dot3(t, _dot3(a_ak, v_h))
        g1 = rp_h + _dot3(a_rb, w1)
        g2 = _dot3(a_rb, w2) + _dot3(a_rk, v_h)
        m1 = jnp.where(eye_n, p_end[:, sl], 0.0) + _dot3(bpp[:, sl], w1, TN)
        m2 = _dot3(bpp[:, sl], w2, TN) + _dot3(kpp[:, sl], v_h, TN)
        st = st_ref[h]
        ys.append(_dot3(g1, st) + g2)
        st_ref[h] = _dot3(m1, st) + m2
    y_ref[...] = jnp.concatenate(ys, axis=1)

    @pl.when(c == pl.num_programs(1) - 1)
    def _():
        s_ref[0] = st_ref[...]


def _rw_chunked(ops, B, T):
    C = _tile(T, 64)
    nc = T // C
    blk = pl.BlockSpec((C, RW_W), lambda b, c: (b * nc + c, 0))
    return _pallas(
        functools.partial(_rw_chunk_kernel, C=C),
        out_shape=(jax.ShapeDtypeStruct((B * T, RW_W), F32),
                   jax.ShapeDtypeStruct((B, RW_HEADS, RW_N, RW_N), F32)),
        grid=(B, nc),
        in_specs=[blk] * 6,
        out_specs=(blk, pl.BlockSpec((1, RW_HEADS, RW_N, RW_N), lambda b, c: (b, 0, 0, 0))),
        scratch_shapes=[pltpu.VMEM((RW_HEADS, RW_N, RW_N), F32)],
        compiler_params=_cp("parallel", "arbitrary"),
        name="rwkv_chunked",
    )(*ops)


def _rw_step_kernel(s_ref, r_ref, wl_ref, k_ref, v_ref, kk_ref, b_ref, ej_ref, ei_ref, rd_ref, so_ref, y_ref):
    s = s_ref[0]
    ej = ej_ref[...]
    ei = ei_ref[...]
    rd = rd_ref[...]
    over_j = lambda x: _dot_sel_r(x, ej)
    sa = _dot_sel_r(s * over_j(kk_ref[0]), rd)
    s_new = (s * over_j(jnp.exp(wl_ref[0])) - _dot_sel_r(sa, ei) * over_j(b_ref[0])
             + _dot_sel_r(v_ref[0], ei) * over_j(k_ref[0]))
    so_ref[0] = s_new
    y_ref[0] = _dot_sel_r(s_new * over_j(r_ref[0]), rd)


def _rw_step(state_wkv, layer, vecs):
    _, DB, _ = state_wkv.shape
    H, N = RW_HEADS, RW_N
    bt = _tile(DB, 32)
    ej = jnp.asarray(np.kron(np.ones((1, N), np.float32), np.eye(N, dtype=np.float32)), BF16)
    ei = jnp.asarray(np.kron(np.eye(N, dtype=np.float32), np.ones((1, N), np.float32)), BF16)
    rd = ei.T
    sblk = pl.BlockSpec((1, bt, N * N), lambda i, h: (layer, i, h))
    oblk = pl.BlockSpec((1, bt, N * N), lambda i, h: (0, i, h))
    vblk = pl.BlockSpec((1, bt, N), lambda i, h: (h, i, 0))
    const = lambda a: pl.BlockSpec(a.shape, lambda i, h: (0, 0))
    s_new, y = _pallas(
        _rw_step_kernel,
        out_shape=(jax.ShapeDtypeStruct((1, DB, H * N * N), F32), jax.ShapeDtypeStruct((H, DB, N), F32)),
        grid=(DB // bt, H),
        in_specs=[sblk] + [vblk] * 6 + [const(ej), const(ei), const(rd)],
        out_specs=(oblk, vblk),
        compiler_params=_cp("parallel", "parallel"),
        name="rwkv_step",
    )(state_wkv, *vecs, ej, ei, rd)
    return s_new[0], y


def _rw_post_kernel(y_ref, r_ref, k_ref, v_ref, g_ref, lng_ref, lnb_ref, rk_ref, seg_ref, o_ref):
    seg = seg_ref[...]
    y = y_ref[...]
    mu = _dot_sel_r(y, seg) / RW_N
    d = y - mu
    var = _dot_sel_r(d * d, seg) / RW_N
    yn = (d * lax.rsqrt(var + GN_EPS)) * lng_ref[...] + lnb_ref[...]
    bonus = _dot_sel_r(r_ref[...] * k_ref[...] * rk_ref[...], seg) * v_ref[...]
    o_ref[...] = ((yn + bonus) * g_ref[...]).astype(o_ref.dtype)


def _rw_post(y, r, k, v, g, lw, tm_pref=256):
    M = y.shape[0]
    tm = _tile(M, tm_pref)
    seg = _seg_matrix(RW_W, RW_N)
    row = pl.BlockSpec((tm, RW_W), lambda i: (i, 0))
    const = lambda a: pl.BlockSpec(a.shape, lambda i: (0,) * a.ndim)
    weights = (lw["rw_lnx_g"], lw["rw_lnx_b"], lw["rw_r_k"], seg)
    return _pallas(
        _rw_post_kernel,
        out_shape=jax.ShapeDtypeStruct((M, RW_W), BF16),
        grid=(M // tm,),
        in_specs=[row] * 5 + [const(a) for a in weights],
        out_specs=row,
        compiler_params=_cp("parallel"),
        name="rwkv_post",
    )(y, r, k, v, g, *weights)


def _pool_windows(x_ext, halo_rows, count_of, wp_ref, ps_ref, o_ref):
    G = len(POOL_WINDOWS)
    gw = x_ext.shape[1] // G
    for gi, win in enumerate(POOL_WINDOWS):
        xg = x_ext[:, gi * gw:(gi + 1) * gw]
        s = xg
        step = 1
        while step < win:
            s = s + pltpu.roll(s, step, 0)
            step *= 2
        diff = s[halo_rows:] / count_of(win) - xg[halo_rows:]
        o = _dot(diff.astype(BF16), wp_ref[gi]) * ps_ref[:, gi * gw:(gi + 1) * gw]
        o_ref[:, gi * gw:(gi + 1) * gw] = o.astype(o_ref.dtype)


def _pool_seq_kernel(pc_ref, halo_ref, wp_ref, ps_ref, o_ref, *, tm, tpg):
    i = pl.program_id(0)
    t0 = (i % tpg) * tm
    hrow = lax.broadcasted_iota(jnp.int32, (halo_ref.shape[0], 1), 0)
    halo = jnp.where(hrow < jnp.where(t0 == 0, halo_ref.shape[0], 0), 0.0, halo_ref[...])
    x_ext = jnp.concatenate([halo, pc_ref[...]], axis=0)
    pos = t0 + lax.broadcasted_iota(jnp.int32, (tm, 1), 0)
    count_of = lambda win: jnp.minimum(pos + 1, win).astype(F32)
    _pool_windows(x_ext, halo_ref.shape[0], count_of, wp_ref, ps_ref, o_ref)


def _pool_prompt(p, w_pool, pool_scale, rpg, tm_pref=256):
    M = p.shape[0]
    W = 512
    HALO = 16
    tm = _tile(rpg, tm_pref)
    tpg = rpg // tm
    return _pallas(
        functools.partial(_pool_seq_kernel, tm=tm, tpg=tpg),
        out_shape=jax.ShapeDtypeStruct((M, W), BF16),
        grid=(M // tm,),
        in_specs=[pl.BlockSpec((tm, W), lambda i: (i, PC_OFF // W)),
                  pl.BlockSpec((HALO, W), lambda i: (jnp.maximum(i * (tm // HALO) - 1, 0), PC_OFF // W)),
                  pl.BlockSpec(w_pool.shape, lambda i: (0, 0, 0)),
                  pl.BlockSpec(pool_scale.shape, lambda i: (0, 0))],
        out_specs=pl.BlockSpec((tm, W), lambda i: (i, 0)),
        compiler_params=_cp("parallel"),
        name="pool_prompt",
    )(p, p, w_pool, pool_scale)


def _pool_step_kernel(st_ref, pc_ref, wp_ref, ps_ref, o_ref, *, past):
    G = len(POOL_WINDOWS)
    pc = pc_ref[...]
    gw = pc.shape[1] // G
    nbuf = st_ref.shape[0]
    for gi, win in enumerate(POOL_WINDOWS):
        sl = slice(gi * gw, (gi + 1) * gw)
        xg = pc[:, sl]
        total = xg
        for j in range(1, win):
            total = total + st_ref[nbuf - j][:, sl]
        diff = total / float(min(past + 1, win)) - xg
        o = _dot(diff.astype(BF16), wp_ref[gi]) * ps_ref[:, sl]
        o_ref[:, sl] = o.astype(o_ref.dtype)


def _pool_step(state_t, p, w_pool, pool_scale, past):
    nbuf, DB, W = state_t.shape
    return _pallas(
        functools.partial(_pool_step_kernel, past=past),
        out_shape=jax.ShapeDtypeStruct((DB, W), BF16),
        grid=(1,),
        in_specs=[pl.BlockSpec((nbuf, DB, W), lambda i: (0, 0, 0)),
                  pl.BlockSpec((DB, W), lambda i: (0, PC_OFF // W)),
                  pl.BlockSpec(w_pool.shape, lambda i: (0, 0, 0)),
                  pl.BlockSpec(pool_scale.shape, lambda i: (0, 0))],
        out_specs=pl.BlockSpec((DB, W), lambda i: (0, 0)),
        compiler_params=_cp("arbitrary"),
        name="pool_step",
    )(state_t, p, w_pool, pool_scale)


FFN_HALO = 16


def _ffn_up_seq_kernel(x_ref, xh_ref, g_ref, sc_ref, sh_ref, wv_ref, wg_ref, cwv_ref, cwg_ref, cbv_ref, cbg_ref,
                       act_ref, tv_ref, tg_ref, h_ref, *, tm, tpg):
    i = pl.program_id(0)

    @pl.when(pl.program_id(1) == 0)
    def _():
        g, sc, sh = g_ref[...], sc_ref[0], sh_ref[0]
        h_ref[pl.ds(0, FFN_HALO), :] = _modnorm(xh_ref[...], g, sc, sh).astype(BF16)
        h_ref[pl.ds(FFN_HALO, tm), :] = _modnorm(x_ref[...], g, sc, sh).astype(BF16)

    h = h_ref[...]
    row = lax.broadcasted_iota(jnp.int32, (tm + FFN_HALO, 1), 0)
    keep = row >= jnp.where(i % tpg == 0, FFN_HALO, 0)

    def branch(w_ref, cw_ref, cb_ref, t_ref):
        u = jnp.where(keep, _dot(h, w_ref[...]), 0.0)
        t_ref[0] = u[tm + FFN_HALO - 8:, :]
        cw = cw_ref[...]
        return (cb_ref[...] + pltpu.roll(u, 2, 0)[FFN_HALO:] * cw[0:1] + pltpu.roll(u, 1, 0)[FFN_HALO:] * cw[1:2]
                + u[FFN_HALO:] * cw[2:3])

    val = branch(wv_ref, cwv_ref, cbv_ref, tv_ref)
    gate = branch(wg_ref, cwg_ref, cbg_ref, tg_ref)
    act_ref[...] = ((gate * _sigmoid(gate)) * val).astype(act_ref.dtype)


def _ffn_up_prompt(x, g, sc, sh, w_up, conv_w, conv_b, B, T, tm_pref=512, tn_pref=512):
    M, D = x.shape
    F = w_up.shape[1] // 2
    tm = _tile(T, tm_pref)
    tpg = T // tm
    tn = _tile(F, tn_pref)
    nj = F // tn
    lo = lambda i, j: (0, j)
    hi = lambda i, j: (0, nj + j)
    tail = jax.ShapeDtypeStruct((M // tm, 8, F), F32)
    tspec = pl.BlockSpec((1, 8, tn), lambda i, j: (i, 0, j))
    return _pallas(
        functools.partial(_ffn_up_seq_kernel, tm=tm, tpg=tpg),
        out_shape=(jax.ShapeDtypeStruct((M, F), BF16), tail, tail),
        grid=(M // tm, nj),
        in_specs=[pl.BlockSpec((tm, D), lambda i, j: (i, 0)),
                  pl.BlockSpec((FFN_HALO, D), lambda i, j: (jnp.maximum(i * (tm // FFN_HALO) - 1, 0), 0)),
                  pl.BlockSpec((1, D), lambda i, j: (0, 0)),
                  _mod_spec(tm, D, tpg, 1, False),
                  _mod_spec(tm, D, tpg, 1, False),
                  pl.BlockSpec((D, tn), lo), pl.BlockSpec((D, tn), hi),
                  pl.BlockSpec((3, tn), lo), pl.BlockSpec((3, tn), hi),
                  pl.BlockSpec((1, tn), lo), pl.BlockSpec((1, tn), hi)],
        out_specs=(pl.BlockSpec((tm, tn), lambda i, j: (i, j)), tspec, tspec),
        scratch_shapes=[pltpu.VMEM((tm + FFN_HALO, D), BF16)],
        compiler_params=_cp("arbitrary", "arbitrary"),
        name="ffn_up_prompt",
    )(x, x, g, sc, sh, w_up, w_up, conv_w, conv_w, conv_b, conv_b)


def _ffn_up_step_kernel(x_ref, g_ref, sc_ref, sh_ref, wv_ref, wg_ref, cwv_ref, cwg_ref, cbv_ref, cbg_ref,
                        s0v_ref, s0g_ref, s1v_ref, s1g_ref, act_ref, uv_ref, ug_ref, h_ref):
    @pl.when(pl.program_id(0) == 0)
    def _():
        h_ref[...] = _modnorm(x_ref[...], g_ref[...], sc_ref[0], sh_ref[0]).astype(BF16)

    h = h_ref[...]

    def branch(w_ref, cw_ref, cb_ref, s0_ref, s1_ref, u_ref):
        u = _dot(h, w_ref[...])
        u_ref[...] = u
        cw = cw_ref[...]
        return cb_ref[...] + s0_ref[...] * cw[0:1] + s1_ref[...] * cw[1:2] + u * cw[2:3]

    val = branch(wv_ref, cwv_ref, cbv_ref, s0v_ref, s1v_ref, uv_ref)
    gate = branch(wg_ref, cwg_ref, cbg_ref, s0g_ref, s1g_ref, ug_ref)
    act_ref[...] = ((gate * _sigmoid(gate)) * val).astype(act_ref.dtype)


def _ffn_up_step(x, g, sc, sh, w_up, conv_w, conv_b, st0, st1, tn_pref=512):
    M, D = x.shape
    F = w_up.shape[1] // 2
    tn = _tile(F, tn_pref)
    nj = F // tn
    lo = lambda j: (0, j)
    hi = lambda j: (0, nj + j)
    full = lambda a: pl.BlockSpec(a.shape, lambda j: (0,) * a.ndim)
    u = jax.ShapeDtypeStruct((M, F), F32)
    ospec = pl.BlockSpec((M, tn), lo)
    return _pallas(
        _ffn_up_step_kernel,
        out_shape=(jax.ShapeDtypeStruct((M, F), BF16), u, u),
        grid=(nj,),
        in_specs=[full(x), full(g), full(sc), full(sh),
                  pl.BlockSpec((D, tn), lo), pl.BlockSpec((D, tn), hi),
                  pl.BlockSpec((3, tn), lo), pl.BlockSpec((3, tn), hi),
                  pl.BlockSpec((1, tn), lo), pl.BlockSpec((1, tn), hi),
                  pl.BlockSpec((M, tn), lo), pl.BlockSpec((M, tn), hi),
                  pl.BlockSpec((M, tn), lo), pl.BlockSpec((M, tn), hi)],
        out_specs=(ospec, ospec, ospec),
        scratch_shapes=[pltpu.VMEM((M, D), BF16)],
        compiler_params=_cp("arbitrary"),
        name="ffn_up_step",
    )(x, g, sc, sh, w_up, w_up, conv_w, conv_w, conv_b, conv_b, st0, st0, st1, st1)


def _in_proj_columns():
    a_cols = 512 + 128 + QK_ROPE
    b0 = a_cols
    sizes = (RW_W, DECAY_LORA, RW_W, RW_W, ICLR_LORA, GATE_LORA)
    offs = np.concatenate([[0], np.cumsum(sizes)])[:-1] + b0
    r0, wd0, k0, v0, ad0, gd0 = [int(o) for o in offs]
    b_cols = int(sum(sizes))
    c0 = b0 + b_cols
    pad = lambda n: [-1] * n
    rng = lambda s, n: list(range(s, s + n))
    cols = (rng(r0, RW_W) + rng(k0, RW_W) + rng(v0, RW_W)
            + rng(wd0, DECAY_LORA) + pad(LANE - DECAY_LORA)
            + rng(ad0, ICLR_LORA) + pad(LANE - ICLR_LORA)
            + rng(gd0, GATE_LORA)
            + rng(0, 512)
            + rng(c0, 512)
            + rng(512, 128)
            + rng(640, QK_ROPE) + pad(LANE - QK_ROPE))
    cols = np.asarray(cols, np.int32)
    assert cols.shape[0] == P_COLS
    pb_src = cols[:PB_W] - b0
    return cols, pb_src, b_cols


def _gather_cols(w, cols):
    valid = jnp.asarray(cols >= 0)
    return jnp.where(valid, jnp.take(w, jnp.asarray(np.maximum(cols, 0)), axis=-1), 0.0)


def _uq_columns():
    cols = []
    for h in range(MLA_HEADS):
        base = h * (QK_NOPE + QK_ROPE)
        cols += list(range(base, base + QK_NOPE + QK_ROPE)) + [-1] * (QK_PAD - QK_NOPE - QK_ROPE)
    return np.asarray(cols, np.int32)


def _pad_lanes(a, n):
    return jnp.pad(a, [(0, 0)] * (a.ndim - 1) + [(0, n - a.shape[-1])])


def _pad_rows(a, n):
    return jnp.pad(a, [(0, 0)] * (a.ndim - 2) + [(0, n - a.shape[-2]), (0, 0)])


def _rope_tables(pos):
    half = QK_ROPE // 2
    inv = jnp.power(ROPE_THETA, -jnp.arange(half, dtype=F32) * 2.0 / QK_ROPE)
    ang = pos.astype(F32)[:, None] * inv[None, :]
    cos, sin = jnp.cos(ang), jnp.sin(ang)
    z = jnp.zeros((pos.shape[0], LANE - QK_ROPE), F32)
    return jnp.concatenate([cos, cos, z], axis=1), jnp.concatenate([sin, sin, z], axis=1)


def kernel(x_prompt, x_sample, cache_latent, cache_krope, state_wkv, state_shift, state_pool, state_conv, page_table, c_prompt, c_sample, ada_w, ada_b, norm_mix_g, norm_ffn_g, w_in, q_lora_g, w_uq, kv_lora_g, w_uk, w_uv, qn_g, qr_g, kn_g, kr_g, rw_mu, rw_w0, rw_w_up, rw_a0, rw_a_up, rw_g_up, rw_k_k, rw_k_a, rw_r_k, rw_lnx_g, rw_lnx_b, w_pool, pool_scale, w_out, ffn_w_up, ffn_conv_w, ffn_conv_b, ffn_w_down):
    B, T, D = x_prompt.shape
    DB, TS, _ = x_sample.shape
    assert TS == 1, "the sample group decodes one token per sequence"
    L = ada_w.shape[0]
    n_pages = page_table.shape[1]
    past = n_pages * PAGE
    H, N = RW_HEADS, RW_N
    F = ffn_w_down.shape[1]

    cols, pb_src, b_cols = _in_proj_columns()
    pb_valid = np.nonzero(pb_src >= 0)[0]
    pb_inverse = pb_valid[np.argsort(pb_src[pb_valid])]

    w_in_p = _gather_cols(w_in, cols).astype(BF16)
    w_uq_p = _gather_cols(w_uq, _uq_columns()).astype(BF16)
    w_uk_f = w_uk.reshape(L, w_uk.shape[1], MLA_HEADS * QK_NOPE)
    w_uk_b = w_uk_f.astype(BF16)
    w_uv_b = w_uv.reshape(L, w_uv.shape[1], MLA_HEADS * V_HEAD).astype(BF16)
    w_out_b = w_out.astype(BF16)
    w_up_b = ffn_w_up.astype(BF16)
    w_down_b = ffn_w_down.astype(BF16)
    w_pool_b = w_pool.astype(BF16)
    mu_p = _gather_cols(rw_mu, pb_src)
    rw_w_up_p = _pad_rows(rw_w_up, LANE).astype(BF16)
    rw_a_up_p = _pad_rows(rw_a_up, LANE).astype(BF16)
    rw_g_up_b = rw_g_up.astype(BF16)
    qr_g_p = _pad_lanes(qr_g, LANE)
    kr_g_p = _pad_lanes(kr_g, LANE)
    shift_p = _gather_cols(state_shift, pb_src)
    row2 = lambda a, l: a[l].reshape(1, -1)

    mc = DB + B
    c_all = _pad_rows(jnp.concatenate([c_sample, c_prompt], axis=0), mc + (-mc) % 8)
    mod = _ada_mod(c_all, ada_w, ada_b)

    cos_p, sin_p = _rope_tables(jnp.arange(T, dtype=jnp.int32))
    cos_s, sin_s = _rope_tables(jnp.full((DB,), past, jnp.int32))

    state_wkv_flat = state_wkv.reshape(L, DB, H * N * N)

    xp = x_prompt.reshape(B * T, D)
    xs = x_sample.reshape(DB, D)
    outs = [[] for _ in range(12)]

    for l in range(L):
        lw = dict(q_lora_g=row2(q_lora_g, l), w_uq=w_uq_p[l], kv_lora_g=row2(kv_lora_g, l), qn_g=row2(qn_g, l),
                  qr_g=qr_g_p[l].reshape(1, -1), kn_g=row2(kn_g, l), kr_g=kr_g_p[l].reshape(1, -1),
                  w_uk=w_uk_b[l], w_uv=w_uv_b[l],
                  rw_mu=mu_p[l].reshape(1, -1), rw_w0=row2(rw_w0, l), rw_w_up=rw_w_up_p[l], rw_a0=row2(rw_a0, l),
                  rw_a_up=rw_a_up_p[l], rw_g_up=rw_g_up_b[l], rw_k_k=row2(rw_k_k, l), rw_k_a=row2(rw_k_a, l),
                  rw_r_k=row2(rw_r_k, l), rw_lnx_g=row2(rw_lnx_g, l), rw_lnx_b=row2(rw_lnx_b, l))
        g_mix, g_ffn = row2(norm_mix_g, l), row2(norm_ffn_g, l)
        ps = row2(pool_scale, l)
        cw, cb = ffn_conv_w[l], row2(ffn_conv_b, l)
        mods_p = [mod[l, DB:DB + B, k * D:(k + 1) * D][:, None, :] for k in range(6)]
        mods_s = [mod[l, :DB, k * D:(k + 1) * D][None] for k in range(6)]

        sh1, sc1, g1, sh2, sc2, g2 = mods_p
        p = _norm_mod_matmul(xp, g_mix, sc1, sh1, w_in_p[l], T)
        q, k, v, lat, kr = _mla_proj(p, cos_p, sin_p, lw, T, BF16)
        ya = _attn_prompt(q, k, v, B, T)
        r_, wl_, k_, v_, kk_, b_, g_ = _rw_prep(p, None, lw, T)
        y_raw, st_t = _rw_chunked((r_, wl_, k_, v_, kk_, b_), B, T)
        yb = _rw_post(y_raw, r_, k_, v_, g_, lw)
        yc = _pool_prompt(p, w_pool_b[l], ps, T)
        xp = _matmul_gate_res(jnp.concatenate([ya, yb, yc], axis=1), w_out_b[l], g1, xp, T)
        act, tail_v, tail_g = _ffn_up_prompt(xp, g_ffn, sc2, sh2, w_up_b[l], cw, cb, B, T)
        xp = _matmul_gate_res(act, w_down_b[l], g2, xp, T)

        p3 = p.reshape(B, T, P_COLS)
        outs[0].append(lat.reshape(B, T, LANE))
        outs[2].append(kr.reshape(B, T, LANE)[:, :, :QK_ROPE])
        outs[4].append(jnp.swapaxes(st_t, -1, -2))
        outs[6].append(jnp.take(p3[:, -1, :PB_W], jnp.asarray(pb_inverse), axis=-1))
        pool_rows = p3[:, max(T - POOL_BUF, 0):, PC_OFF:PC_OFF + 512]
        if T < POOL_BUF:
            pool_rows = jnp.concatenate([jnp.zeros((B, POOL_BUF - T, 512), F32), pool_rows], axis=1)
        outs[8].append(pool_rows)
        last = lambda t: t.reshape(B, -1, 8, F)[:, -1, 6:8]
        outs[10].append(jnp.concatenate([last(tail_v), last(tail_g)], axis=-1))

        sh1, sc1, g1, sh2, sc2, g2 = mods_s
        p = _norm_mod_matmul(xs, g_mix, sc1, sh1, w_in_p[l], DB)
        q, _, _, lat, kr = _mla_proj(p, cos_s, sin_s, lw, DB, F32)
        qt, qrp = _absorb_q(q, lw["kn_g"], w_uk_f[l])
        o_lat = _paged_attention(page_table, jnp.swapaxes(qt, 0, 1), jnp.swapaxes(qrp, 0, 1), lat, kr, lw["w_uk"],
                                 cache_latent, cache_krope, l)
        ya = _uv_out(jnp.swapaxes(o_lat, 0, 1), lw["w_uv"])
        r_, wl_, k_, v_, kk_, b_, g_ = _rw_prep(p, shift_p[l], lw, DB)
        heads = lambda a: jnp.swapaxes(a.reshape(DB, H, N), 0, 1)
        s_new, y_h = _rw_step(state_wkv_flat, l, [heads(a) for a in (r_, wl_, k_, v_, kk_, b_)])
        y_raw = jnp.swapaxes(y_h, 0, 1).reshape(DB, H * N)
        yb = _rw_post(y_raw, r_, k_, v_, g_, lw)
        yc = _pool_step(jnp.swapaxes(state_pool[l], 0, 1), p, w_pool_b[l], ps, past)
        xs = _matmul_gate_res(jnp.concatenate([ya, yb, yc], axis=1), w_out_b[l], g1, xs, DB)
        act, u_v, u_g = _ffn_up_step(xs, g_ffn, sc2, sh2, w_up_b[l], cw, cb, state_conv[l, :, 0], state_conv[l, :, 1])
        xs = _matmul_gate_res(act, w_down_b[l], g2, xs, DB)

        outs[1].append(lat.reshape(DB, 1, LANE))
        outs[3].append(kr.reshape(DB, 1, LANE)[:, :, :QK_ROPE])
        outs[5].append(s_new.reshape(DB, H, N, N))
        outs[7].append(jnp.take(p[:, :PB_W], jnp.asarray(pb_inverse), axis=-1))
        outs[9].append(jnp.concatenate([state_pool[l][:, 1:], p[:, None, PC_OFF:PC_OFF + 512]], axis=1))
        outs[11].append(jnp.stack([state_conv[l, :, 1], jnp.concatenate([u_v, u_g], axis=-1)], axis=1))

    stacked = [jnp.stack(o) for o in outs]
    return (xp.reshape(B, T, D), xs.reshape(DB, 1, D), *stacked)
```

```python
import functools

import numpy as np
import jax
import jax.numpy as jnp
from jax import lax
from jax.experimental import pallas as pl
from jax.experimental.pallas import tpu as pltpu

F32 = jnp.float32
BF16 = jnp.bfloat16

MLA_HEADS = 8
QK_NOPE = 128
QK_ROPE = 64
V_HEAD = 128
ROPE_THETA = 10000.0
MLA_SCALE = (QK_NOPE + QK_ROPE) ** -0.5
RW_HEADS = 8
RW_N = 64
DECAY_LORA = 96
ICLR_LORA = 96
GATE_LORA = 256
GN_EPS = 64e-5
NORM_EPS = 1e-6
POOL_WINDOWS = (2, 4, 8, 16)
POOL_BUF = 15
PAGE = 128

LANE = 128
QK_PAD = 256
VMEM_LIMIT = 48 * 1024 * 1024

RW_W = RW_HEADS * RW_N
PB_W = 2048
CQ_OFF = PB_W
PC_OFF = CQ_OFF + 512
CKV_OFF = PC_OFF + 512
KR_OFF = CKV_OFF + 128
P_COLS = KR_OFF + 128

NN = (((1,), (0,)), ((), ()))
NT = (((1,), (1,)), ((), ()))
TN = (((0,), (0,)), ((), ()))


def _cp(*sem):
    return pltpu.CompilerParams(dimension_semantics=sem, vmem_limit_bytes=VMEM_LIMIT)


def _pallas(body, **kw):
    call = pl.pallas_call(body, **kw)

    def run(*args):
        if any(isinstance(a, jax.core.Tracer) for a in args):
            args = [a if a.dtype == jnp.int32 else pltpu.with_memory_space_constraint(a, pltpu.HBM) for a in args]
        return call(*args)
    return run


def _tile(n, pref):
    if n <= pref:
        return n
    t = pref - pref % 8
    while t >= 8:
        if n % t == 0:
            return t
        t -= 8
    return n


def _dot(a, b, dims=NN):
    return lax.dot_general(a, b, dims, preferred_element_type=F32)


def _split2(x):
    hi = x.astype(BF16)
    lo = (x - hi.astype(F32)).astype(BF16)
    return hi, lo


def _split3(x):
    hi = x.astype(BF16)
    r1 = x - hi.astype(F32)
    mid = r1.astype(BF16)
    lo = (r1 - mid.astype(F32)).astype(BF16)
    return hi, mid, lo


def _dot3(a, b, dims=NN):
    (ca,), (cb,) = dims[0]
    ah = a.astype(BF16).astype(F32)
    bh = b.astype(BF16).astype(F32)
    a3 = jnp.concatenate([ah, a - ah, ah], axis=ca).astype(BF16)
    b3 = jnp.concatenate([bh, bh, b - bh], axis=cb).astype(BF16)
    return _dot(a3, b3, dims)


def _dot_sel_l(sel, x, dims=NN):
    h, m, l = _split3(x)
    return _dot(sel, h, dims) + (_dot(sel, m, dims) + _dot(sel, l, dims))


def _dot_sel_r(x, sel, dims=NN):
    h, m, l = _split3(x)
    return _dot(h, sel, dims) + (_dot(m, sel, dims) + _dot(l, sel, dims))


def _sigmoid(x):
    return 1.0 / (1.0 + jnp.exp(-x))


def _rms(x, g, n):
    return (x * lax.rsqrt(jnp.sum(x * x, axis=-1, keepdims=True) / n + NORM_EPS)) * g


def _modnorm(x, g, sc, sh):
    return _rms(x, g, x.shape[-1]) * (1.0 + sc) + sh


def _mod_spec(tm, tn, tpg, rows, col_arg):
    if rows == 1:
        if col_arg:
            return pl.BlockSpec((1, 1, tn), lambda i, j: (i // tpg, 0, j))
        return pl.BlockSpec((1, 1, tn), lambda i, j: (i // tpg, 0, 0))
    if col_arg:
        return pl.BlockSpec((1, tm, tn), lambda i, j: (i // tpg, i % tpg, j))
    return pl.BlockSpec((1, tm, tn), lambda i, j: (i // tpg, i % tpg, 0))


def _ada_kernel(c_ref, w_ref, b_ref, o_ref):
    c = c_ref[...]
    s = c * _sigmoid(c)
    o_ref[0] = _dot(s.astype(BF16), w_ref[0].astype(BF16)) + b_ref[0]


def _ada_mod(c_all, ada_w, ada_b):
    L, D, N = ada_w.shape
    mc = c_all.shape[0]
    tn = _tile(N, 1024)
    return _pallas(
        _ada_kernel,
        out_shape=jax.ShapeDtypeStruct((L, mc, N), F32),
        grid=(L, N // tn),
        in_specs=[pl.BlockSpec((mc, D), lambda l, j: (0, 0)),
                  pl.BlockSpec((1, D, tn), lambda l, j: (l, 0, j)),
                  pl.BlockSpec((1, 1, tn), lambda l, j: (l, 0, j))],
        out_specs=pl.BlockSpec((1, mc, tn), lambda l, j: (l, 0, j)),
        compiler_params=_cp("parallel", "parallel"),
        name="ada_mod",
    )(c_all, ada_w, ada_b.reshape(L, 1, N))


def _nmm_kernel(x_ref, g_ref, sc_ref, sh_ref, w_ref, o_ref, h_ref):
    @pl.when(pl.program_id(1) == 0)
    def _():
        h_ref[...] = _modnorm(x_ref[...], g_ref[...], sc_ref[0], sh_ref[0]).astype(BF16)

    o_ref[...] = _dot(h_ref[...], w_ref[...])


def _norm_mod_matmul(x, g, sc, sh, w, rpg, tm_pref=512):
    M, D = x.shape
    N = w.shape[1]
    tm = _tile(rpg, tm_pref)
    tpg = rpg // tm
    tn = N // 2 if (N // 2) % LANE == 0 else N
    rows = sc.shape[1]
    return _pallas(
        _nmm_kernel,
        out_shape=jax.ShapeDtypeStruct((M, N), F32),
        grid=(M // tm, N // tn),
        in_specs=[pl.BlockSpec((tm, D), lambda i, j: (i, 0)),
                  pl.BlockSpec((1, D), lambda i, j: (0, 0)),
                  _mod_spec(tm, D, tpg, rows, False),
                  _mod_spec(tm, D, tpg, rows, False),
                  pl.BlockSpec((D, tn), lambda i, j: (0, j))],
        out_specs=pl.BlockSpec((tm, tn), lambda i, j: (i, j)),
        scratch_shapes=[pltpu.VMEM((tm, D), BF16)],
        compiler_params=_cp("parallel", "arbitrary"),
        name="norm_mod_matmul",
    )(x, g, sc, sh, w)


def _mmres_kernel(a_ref, w_ref, gate_ref, res_ref, o_ref):
    o_ref[...] = res_ref[...] + gate_ref[0] * _dot(a_ref[...], w_ref[...])


def _matmul_gate_res(a, w, gate, res, rpg, tm_pref=512, tn_pref=512):
    M, K = a.shape
    N = w.shape[1]
    tm = _tile(rpg, tm_pref)
    tpg = rpg // tm
    tn = _tile(N, tn_pref)
    rows = gate.shape[1]
    return _pallas(
        _mmres_kernel,
        out_shape=jax.ShapeDtypeStruct((M, N), F32),
        grid=(M // tm, N // tn),
        in_specs=[pl.BlockSpec((tm, K), lambda i, j: (i, 0)),
                  pl.BlockSpec((K, tn), lambda i, j: (0, j)),
                  _mod_spec(tm, tn, tpg, rows, True),
                  pl.BlockSpec((tm, tn), lambda i, j: (i, j))],
        out_specs=pl.BlockSpec((tm, tn), lambda i, j: (i, j)),
        compiler_params=_cp("parallel", "arbitrary"),
        name="matmul_gate_res",
    )(a, w, gate, res)


def _mla_proj_kernel(cq_ref, ckv_ref, kr_ref, cos_ref, sin_ref, qlg_ref, wuq_ref, kvg_ref, qng_ref, qrg_ref,
                     kng_ref, krg_ref, wuk_ref, wuv_ref, q_ref, k_ref, v_ref, lat_ref, kro_ref):
    cos = cos_ref[...]
    sin = sin_ref[...]
    lane = lax.broadcasted_iota(jnp.int32, cos.shape, 1)
    half = QK_ROPE // 2

    def rope(x):
        rot = jnp.where(lane < half, -pltpu.roll(x, LANE - half, 1), pltpu.roll(x, half, 1))
        return x * cos + rot * sin

    cqn = _rms(cq_ref[...], qlg_ref[...], cq_ref.shape[-1])
    q = _dot(cqn.astype(BF16), wuq_ref[...])
    lat = _rms(ckv_ref[...], kvg_ref[...], ckv_ref.shape[-1])
    lat_ref[...] = lat
    kr = rope(_rms(kr_ref[...], krg_ref[...], QK_ROPE))
    kro_ref[...] = kr
    latb = lat.astype(BF16)
    kx = _dot(latb, wuk_ref[...])
    v_ref[...] = _dot(latb, wuv_ref[...]).astype(v_ref.dtype)
    krb = kr.astype(k_ref.dtype)
    for h in range(MLA_HEADS):
        o = h * QK_PAD
        qn = _rms(q[:, o:o + QK_NOPE], qng_ref[...], QK_NOPE)
        qr = rope(_rms(q[:, o + QK_NOPE:o + QK_PAD], qrg_ref[...], QK_ROPE))
        q_ref[:, o:o + QK_NOPE] = qn.astype(q_ref.dtype)
        q_ref[:, o + QK_NOPE:o + QK_PAD] = qr.astype(q_ref.dtype)
        kn = _rms(kx[:, h * QK_NOPE:(h + 1) * QK_NOPE], kng_ref[...], QK_NOPE)
        k_ref[:, o:o + QK_NOPE] = kn.astype(k_ref.dtype)
        k_ref[:, o + QK_NOPE:o + QK_PAD] = krb


def _mla_proj(p, cos_t, sin_t, lw, rpg, q_dtype, tm_pref=256):
    M = p.shape[0]
    tm = _tile(rpg, tm_pref)
    tpg = rpg // tm
    H = MLA_HEADS
    row = lambda w: pl.BlockSpec((tm, w), lambda i: (i, 0))
    const = lambda a: pl.BlockSpec(a.shape, lambda i: (0,) * a.ndim)
    tab = pl.BlockSpec((tm, LANE), lambda i: (i % tpg, 0))
    weights = (lw["q_lora_g"], lw["w_uq"], lw["kv_lora_g"], lw["qn_g"], lw["qr_g"], lw["kn_g"], lw["kr_g"],
               lw["w_uk"], lw["w_uv"])
    return _pallas(
        _mla_proj_kernel,
        out_shape=(jax.ShapeDtypeStruct((M, H * QK_PAD), q_dtype),
                   jax.ShapeDtypeStruct((M, H * QK_PAD), BF16),
                   jax.ShapeDtypeStruct((M, H * V_HEAD), BF16),
                   jax.ShapeDtypeStruct((M, LANE), F32),
                   jax.ShapeDtypeStruct((M, LANE), F32)),
        grid=(M // tm,),
        in_specs=[pl.BlockSpec((tm, 512), lambda i: (i, CQ_OFF // 512)),
                  pl.BlockSpec((tm, LANE), lambda i: (i, CKV_OFF // LANE)),
                  pl.BlockSpec((tm, LANE), lambda i: (i, KR_OFF // LANE)),
                  tab, tab] + [const(a) for a in weights],
        out_specs=(row(H * QK_PAD), row(H * QK_PAD), row(H * V_HEAD), row(LANE), row(LANE)),
        compiler_params=_cp("parallel"),
        name="mla_proj",
    )(p, p, p, cos_t, sin_t, *weights)


def _attn_kernel(q_ref, k_ref, v_ref, o_ref, *, tq):
    qi = pl.program_id(2)
    q = q_ref[...]
    row = qi * tq + lax.broadcasted_iota(jnp.int32, (tq, tq), 0)
    col0 = lax.broadcasted_iota(jnp.int32, (tq, tq), 1)

    def body(kj, carry):
        m, l, acc = carry
        start = pl.multiple_of(kj * tq, tq)
        k = k_ref[pl.ds(start, tq), :]
        v = v_ref[pl.ds(start, tq), :]
        s = _dot(q, k, NT) * MLA_SCALE
        s = jnp.where(col0 + kj * tq <= row, s, -jnp.inf)
        m_new = jnp.maximum(m, jnp.max(s, axis=-1, keepdims=True))
        alpha = jnp.exp(m - m_new)
        e = jnp.exp(s - m_new)
        l = alpha * l + jnp.sum(e, axis=-1, keepdims=True)
        acc = alpha * acc + _dot(e.astype(BF16), v)
        return m_new, l, acc

    init = (jnp.full((tq, 1), -jnp.inf, F32), jnp.zeros((tq, 1), F32), jnp.zeros((tq, V_HEAD), F32))
    m, l, acc = lax.fori_loop(0, qi + 1, body, init)
    o_ref[...] = (acc / l).astype(o_ref.dtype)


def _attn_prompt(q, k, v, B, T, tq_pref=512):
    H = MLA_HEADS
    tq = _tile(T, tq_pref)
    nq = T // tq
    return _pallas(
        functools.partial(_attn_kernel, tq=tq),
        out_shape=jax.ShapeDtypeStruct((B * T, H * V_HEAD), BF16),
        grid=(B, H, nq),
        in_specs=[pl.BlockSpec((tq, QK_PAD), lambda b, h, i: (b * nq + i, h)),
                  pl.BlockSpec((T, QK_PAD), lambda b, h, i: (b, h)),
                  pl.BlockSpec((T, V_HEAD), lambda b, h, i: (b, h))],
        out_specs=pl.BlockSpec((tq, V_HEAD), lambda b, h, i: (b * nq + i, h)),
        compiler_params=_cp("parallel", "parallel", "arbitrary"),
        name="attn_prompt",
    )(q, k, v)


def _absorb_kernel(q_ref, kng_ref, wuk_ref, qt_ref, qr_ref):
    g = kng_ref[...]
    for h in range(MLA_HEADS):
        o = h * QK_PAD
        qn = q_ref[:, o:o + QK_NOPE] * g
        wh = wuk_ref[:, h * QK_NOPE:(h + 1) * QK_NOPE]
        qt_ref[h] = _dot3(qn, wh, NT)
        qr_ref[h] = q_ref[:, o + QK_NOPE:o + QK_PAD]


def _absorb_q(q, kn_g, w_uk):
    M = q.shape[0]
    H = MLA_HEADS
    return _pallas(
        _absorb_kernel,
        out_shape=(jax.ShapeDtypeStruct((H, M, LANE), F32), jax.ShapeDtypeStruct((H, M, LANE), F32)),
        compiler_params=pltpu.CompilerParams(vmem_limit_bytes=VMEM_LIMIT),
        name="absorb_q",
    )(q, kn_g, w_uk)


def _paged_kernel(pt_ref, qt_ref, qr_ref, latn_ref, krn_ref, wukt_ref, ind_ref, lat_hbm, kr_hbm, o_ref,
                  latbuf, krbuf, lhs_ref, sem, *, layer, n_pages, chunk_pages):
    b = pl.program_id(0)
    nb = pl.num_programs(0)
    slot = b % 2
    H = MLA_HEADS
    HD = H * QK_NOPE

    def page_copies(seq, sl, p):
        pg = pt_ref[seq * n_pages + p]
        return (pltpu.make_async_copy(lat_hbm.at[layer, pg], latbuf.at[sl, p], sem.at[sl, 0]),
                pltpu.make_async_copy(kr_hbm.at[layer, pg],
                                      krbuf.at[sl, :, pl.ds(pl.multiple_of(p * PAGE, PAGE), PAGE)], sem.at[sl, 1]))

    def fetch(seq, sl):
        def body(p, c):
            for cp in page_copies(seq, sl, p):
                cp.start()
            return c
        lax.fori_loop(0, n_pages, body, 0)

    @pl.when(b == 0)
    def _():
        fetch(0, 0)

    @pl.when(b + 1 < nb)
    def _():
        fetch(b + 1, 1 - slot)

    def wait_body(p, c):
        for cp in page_copies(b, slot, p):
            cp.wait()
        return c
    lax.fori_loop(0, n_pages, wait_body, 0)

    lhs_ref[pl.ds(0, HD), :] = wukt_ref[...]
    lhs_ref[pl.ds(HD, 16), :] = jnp.concatenate([qt_ref[0], jnp.zeros_like(qt_ref[0])], axis=0).astype(BF16)
    lhs = lhs_ref[...]
    qr = qr_ref[0][:, :QK_ROPE].astype(BF16)
    ind = ind_ref[...]

    def latent_scores(latb):
        xt = _dot(lhs, latb, NT)
        parts = []
        for h in range(H):
            acc = None
            for i in range(QK_NOPE // 8):
                blk = xt[h * QK_NOPE + 8 * i:h * QK_NOPE + 8 * i + 8]
                acc = blk * blk if acc is None else acc + blk * blk
            parts.append(acc)
        ms = _dot_sel_l(ind, jnp.concatenate(parts, axis=0))
        return xt[HD:HD + H] * lax.rsqrt(ms / QK_NOPE + NORM_EPS)

    def update(carry, s, latb):
        m, l, acc = carry
        m_new = jnp.maximum(m, jnp.max(s, axis=-1, keepdims=True))
        alpha = jnp.exp(m - m_new)
        e = jnp.exp(s - m_new)
        l = alpha * l + jnp.sum(e, axis=-1, keepdims=True)
        acc = alpha * acc + _dot(e.astype(BF16), latb)
        return m_new, l, acc

    rows = chunk_pages * PAGE
    R = latbuf.shape[-1]

    def chunk(c, carry):
        p0 = pl.multiple_of(c * chunk_pages, chunk_pages)
        latb = latbuf[slot, pl.ds(p0, chunk_pages)].reshape(rows, R).astype(BF16)
        krt = krbuf[slot, :, pl.ds(pl.multiple_of(c * rows, rows), rows)].astype(BF16)
        s = (latent_scores(latb) + _dot(qr, krt)) * MLA_SCALE
        return update(carry, s, latb)

    init = (jnp.full((H, 1), -jnp.inf, F32), jnp.zeros((H, 1), F32), jnp.zeros((H, R), F32))
    carry = lax.fori_loop(0, n_pages // chunk_pages, chunk, init)
    latn = latn_ref[0]
    latb = jnp.concatenate([latn, jnp.zeros((PAGE - latn.shape[0], R), F32)], axis=0).astype(BF16)
    krn = krn_ref[0][0:1, :QK_ROPE].astype(BF16).astype(F32)
    s_rope = jnp.sum(qr.astype(F32) * krn, axis=-1, keepdims=True)
    s = (latent_scores(latb) + s_rope) * MLA_SCALE
    s = jnp.where(lax.broadcasted_iota(jnp.int32, s.shape, 1) == 0, s, -jnp.inf)
    m, l, acc = update(carry, s, latb)
    o_ref[0] = acc / l


def _paged_attention(page_table, qt, qr, lat_new, kr_new, w_uk_t, cache_latent, cache_krope_t, layer):
    DB, n_pages = page_table.shape
    H = MLA_HEADS
    R = cache_latent.shape[-1]
    chunk_pages = 8 if n_pages % 8 == 0 else 1
    ind = jnp.asarray(np.kron(np.eye(H, dtype=np.float32), np.ones((1, 8), np.float32)), BF16)
    pad8 = lambda a: jnp.pad(a[:, None, :], ((0, 0), (0, 7), (0, 0)))
    seq = lambda w: pl.BlockSpec((1, 8, w), lambda b, pt: (b, 0, 0))
    grid_spec = pltpu.PrefetchScalarGridSpec(
        num_scalar_prefetch=1,
        grid=(DB,),
        in_specs=[seq(LANE), seq(LANE), seq(LANE), seq(LANE),
                  pl.BlockSpec(w_uk_t.shape, lambda b, pt: (0, 0)),
                  pl.BlockSpec(ind.shape, lambda b, pt: (0, 0)),
                  pl.BlockSpec(memory_space=pl.ANY),
                  pl.BlockSpec(memory_space=pl.ANY)],
        out_specs=pl.BlockSpec((1, H, R), lambda b, pt: (b, 0, 0)),
        scratch_shapes=[pltpu.VMEM((2, n_pages, PAGE, R), F32),
                        pltpu.VMEM((2, QK_ROPE, n_pages * PAGE), F32),
                        pltpu.VMEM((w_uk_t.shape[0] + 16, R), BF16),
                        pltpu.SemaphoreType.DMA((2, 2))],
    )
    return _pallas(
        functools.partial(_paged_kernel, layer=layer, n_pages=n_pages, chunk_pages=chunk_pages),
        out_shape=jax.ShapeDtypeStruct((DB, H, R), F32),
        grid_spec=grid_spec,
        compiler_params=_cp("arbitrary"),
        name="paged_attention",
    )(page_table.reshape(-1), qt, qr, pad8(lat_new), pad8(kr_new), w_uk_t, ind, cache_latent, cache_krope_t)


def _uv_kernel(o_ref, wuv_ref, y_ref):
    for h in range(MLA_HEADS):
        y_ref[:, h * V_HEAD:(h + 1) * V_HEAD] = _dot(
            o_ref[h].astype(BF16), wuv_ref[:, h * V_HEAD:(h + 1) * V_HEAD]).astype(y_ref.dtype)


def _uv_out(o_lat, w_uv):
    H, M, _ = o_lat.shape
    return _pallas(
        _uv_kernel,
        out_shape=jax.ShapeDtypeStruct((M, H * V_HEAD), BF16),
        compiler_params=pltpu.CompilerParams(vmem_limit_bytes=VMEM_LIMIT),
        name="uv_out",
    )(o_lat, w_uv)


def _rw_prep_body(pb, prev, mu_ref, w0_ref, wup_ref, a0_ref, aup_ref, gup_ref, kk_ref, ka_ref, seg_ref,
                  r_o, wl_o, k_o, v_o, kk_o, b_o, g_o):
    W = RW_W
    xs = pb + (prev - pb) * mu_ref[...]
    r = xs[:, 0:W]
    k = xs[:, W:2 * W]
    v = xs[:, 2 * W:3 * W]
    wd = xs[:, 3 * W:3 * W + LANE]
    ad = xs[:, 3 * W + LANE:3 * W + 2 * LANE]
    gd = xs[:, 3 * W + 2 * LANE:]
    z = -(w0_ref[...] + _dot(jnp.tanh(wd).astype(BF16), wup_ref[...]))
    softplus = jnp.maximum(z, 0.0) + jnp.log1p(jnp.exp(-jnp.abs(z)))
    w = -softplus - 0.5
    a = _sigmoid(a0_ref[...] + _dot(ad.astype(BF16), aup_ref[...]))
    g = _dot(_sigmoid(gd).astype(BF16), gup_ref[...])
    kk = k * kk_ref[...]
    k2 = k * (1.0 + (a - 1.0) * ka_ref[...])
    n2 = _dot_sel_r(kk * kk, seg_ref[...])
    kkn = kk / jnp.maximum(jnp.sqrt(n2), 1e-12)
    r_o[...] = r
    wl_o[...] = -jnp.exp(w)
    k_o[...] = k2
    v_o[...] = v
    kk_o[...] = kkn
    b_o[...] = kkn * a
    g_o[...] = g


def _rw_prep_seq_kernel(pb_ref, halo_ref, *rest, tpg):
    pb = pb_ref[...]
    i = pl.program_id(0)
    rolled = pltpu.roll(pb, 1, 0)
    row = lax.broadcasted_iota(jnp.int32, pb.shape, 0)
    first_row = jnp.where(i % tpg == 0, 0, -1)
    prev = jnp.where(row == 0, halo_ref[7:8, :], rolled)
    prev = jnp.where(row == first_row, 0.0, prev)
    _rw_prep_body(pb, prev, *rest)


def _rw_prep_step_kernel(pb_ref, prev_ref, *rest):
    _rw_prep_body(pb_ref[...], prev_ref[...], *rest)


def _seg_matrix(n, w):
    return jnp.asarray(np.kron(np.eye(n // w, dtype=np.float32), np.ones((w, w), np.float32)), BF16)


def _rw_prep(p, prev, lw, rpg, tm_pref=256):
    M = p.shape[0]
    tm = _tile(rpg, tm_pref)
    tpg = rpg // tm
    seg = _seg_matrix(RW_W, RW_N)
    weights = (lw["rw_mu"], lw["rw_w0"], lw["rw_w_up"], lw["rw_a0"], lw["rw_a_up"], lw["rw_g_up"], lw["rw_k_k"],
               lw["rw_k_a"], seg)
    const = lambda a: pl.BlockSpec(a.shape, lambda i: (0,) * a.ndim)
    out = jax.ShapeDtypeStruct((M, RW_W), F32)
    ospec = pl.BlockSpec((tm, RW_W), lambda i: (i, 0))
    main = pl.BlockSpec((tm, PB_W), lambda i: (i, 0))
    if prev is None:
        kern = functools.partial(_rw_prep_seq_kernel, tpg=tpg)
        second = pl.BlockSpec((8, PB_W), lambda i: (jnp.maximum(i * (tm // 8) - 1, 0), 0))
        second_arg = p
    else:
        kern = _rw_prep_step_kernel
        second = pl.BlockSpec((tm, PB_W), lambda i: (i, 0))
        second_arg = prev
    return _pallas(
        kern,
        out_shape=(out,) * 7,
        grid=(M // tm,),
        in_specs=[main, second] + [const(a) for a in weights],
        out_specs=(ospec,) * 7,
        compiler_params=_cp("parallel"),
        name="rwkv_prep",
    )(p, second_arg, *weights)


def _rw_chunk_kernel(r_ref, wl_ref, k_ref, v_ref, kk_ref, b_ref, y_ref, s_ref, st_ref, *, C):
    c = pl.program_id(1)

    @pl.when(c == 0)
    def _():
        st_ref[...] = jnp.zeros_like(st_ref)

    N = RW_N
    ri = lax.broadcasted_iota(jnp.int32, (C, C), 0)
    ci = lax.broadcasted_iota(jnp.int32, (C, C), 1)
    incl = ci <= ri
    strict = ci < ri
    eye_c = (ci == ri).astype(F32)
    rn = lax.broadcasted_iota(jnp.int32, (N, N), 0)
    cn = lax.broadcasted_iota(jnp.int32, (N, N), 1)
    eye_n = rn == cn

    wl = wl_ref[...]
    cs = _dot_sel_l(incl.astype(BF16), wl)
    p_in = jnp.exp(cs)
    p_ex = jnp.exp(cs - wl)
    p_inv = jnp.exp(-cs)
    p_end = p_in[C - 1:C, :]
    at = -kk_ref[...] * p_ex
    rp = r_ref[...] * p_in
    bp = b_ref[...] * p_inv
    kp = k_ref[...] * p_inv
    bpp = bp * p_end
    kpp = kp * p_end
    v = v_ref[...]

    heads = range(RW_HEADS)
    cut = lambda x: [x[:, h * N:(h + 1) * N] for h in heads]
    at_h, rp_h, bp_h, kp_h, v_h, bpp_h, kpp_h, pe_h = (cut(x) for x in (at, rp, bp, kp, v, bpp, kpp, p_end))
    lhs = [jnp.concatenate([at_h[h], rp_h[h]], axis=0) for h in heads]
    gb = [_dot3(lhs[h], bp_h[h], NT) for h in heads]
    gk = [_dot3(lhs[h], kp_h[h], NT) for h in heads]
    a_ab = [jnp.where(strict, gb[h][:C], 0.0) for h in heads]
    a_rb = [jnp.where(incl, gb[h][C:], 0.0) for h in heads]
    a_ak = [jnp.where(strict, gk[h][:C], 0.0) for h in heads]
    a_rk = [jnp.where(incl, gk[h][C:], 0.0) for h in heads]
    akv = [_dot3(a_ak[h], v_h[h]) for h in heads]
    rkv = [_dot3(a_rk[h], v_h[h]) for h in heads]
    kv = [_dot3(kpp_h[h], v_h[h], TN) for h in heads]
    t = None
    m = 1
    while m < C:
        sh = m.bit_length() - 1
        low_left = ((ri >> (sh + 1)) == (ci >> (sh + 1))) & (((ri >> sh) & 1) == 1) & (((ci >> sh) & 1) == 0)
        a_off = [jnp.where(low_left, a_ab[h], 0.0) for h in heads]
        if m == 1:
            t = [eye_c + a_off[h] for h in heads]
        else:
            inner = [_dot3(a_off[h], t[h]) for h in heads]
            t = [t[h] + _dot3(t[h], inner[h]) for h in heads]
        m *= 2
    w = [_dot3(t[h], jnp.concatenate([at_h[h], akv[h]], axis=1)) for h in heads]
    g = [_dot3(a_rb[h], w[h]) for h in heads]
    mm = [_dot3(bpp_h[h], w[h], TN) for h in heads]
    top = [jnp.concatenate([rp_h[h] + g[h][:, :N], jnp.where(eye_n, pe_h[h], 0.0) + mm[h][:, :N]], axis=0)
           for h in heads]
    res = [_dot3(top[h], st_ref[h]) for h in heads]
    for h in heads:
        st_ref[h] = res[h][C:] + (mm[h][:, N:] + kv[h])
    y_ref[...] = jnp.concatenate([res[h][:C] + (g[h][:, N:] + rkv[h]) for h in heads], axis=1)

    @pl.when(c == pl.num_programs(1) - 1)
    def _():
        s_ref[0] = st_ref[...]


def _rw_chunked(ops, B, T):
    C = _tile(T, 64)
    nc = T // C
    blk = pl.BlockSpec((C, RW_W), lambda b, c: (b * nc + c, 0))
    return _pallas(
        functools.partial(_rw_chunk_kernel, C=C),
        out_shape=(jax.ShapeDtypeStruct((B * T, RW_W), F32),
                   jax.ShapeDtypeStruct((B, RW_HEADS, RW_N, RW_N), F32)),
        grid=(B, nc),
        in_specs=[blk] * 6,
        out_specs=(blk, pl.BlockSpec((1, RW_HEADS, RW_N, RW_N), lambda b, c: (b, 0, 0, 0))),
        scratch_shapes=[pltpu.VMEM((RW_HEADS, RW_N, RW_N), F32)],
        compiler_params=_cp("parallel", "arbitrary"),
        name="rwkv_chunked",
    )(*ops)


def _rw_step_kernel(s_ref, r_ref, wl_ref, k_ref, v_ref, kk_ref, b_ref, ej_ref, ei_ref, rd_ref, so_ref, y_ref):
    s = s_ref[0]
    ej = ej_ref[...]
    ei = ei_ref[...]
    rd = rd_ref[...]
    over_j = lambda x: _dot_sel_r(x, ej)
    sa = _dot_sel_r(s * over_j(kk_ref[0]), rd)
    s_new = (s * over_j(jnp.exp(wl_ref[0])) - _dot_sel_r(sa, ei) * over_j(b_ref[0])
             + _dot_sel_r(v_ref[0], ei) * over_j(k_ref[0]))
    so_ref[0] = s_new
    y_ref[0] = _dot_sel_r(s_new * over_j(r_ref[0]), rd)


def _rw_step(state_wkv, layer, vecs):
    _, DB, _ = state_wkv.shape
    H, N = RW_HEADS, RW_N
    bt = _tile(DB, 32)
    ej = jnp.asarray(np.kron(np.ones((1, N), np.float32), np.eye(N, dtype=np.float32)), BF16)
    ei = jnp.asarray(np.kron(np.eye(N, dtype=np.float32), np.ones((1, N), np.float32)), BF16)
    rd = ei.T
    sblk = pl.BlockSpec((1, bt, N * N), lambda i, h: (layer, i, h))
    oblk = pl.BlockSpec((1, bt, N * N), lambda i, h: (0, i, h))
    vblk = pl.BlockSpec((1, bt, N), lambda i, h: (h, i, 0))
    const = lambda a: pl.BlockSpec(a.shape, lambda i, h: (0, 0))
    s_new, y = _pallas(
        _rw_step_kernel,
        out_shape=(jax.ShapeDtypeStruct((1, DB, H * N * N), F32), jax.ShapeDtypeStruct((H, DB, N), F32)),
        grid=(DB // bt, H),
        in_specs=[sblk] + [vblk] * 6 + [const(ej), const(ei), const(rd)],
        out_specs=(oblk, vblk),
        compiler_params=_cp("parallel", "parallel"),
        name="rwkv_step",
    )(state_wkv, *vecs, ej, ei, rd)
    return s_new[0], y


def _rw_post_kernel(y_ref, r_ref, k_ref, v_ref, g_ref, lng_ref, lnb_ref, rk_ref, seg_ref, o_ref):
    seg = seg_ref[...]
    y = y_ref[...]
    mu = _dot_sel_r(y, seg) / RW_N
    d = y - mu
    var = _dot_sel_r(d * d, seg) / RW_N
    yn = (d * lax.rsqrt(var + GN_EPS)) * lng_ref[...] + lnb_ref[...]
    bonus = _dot_sel_r(r_ref[...] * k_ref[...] * rk_ref[...], seg) * v_ref[...]
    o_ref[...] = ((yn + bonus) * g_ref[...]).astype(o_ref.dtype)


def _rw_post(y, r, k, v, g, lw, tm_pref=256):
    M = y.shape[0]
    tm = _tile(M, tm_pref)
    seg = _seg_matrix(RW_W, RW_N)
    row = pl.BlockSpec((tm, RW_W), lambda i: (i, 0))
    const = lambda a: pl.BlockSpec(a.shape, lambda i: (0,) * a.ndim)
    weights = (lw["rw_lnx_g"], lw["rw_lnx_b"], lw["rw_r_k"], seg)
    return _pallas(
        _rw_post_kernel,
        out_shape=jax.ShapeDtypeStruct((M, RW_W), BF16),
        grid=(M // tm,),
        in_specs=[row] * 5 + [const(a) for a in weights],
        out_specs=row,
        compiler_params=_cp("parallel"),
        name="rwkv_post",
    )(y, r, k, v, g, *weights)


def _pool_windows(x_ext, halo_rows, count_of, wp_ref, ps_ref, o_ref):
    G = len(POOL_WINDOWS)
    gw = x_ext.shape[1] // G
    for gi, win in enumerate(POOL_WINDOWS):
        xg = x_ext[:, gi * gw:(gi + 1) * gw]
        s = xg
        step = 1
        while step < win:
            s = s + pltpu.roll(s, step, 0)
            step *= 2
        diff = s[halo_rows:] / count_of(win) - xg[halo_rows:]
        o = _dot(diff.astype(BF16), wp_ref[gi]) * ps_ref[:, gi * gw:(gi + 1) * gw]
        o_ref[:, gi * gw:(gi + 1) * gw] = o.astype(o_ref.dtype)


def _pool_seq_kernel(pc_ref, halo_ref, wp_ref, ps_ref, o_ref, *, tm, tpg):
    i = pl.program_id(0)
    t0 = (i % tpg) * tm
    hrow = lax.broadcasted_iota(jnp.int32, (halo_ref.shape[0], 1), 0)
    halo = jnp.where(hrow < jnp.where(t0 == 0, halo_ref.shape[0], 0), 0.0, halo_ref[...])
    x_ext = jnp.concatenate([halo, pc_ref[...]], axis=0)
    pos = t0 + lax.broadcasted_iota(jnp.int32, (tm, 1), 0)
    count_of = lambda win: jnp.minimum(pos + 1, win).astype(F32)
    _pool_windows(x_ext, halo_ref.shape[0], count_of, wp_ref, ps_ref, o_ref)


def _pool_prompt(p, w_pool, pool_scale, rpg, tm_pref=256):
    M = p.shape[0]
    W = 512
    HALO = 16
    tm = _tile(rpg, tm_pref)
    tpg = rpg // tm
    return _pallas(
        functools.partial(_pool_seq_kernel, tm=tm, tpg=tpg),
        out_shape=jax.ShapeDtypeStruct((M, W), BF16),
        grid=(M // tm,),
        in_specs=[pl.BlockSpec((tm, W), lambda i: (i, PC_OFF // W)),
                  pl.BlockSpec((HALO, W), lambda i: (jnp.maximum(i * (tm // HALO) - 1, 0), PC_OFF // W)),
                  pl.BlockSpec(w_pool.shape, lambda i: (0, 0, 0)),
                  pl.BlockSpec(pool_scale.shape, lambda i: (0, 0))],
        out_specs=pl.BlockSpec((tm, W), lambda i: (i, 0)),
        compiler_params=_cp("parallel"),
        name="pool_prompt",
    )(p, p, w_pool, pool_scale)


def _pool_step_kernel(st_ref, pc_ref, wp_ref, ps_ref, o_ref, *, past):
    G = len(POOL_WINDOWS)
    pc = pc_ref[...]
    gw = pc.shape[1] // G
    nbuf = st_ref.shape[0]
    for gi, win in enumerate(POOL_WINDOWS):
        sl = slice(gi * gw, (gi + 1) * gw)
        xg = pc[:, sl]
        total = xg
        for j in range(1, win):
            total = total + st_ref[nbuf - j][:, sl]
        diff = total / float(min(past + 1, win)) - xg
        o = _dot(diff.astype(BF16), wp_ref[gi]) * ps_ref[:, sl]
        o_ref[:, sl] = o.astype(o_ref.dtype)


def _pool_step(state_t, p, w_pool, pool_scale, past):
    nbuf, DB, W = state_t.shape
    return _pallas(
        functools.partial(_pool_step_kernel, past=past),
        out_shape=jax.ShapeDtypeStruct((DB, W), BF16),
        grid=(1,),
        in_specs=[pl.BlockSpec((nbuf, DB, W), lambda i: (0, 0, 0)),
                  pl.BlockSpec((DB, W), lambda i: (0, PC_OFF // W)),
                  pl.BlockSpec(w_pool.shape, lambda i: (0, 0, 0)),
                  pl.BlockSpec(pool_scale.shape, lambda i: (0, 0))],
        out_specs=pl.BlockSpec((DB, W), lambda i: (0, 0)),
        compiler_params=_cp("arbitrary"),
        name="pool_step",
    )(state_t, p, w_pool, pool_scale)


FFN_HALO = 16


def _ffn_up_seq_kernel(x_ref, xh_ref, g_ref, sc_ref, sh_ref, wv_ref, wg_ref, cwv_ref, cwg_ref, cbv_ref, cbg_ref,
                       act_ref, tv_ref, tg_ref, h_ref, *, tm, tpg):
    i = pl.program_id(0)

    @pl.when(pl.program_id(1) == 0)
    def _():
        g, sc, sh = g_ref[...], sc_ref[0], sh_ref[0]
        h_ref[pl.ds(0, FFN_HALO), :] = _modnorm(xh_ref[...], g, sc, sh).astype(BF16)
        h_ref[pl.ds(FFN_HALO, tm), :] = _modnorm(x_ref[...], g, sc, sh).astype(BF16)

    h = h_ref[...]
    row = lax.broadcasted_iota(jnp.int32, (tm + FFN_HALO, 1), 0)
    keep = row >= jnp.where(i % tpg == 0, FFN_HALO, 0)

    def branch(w_ref, cw_ref, cb_ref, t_ref):
        u = jnp.where(keep, _dot(h, w_ref[...]), 0.0)
        t_ref[0] = u[tm + FFN_HALO - 8:, :]
        cw = cw_ref[...]
        return (cb_ref[...] + pltpu.roll(u, 2, 0)[FFN_HALO:] * cw[0:1] + pltpu.roll(u, 1, 0)[FFN_HALO:] * cw[1:2]
                + u[FFN_HALO:] * cw[2:3])

    val = branch(wv_ref, cwv_ref, cbv_ref, tv_ref)
    gate = branch(wg_ref, cwg_ref, cbg_ref, tg_ref)
    act_ref[...] = ((gate * _sigmoid(gate)) * val).astype(act_ref.dtype)


def _ffn_up_prompt(x, g, sc, sh, w_up, conv_w, conv_b, B, T, tm_pref=512, tn_pref=512):
    M, D = x.shape
    F = w_up.shape[1] // 2
    tm = _tile(T, tm_pref)
    tpg = T // tm
    tn = _tile(F, tn_pref)
    nj = F // tn
    lo = lambda i, j: (0, j)
    hi = lambda i, j: (0, nj + j)
    tail = jax.ShapeDtypeStruct((M // tm, 8, F), F32)
    tspec = pl.BlockSpec((1, 8, tn), lambda i, j: (i, 0, j))
    return _pallas(
        functools.partial(_ffn_up_seq_kernel, tm=tm, tpg=tpg),
        out_shape=(jax.ShapeDtypeStruct((M, F), BF16), tail, tail),
        grid=(M // tm, nj),
        in_specs=[pl.BlockSpec((tm, D), lambda i, j: (i, 0)),
                  pl.BlockSpec((FFN_HALO, D), lambda i, j: (jnp.maximum(i * (tm // FFN_HALO) - 1, 0), 0)),
                  pl.BlockSpec((1, D), lambda i, j: (0, 0)),
                  _mod_spec(tm, D, tpg, 1, False),
                  _mod_spec(tm, D, tpg, 1, False),
                  pl.BlockSpec((D, tn), lo), pl.BlockSpec((D, tn), hi),
                  pl.BlockSpec((3, tn), lo), pl.BlockSpec((3, tn), hi),
                  pl.BlockSpec((1, tn), lo), pl.BlockSpec((1, tn), hi)],
        out_specs=(pl.BlockSpec((tm, tn), lambda i, j: (i, j)), tspec, tspec),
        scratch_shapes=[pltpu.VMEM((tm + FFN_HALO, D), BF16)],
        compiler_params=_cp("arbitrary", "arbitrary"),
        name="ffn_up_prompt",
    )(x, x, g, sc, sh, w_up, w_up, conv_w, conv_w, conv_b, conv_b)


def _ffn_up_step_kernel(x_ref, g_ref, sc_ref, sh_ref, wv_ref, wg_ref, cwv_ref, cwg_ref, cbv_ref, cbg_ref,
                        s0v_ref, s0g_ref, s1v_ref, s1g_ref, act_ref, uv_ref, ug_ref, h_ref):
    @pl.when(pl.program_id(0) == 0)
    def _():
        h_ref[...] = _modnorm(x_ref[...], g_ref[...], sc_ref[0], sh_ref[0]).astype(BF16)

    h = h_ref[...]

    def branch(w_ref, cw_ref, cb_ref, s0_ref, s1_ref, u_ref):
        u = _dot(h, w_ref[...])
        u_ref[...] = u
        cw = cw_ref[...]
        return cb_ref[...] + s0_ref[...] * cw[0:1] + s1_ref[...] * cw[1:2] + u * cw[2:3]

    val = branch(wv_ref, cwv_ref, cbv_ref, s0v_ref, s1v_ref, uv_ref)
    gate = branch(wg_ref, cwg_ref, cbg_ref, s0g_ref, s1g_ref, ug_ref)
    act_ref[...] = ((gate * _sigmoid(gate)) * val).astype(act_ref.dtype)


def _ffn_up_step(x, g, sc, sh, w_up, conv_w, conv_b, st0, st1, tn_pref=512):
    M, D = x.shape
    F = w_up.shape[1] // 2
    tn = _tile(F, tn_pref)
    nj = F // tn
    lo = lambda j: (0, j)
    hi = lambda j: (0, nj + j)
    full = lambda a: pl.BlockSpec(a.shape, lambda j: (0,) * a.ndim)
    u = jax.ShapeDtypeStruct((M, F), F32)
    ospec = pl.BlockSpec((M, tn), lo)
    return _pallas(
        _ffn_up_step_kernel,
        out_shape=(jax.ShapeDtypeStruct((M, F), BF16), u, u),
        grid=(nj,),
        in_specs=[full(x), full(g), full(sc), full(sh),
                  pl.BlockSpec((D, tn), lo), pl.BlockSpec((D, tn), hi),
                  pl.BlockSpec((3, tn), lo), pl.BlockSpec((3, tn), hi),
                  pl.BlockSpec((1, tn), lo), pl.BlockSpec((1, tn), hi),
                  pl.BlockSpec((M, tn), lo), pl.BlockSpec((M, tn), hi),
                  pl.BlockSpec((M, tn), lo), pl.BlockSpec((M, tn), hi)],
        out_specs=(ospec, ospec, ospec),
        scratch_shapes=[pltpu.VMEM((M, D), BF16)],
        compiler_params=_cp("arbitrary"),
        name="ffn_up_step",
    )(x, g, sc, sh, w_up, w_up, conv_w, conv_w, conv_b, conv_b, st0, st0, st1, st1)


def _in_proj_columns():
    a_cols = 512 + 128 + QK_ROPE
    b0 = a_cols
    sizes = (RW_W, DECAY_LORA, RW_W, RW_W, ICLR_LORA, GATE_LORA)
    offs = np.concatenate([[0], np.cumsum(sizes)])[:-1] + b0
    r0, wd0, k0, v0, ad0, gd0 = [int(o) for o in offs]
    b_cols = int(sum(sizes))
    c0 = b0 + b_cols
    pad = lambda n: [-1] * n
    rng = lambda s, n: list(range(s, s + n))
    cols = (rng(r0, RW_W) + rng(k0, RW_W) + rng(v0, RW_W)
            + rng(wd0, DECAY_LORA) + pad(LANE - DECAY_LORA)
            + rng(ad0, ICLR_LORA) + pad(LANE - ICLR_LORA)
            + rng(gd0, GATE_LORA)
            + rng(0, 512)
            + rng(c0, 512)
            + rng(512, 128)
            + rng(640, QK_ROPE) + pad(LANE - QK_ROPE))
    cols = np.asarray(cols, np.int32)
    assert cols.shape[0] == P_COLS
    pb_src = cols[:PB_W] - b0
    return cols, pb_src, b_cols


def _gather_cols(w, cols):
    valid = jnp.asarray(cols >= 0)
    return jnp.where(valid, jnp.take(w, jnp.asarray(np.maximum(cols, 0)), axis=-1), 0.0)


def _uq_columns():
    cols = []
    for h in range(MLA_HEADS):
        base = h * (QK_NOPE + QK_ROPE)
        cols += list(range(base, base + QK_NOPE + QK_ROPE)) + [-1] * (QK_PAD - QK_NOPE - QK_ROPE)
    return np.asarray(cols, np.int32)


def _pad_lanes(a, n):
    return jnp.pad(a, [(0, 0)] * (a.ndim - 1) + [(0, n - a.shape[-1])])


def _pad_rows(a, n):
    return jnp.pad(a, [(0, 0)] * (a.ndim - 2) + [(0, n - a.shape[-2]), (0, 0)])


def _rope_tables(pos):
    half = QK_ROPE // 2
    inv = jnp.power(ROPE_THETA, -jnp.arange(half, dtype=F32) * 2.0 / QK_ROPE)
    ang = pos.astype(F32)[:, None] * inv[None, :]
    cos, sin = jnp.cos(ang), jnp.sin(ang)
    z = jnp.zeros((pos.shape[0], LANE - QK_ROPE), F32)
    return jnp.concatenate([cos, cos, z], axis=1), jnp.concatenate([sin, sin, z], axis=1)


def kernel(x_prompt, x_sample, cache_latent, cache_krope, state_wkv, state_shift, state_pool, state_conv, page_table, c_prompt, c_sample, ada_w, ada_b, norm_mix_g, norm_ffn_g, w_in, q_lora_g, w_uq, kv_lora_g, w_uk, w_uv, qn_g, qr_g, kn_g, kr_g, rw_mu, rw_w0, rw_w_up, rw_a0, rw_a_up, rw_g_up, rw_k_k, rw_k_a, rw_r_k, rw_lnx_g, rw_lnx_b, w_pool, pool_scale, w_out, ffn_w_up, ffn_conv_w, ffn_conv_b, ffn_w_down):
    B, T, D = x_prompt.shape
    DB, TS, _ = x_sample.shape
    assert TS == 1, "the sample group decodes one token per sequence"
    L = ada_w.shape[0]
    n_pages = page_table.shape[1]
    past = n_pages * PAGE
    H, N = RW_HEADS, RW_N
    F = ffn_w_down.shape[1]

    cols, pb_src, b_cols = _in_proj_columns()
    pb_valid = np.nonzero(pb_src >= 0)[0]
    pb_inverse = pb_valid[np.argsort(pb_src[pb_valid])]

    w_in_p = _gather_cols(w_in, cols).astype(BF16)
    w_uq_p = _gather_cols(w_uq, _uq_columns()).astype(BF16)
    w_uk_f = w_uk.reshape(L, w_uk.shape[1], MLA_HEADS * QK_NOPE)
    w_uk_b = w_uk_f.astype(BF16)
    w_uk_t = jnp.swapaxes(w_uk_b, 1, 2)
    cache_krope_t = jnp.swapaxes(cache_krope, 2, 3)
    w_uv_b = w_uv.reshape(L, w_uv.shape[1], MLA_HEADS * V_HEAD).astype(BF16)
    w_out_b = w_out.astype(BF16)
    w_up_b = ffn_w_up.astype(BF16)
    w_down_b = ffn_w_down.astype(BF16)
    w_pool_b = w_pool.astype(BF16)
    mu_p = _gather_cols(rw_mu, pb_src)
    rw_w_up_p = _pad_rows(rw_w_up, LANE).astype(BF16)
    rw_a_up_p = _pad_rows(rw_a_up, LANE).astype(BF16)
    rw_g_up_b = rw_g_up.astype(BF16)
    qr_g_p = _pad_lanes(qr_g, LANE)
    kr_g_p = _pad_lanes(kr_g, LANE)
    shift_p = _gather_cols(state_shift, pb_src)
    row2 = lambda a, l: a[l].reshape(1, -1)

    mc = DB + B
    c_all = _pad_rows(jnp.concatenate([c_sample, c_prompt], axis=0), mc + (-mc) % 8)
    mod = _ada_mod(c_all, ada_w, ada_b)

    cos_p, sin_p = _rope_tables(jnp.arange(T, dtype=jnp.int32))
    cos_s, sin_s = _rope_tables(jnp.full((DB,), past, jnp.int32))

    state_wkv_flat = state_wkv.reshape(L, DB, H * N * N)

    xp = x_prompt.reshape(B * T, D)
    xs = x_sample.reshape(DB, D)
    outs = [[] for _ in range(12)]

    for l in range(L):
        lw = dict(q_lora_g=row2(q_lora_g, l), w_uq=w_uq_p[l], kv_lora_g=row2(kv_lora_g, l), qn_g=row2(qn_g, l),
                  qr_g=qr_g_p[l].reshape(1, -1), kn_g=row2(kn_g, l), kr_g=kr_g_p[l].reshape(1, -1),
                  w_uk=w_uk_b[l], w_uv=w_uv_b[l],
                  rw_mu=mu_p[l].reshape(1, -1), rw_w0=row2(rw_w0, l), rw_w_up=rw_w_up_p[l], rw_a0=row2(rw_a0, l),
                  rw_a_up=rw_a_up_p[l], rw_g_up=rw_g_up_b[l], rw_k_k=row2(rw_k_k, l), rw_k_a=row2(rw_k_a, l),
                  rw_r_k=row2(rw_r_k, l), rw_lnx_g=row2(rw_lnx_g, l), rw_lnx_b=row2(rw_lnx_b, l))
        g_mix, g_ffn = row2(norm_mix_g, l), row2(norm_ffn_g, l)
        ps = row2(pool_scale, l)
        cw, cb = ffn_conv_w[l], row2(ffn_conv_b, l)
        mods_p = [mod[l, DB:DB + B, k * D:(k + 1) * D][:, None, :] for k in range(6)]
        mods_s = [mod[l, :DB, k * D:(k + 1) * D][None] for k in range(6)]

        sh1, sc1, g1, sh2, sc2, g2 = mods_p
        p = _norm_mod_matmul(xp, g_mix, sc1, sh1, w_in_p[l], T)
        q, k, v, lat, kr = _mla_proj(p, cos_p, sin_p, lw, T, BF16)
        ya = _attn_prompt(q, k, v, B, T)
        r_, wl_, k_, v_, kk_, b_, g_ = _rw_prep(p, None, lw, T)
        y_raw, st_t = _rw_chunked((r_, wl_, k_, v_, kk_, b_), B, T)
        yb = _rw_post(y_raw, r_, k_, v_, g_, lw)
        yc = _pool_prompt(p, w_pool_b[l], ps, T)
        xp = _matmul_gate_res(jnp.concatenate([ya, yb, yc], axis=1), w_out_b[l], g1, xp, T)
        act, tail_v, tail_g = _ffn_up_prompt(xp, g_ffn, sc2, sh2, w_up_b[l], cw, cb, B, T)
        xp = _matmul_gate_res(act, w_down_b[l], g2, xp, T)

        p3 = p.reshape(B, T, P_COLS)
        outs[0].append(lat.reshape(B, T, LANE))
        outs[2].append(kr.reshape(B, T, LANE)[:, :, :QK_ROPE])
        outs[4].append(jnp.swapaxes(st_t, -1, -2))
        outs[6].append(jnp.take(p3[:, -1, :PB_W], jnp.asarray(pb_inverse), axis=-1))
        pool_rows = p3[:, max(T - POOL_BUF, 0):, PC_OFF:PC_OFF + 512]
        if T < POOL_BUF:
            pool_rows = jnp.concatenate([jnp.zeros((B, POOL_BUF - T, 512), F32), pool_rows], axis=1)
        outs[8].append(pool_rows)
        last = lambda t: t.reshape(B, -1, 8, F)[:, -1, 6:8]
        outs[10].append(jnp.concatenate([last(tail_v), last(tail_g)], axis=-1))

        sh1, sc1, g1, sh2, sc2, g2 = mods_s
        p = _norm_mod_matmul(xs, g_mix, sc1, sh1, w_in_p[l], DB)
        q, _, _, lat, kr = _mla_proj(p, cos_s, sin_s, lw, DB, F32)
        qt, qrp = _absorb_q(q, lw["kn_g"], w_uk_f[l])
        o_lat = _paged_attention(page_table, jnp.swapaxes(qt, 0, 1), jnp.swapaxes(qrp, 0, 1), lat, kr, w_uk_t[l],
                                 cache_latent, cache_krope_t, l)
        ya = _uv_out(jnp.swapaxes(o_lat, 0, 1), lw["w_uv"])
        r_, wl_, k_, v_, kk_, b_, g_ = _rw_prep(p, shift_p[l], lw, DB)
        heads = lambda a: jnp.swapaxes(a.reshape(DB, H, N), 0, 1)
        s_new, y_h = _rw_step(state_wkv_flat, l, [heads(a) for a in (r_, wl_, k_, v_, kk_, b_)])
        y_raw = jnp.swapaxes(y_h, 0, 1).reshape(DB, H * N)
        yb = _rw_post(y_raw, r_, k_, v_, g_, lw)
        yc = _pool_step(jnp.swapaxes(state_pool[l], 0, 1), p, w_pool_b[l], ps, past)
        xs = _matmul_gate_res(jnp.concatenate([ya, yb, yc], axis=1), w_out_b[l], g1, xs, DB)
        act, u_v, u_g = _ffn_up_step(xs, g_ffn, sc2, sh2, w_up_b[l], cw, cb, state_conv[l, :, 0], state_conv[l, :, 1])
        xs = _matmul_gate_res(act, w_down_b[l], g2, xs, DB)

        outs[1].append(lat.reshape(DB, 1, LANE))
        outs[3].append(kr.reshape(DB, 1, LANE)[:, :, :QK_ROPE])
        outs[5].append(s_new.reshape(DB, H, N, N))
        outs[7].append(jnp.take(p[:, :PB_W], jnp.asarray(pb_inverse), axis=-1))
        outs[9].append(jnp.concatenate([state_pool[l][:, 1:], p[:, None, PC_OFF:PC_OFF + 512]], axis=1))
        outs[11].append(jnp.stack([state_conv[l, :, 1], jnp.concatenate([u_v, u_g], axis=-1)], axis=1))

    stacked = [jnp.stack(o) for o in outs]
    return (xp.reshape(B, T, D), xs.reshape(DB, 1, D), *stacked)
```

```python
import functools

import numpy as np
import jax
import jax.numpy as jnp
from jax import lax
from jax.experimental import pallas as pl
from jax.experimental.pallas import tpu as pltpu

F32 = jnp.float32
BF16 = jnp.bfloat16

MLA_HEADS = 8
QK_NOPE = 128
QK_ROPE = 64
V_HEAD = 128
ROPE_THETA = 10000.0
MLA_SCALE = (QK_NOPE + QK_ROPE) ** -0.5
RW_HEADS = 8
RW_N = 64
DECAY_LORA = 96
ICLR_LORA = 96
GATE_LORA = 256
GN_EPS = 64e-5
NORM_EPS = 1e-6
POOL_WINDOWS = (2, 4, 8, 16)
POOL_BUF = 15
PAGE = 128

LANE = 128
QK_PAD = 256
VMEM_LIMIT = 48 * 1024 * 1024

RW_W = RW_HEADS * RW_N
PB_W = 2048
CQ_OFF = PB_W
PC_OFF = CQ_OFF + 512
CKV_OFF = PC_OFF + 512
KR_OFF = CKV_OFF + 128
P_COLS = KR_OFF + 128

NN = (((1,), (0,)), ((), ()))
NT = (((1,), (1,)), ((), ()))
TN = (((0,), (0,)), ((), ()))


def _cp(*sem):
    return pltpu.CompilerParams(dimension_semantics=sem, vmem_limit_bytes=VMEM_LIMIT)


def _pallas(body, **kw):
    call = pl.pallas_call(body, **kw)

    def run(*args):
        if any(isinstance(a, jax.core.Tracer) for a in args):
            args = [a if a.dtype == jnp.int32 else pltpu.with_memory_space_constraint(a, pltpu.HBM) for a in args]
        return call(*args)
    return run


def _tile(n, pref):
    if n <= pref:
        return n
    t = pref - pref % 8
    while t >= 8:
        if n % t == 0:
            return t
        t -= 8
    return n


def _dot(a, b, dims=NN):
    return lax.dot_general(a, b, dims, preferred_element_type=F32)


def _split2(x):
    hi = x.astype(BF16)
    lo = (x - hi.astype(F32)).astype(BF16)
    return hi, lo


def _split3(x):
    hi = x.astype(BF16)
    r1 = x - hi.astype(F32)
    mid = r1.astype(BF16)
    lo = (r1 - mid.astype(F32)).astype(BF16)
    return hi, mid, lo


def _dot3(a, b, dims=NN):
    (ca,), (cb,) = dims[0]
    ah = a.astype(BF16).astype(F32)
    bh = b.astype(BF16).astype(F32)
    a3 = jnp.concatenate([ah, a - ah, ah], axis=ca).astype(BF16)
    b3 = jnp.concatenate([bh, bh, b - bh], axis=cb).astype(BF16)
    return _dot(a3, b3, dims)


def _dot_sel_l(sel, x, dims=NN):
    h, m, l = _split3(x)
    return _dot(sel, h, dims) + (_dot(sel, m, dims) + _dot(sel, l, dims))


def _dot_sel_r(x, sel, dims=NN):
    h, m, l = _split3(x)
    return _dot(h, sel, dims) + (_dot(m, sel, dims) + _dot(l, sel, dims))


def _sigmoid(x):
    return 1.0 / (1.0 + jnp.exp(-x))


def _rms(x, g, n):
    return (x * lax.rsqrt(jnp.sum(x * x, axis=-1, keepdims=True) / n + NORM_EPS)) * g


def _modnorm(x, g, sc, sh):
    return _rms(x, g, x.shape[-1]) * (1.0 + sc) + sh


def _mod_spec(tm, tn, tpg, rows, col_arg):
    if rows == 1:
        if col_arg:
            return pl.BlockSpec((1, 1, tn), lambda i, j: (i // tpg, 0, j))
        return pl.BlockSpec((1, 1, tn), lambda i, j: (i // tpg, 0, 0))
    if col_arg:
        return pl.BlockSpec((1, tm, tn), lambda i, j: (i // tpg, i % tpg, j))
    return pl.BlockSpec((1, tm, tn), lambda i, j: (i // tpg, i % tpg, 0))


def _ada_kernel(c_ref, w_ref, b_ref, o_ref):
    c = c_ref[...]
    s = c * _sigmoid(c)
    o_ref[0] = _dot(s.astype(BF16), w_ref[0].astype(BF16)) + b_ref[0]


def _ada_mod(c_all, ada_w, ada_b):
    L, D, N = ada_w.shape
    mc = c_all.shape[0]
    tn = _tile(N, 1024)
    return _pallas(
        _ada_kernel,
        out_shape=jax.ShapeDtypeStruct((L, mc, N), F32),
        grid=(L, N // tn),
        in_specs=[pl.BlockSpec((mc, D), lambda l, j: (0, 0)),
                  pl.BlockSpec((1, D, tn), lambda l, j: (l, 0, j)),
                  pl.BlockSpec((1, 1, tn), lambda l, j: (l, 0, j))],
        out_specs=pl.BlockSpec((1, mc, tn), lambda l, j: (l, 0, j)),
        compiler_params=_cp("parallel", "parallel"),
        name="ada_mod",
    )(c_all, ada_w, ada_b.reshape(L, 1, N))


def _nmm_kernel(x_ref, g_ref, sc_ref, sh_ref, w_ref, o_ref, h_ref):
    @pl.when(pl.program_id(1) == 0)
    def _():
        h_ref[...] = _modnorm(x_ref[...], g_ref[...], sc_ref[0], sh_ref[0]).astype(BF16)

    o_ref[...] = _dot(h_ref[...], w_ref[...])


def _norm_mod_matmul(x, g, sc, sh, w, rpg, tm_pref=512):
    M, D = x.shape
    N = w.shape[1]
    tm = _tile(rpg, tm_pref)
    tpg = rpg // tm
    tn = N // 2 if (N // 2) % LANE == 0 else N
    rows = sc.shape[1]
    return _pallas(
        _nmm_kernel,
        out_shape=jax.ShapeDtypeStruct((M, N), F32),
        grid=(M // tm, N // tn),
        in_specs=[pl.BlockSpec((tm, D), lambda i, j: (i, 0)),
                  pl.BlockSpec((1, D), lambda i, j: (0, 0)),
                  _mod_spec(tm, D, tpg, rows, False),
                  _mod_spec(tm, D, tpg, rows, False),
                  pl.BlockSpec((D, tn), lambda i, j: (0, j))],
        out_specs=pl.BlockSpec((tm, tn), lambda i, j: (i, j)),
        scratch_shapes=[pltpu.VMEM((tm, D), BF16)],
        compiler_params=_cp("parallel", "arbitrary"),
        name="norm_mod_matmul",
    )(x, g, sc, sh, w)


def _mmres_kernel(a_ref, w_ref, gate_ref, res_ref, o_ref):
    o_ref[...] = res_ref[...] + gate_ref[0] * _dot(a_ref[...], w_ref[...])


def _matmul_gate_res(a, w, gate, res, rpg, tm_pref=512, tn_pref=512):
    M, K = a.shape
    N = w.shape[1]
    tm = _tile(rpg, tm_pref)
    tpg = rpg // tm
    tn = _tile(N, tn_pref)
    rows = gate.shape[1]
    return _pallas(
        _mmres_kernel,
        out_shape=jax.ShapeDtypeStruct((M, N), F32),
        grid=(M // tm, N // tn),
        in_specs=[pl.BlockSpec((tm, K), lambda i, j: (i, 0)),
                  pl.BlockSpec((K, tn), lambda i, j: (0, j)),
                  _mod_spec(tm, tn, tpg, rows, True),
                  pl.BlockSpec((tm, tn), lambda i, j: (i, j))],
        out_specs=pl.BlockSpec((tm, tn), lambda i, j: (i, j)),
        compiler_params=_cp("parallel", "arbitrary"),
        name="matmul_gate_res",
    )(a, w, gate, res)


def _mla_proj_kernel(cq_ref, ckv_ref, kr_ref, cos_ref, sin_ref, qlg_ref, wuq_ref, kvg_ref, qng_ref, qrg_ref,
                     kng_ref, krg_ref, wuk_ref, wuv_ref, q_ref, k_ref, v_ref, lat_ref, kro_ref):
    cos = cos_ref[...]
    sin = sin_ref[...]
    lane = lax.broadcasted_iota(jnp.int32, cos.shape, 1)
    half = QK_ROPE // 2

    def rope(x):
        rot = jnp.where(lane < half, -pltpu.roll(x, LANE - half, 1), pltpu.roll(x, half, 1))
        return x * cos + rot * sin

    cqn = _rms(cq_ref[...], qlg_ref[...], cq_ref.shape[-1])
    q = _dot(cqn.astype(BF16), wuq_ref[...])
    lat = _rms(ckv_ref[...], kvg_ref[...], ckv_ref.shape[-1])
    lat_ref[...] = lat
    kr = rope(_rms(kr_ref[...], krg_ref[...], QK_ROPE))
    kro_ref[...] = kr
    latb = lat.astype(BF16)
    kx = _dot(latb, wuk_ref[...])
    v_ref[...] = _dot(latb, wuv_ref[...]).astype(v_ref.dtype)
    krb = kr.astype(k_ref.dtype)
    for h in range(MLA_HEADS):
        o = h * QK_PAD
        qn = _rms(q[:, o:o + QK_NOPE], qng_ref[...], QK_NOPE)
        qr = rope(_rms(q[:, o + QK_NOPE:o + QK_PAD], qrg_ref[...], QK_ROPE))
        q_ref[:, o:o + QK_NOPE] = qn.astype(q_ref.dtype)
        q_ref[:, o + QK_NOPE:o + QK_PAD] = qr.astype(q_ref.dtype)
        kn = _rms(kx[:, h * QK_NOPE:(h + 1) * QK_NOPE], kng_ref[...], QK_NOPE)
        k_ref[:, o:o + QK_NOPE] = kn.astype(k_ref.dtype)
        k_ref[:, o + QK_NOPE:o + QK_PAD] = krb


def _mla_proj(p, cos_t, sin_t, lw, rpg, q_dtype, tm_pref=256):
    M = p.shape[0]
    tm = _tile(rpg, tm_pref)
    tpg = rpg // tm
    H = MLA_HEADS
    row = lambda w: pl.BlockSpec((tm, w), lambda i: (i, 0))
    const = lambda a: pl.BlockSpec(a.shape, lambda i: (0,) * a.ndim)
    tab = pl.BlockSpec((tm, LANE), lambda i: (i % tpg, 0))
    weights = (lw["q_lora_g"], lw["w_uq"], lw["kv_lora_g"], lw["qn_g"], lw["qr_g"], lw["kn_g"], lw["kr_g"],
               lw["w_uk"], lw["w_uv"])
    return _pallas(
        _mla_proj_kernel,
        out_shape=(jax.ShapeDtypeStruct((M, H * QK_PAD), q_dtype),
                   jax.ShapeDtypeStruct((M, H * QK_PAD), BF16),
                   jax.ShapeDtypeStruct((M, H * V_HEAD), BF16),
                   jax.ShapeDtypeStruct((M, LANE), F32),
                   jax.ShapeDtypeStruct((M, LANE), F32)),
        grid=(M // tm,),
        in_specs=[pl.BlockSpec((tm, 512), lambda i: (i, CQ_OFF // 512)),
                  pl.BlockSpec((tm, LANE), lambda i: (i, CKV_OFF // LANE)),
                  pl.BlockSpec((tm, LANE), lambda i: (i, KR_OFF // LANE)),
                  tab, tab] + [const(a) for a in weights],
        out_specs=(row(H * QK_PAD), row(H * QK_PAD), row(H * V_HEAD), row(LANE), row(LANE)),
        compiler_params=_cp("parallel"),
        name="mla_proj",
    )(p, p, p, cos_t, sin_t, *weights)


def _attn_kernel(q_ref, k_ref, v_ref, o_ref, *, tq):
    qi = pl.program_id(2)
    q = q_ref[...]
    row = qi * tq + lax.broadcasted_iota(jnp.int32, (tq, tq), 0)
    col0 = lax.broadcasted_iota(jnp.int32, (tq, tq), 1)

    def body(kj, carry):
        m, l, acc = carry
        start = pl.multiple_of(kj * tq, tq)
        k = k_ref[pl.ds(start, tq), :]
        v = v_ref[pl.ds(start, tq), :]
        s = _dot(q, k, NT) * MLA_SCALE
        s = jnp.where(col0 + kj * tq <= row, s, -jnp.inf)
        m_new = jnp.maximum(m, jnp.max(s, axis=-1, keepdims=True))
        alpha = jnp.exp(m - m_new)
        e = jnp.exp(s - m_new)
        l = alpha * l + jnp.sum(e, axis=-1, keepdims=True)
        acc = alpha * acc + _dot(e.astype(BF16), v)
        return m_new, l, acc

    init = (jnp.full((tq, 1), -jnp.inf, F32), jnp.zeros((tq, 1), F32), jnp.zeros((tq, V_HEAD), F32))
    m, l, acc = lax.fori_loop(0, qi + 1, body, init)
    o_ref[...] = (acc / l).astype(o_ref.dtype)


def _attn_prompt(q, k, v, B, T, tq_pref=512):
    H = MLA_HEADS
    tq = _tile(T, tq_pref)
    nq = T // tq
    return _pallas(
        functools.partial(_attn_kernel, tq=tq),
        out_shape=jax.ShapeDtypeStruct((B * T, H * V_HEAD), BF16),
        grid=(B, H, nq),
        in_specs=[pl.BlockSpec((tq, QK_PAD), lambda b, h, i: (b * nq + i, h)),
                  pl.BlockSpec((T, QK_PAD), lambda b, h, i: (b, h)),
                  pl.BlockSpec((T, V_HEAD), lambda b, h, i: (b, h))],
        out_specs=pl.BlockSpec((tq, V_HEAD), lambda b, h, i: (b * nq + i, h)),
        compiler_params=_cp("parallel", "parallel", "arbitrary"),
        name="attn_prompt",
    )(q, k, v)


def _absorb_kernel(q_ref, kng_ref, wuk_ref, qt_ref, qr_ref):
    g = kng_ref[...]
    for h in range(MLA_HEADS):
        o = h * QK_PAD
        qn = q_ref[:, o:o + QK_NOPE] * g
        wh = wuk_ref[:, h * QK_NOPE:(h + 1) * QK_NOPE]
        qt_ref[h] = _dot3(qn, wh, NT)
        qr_ref[h] = q_ref[:, o + QK_NOPE:o + QK_PAD]


def _absorb_q(q, kn_g, w_uk):
    M = q.shape[0]
    H = MLA_HEADS
    return _pallas(
        _absorb_kernel,
        out_shape=(jax.ShapeDtypeStruct((H, M, LANE), F32), jax.ShapeDtypeStruct((H, M, LANE), F32)),
        compiler_params=pltpu.CompilerParams(vmem_limit_bytes=VMEM_LIMIT),
        name="absorb_q",
    )(q, kn_g, w_uk)


def _paged_kernel(pt_ref, qt_ref, qr_ref, latn_ref, krn_ref, wukt_ref, lat_hbm, kr_hbm, o_ref,
                  latbuf, krbuf, lhs_ref, s_ref, sem, *, layer, n_pages, chunk_pages):
    b = pl.program_id(0)
    nb = pl.num_programs(0)
    slot = b % 2
    H = MLA_HEADS
    HD = H * QK_NOPE
    n_chunks = n_pages // chunk_pages
    rows = chunk_pages * PAGE
    R = latbuf.shape[-1]

    def page_copies(seq, sl, c, j):
        p = c * chunk_pages + j
        pg = pt_ref[seq * n_pages + p]
        return (pltpu.make_async_copy(lat_hbm.at[layer, pg], latbuf.at[sl, p], sem.at[sl, 0, c]),
                pltpu.make_async_copy(kr_hbm.at[layer, pg],
                                      krbuf.at[sl, :, pl.ds(pl.multiple_of(p * PAGE, PAGE), PAGE)], sem.at[sl, 1, c]))

    def fetch_chunk(seq, sl, c):
        for j in range(chunk_pages):
            for cp in page_copies(seq, sl, c, j):
                cp.start()

    def wait_chunk(seq, sl, c):
        for j in range(chunk_pages):
            for cp in page_copies(seq, sl, c, j):
                cp.wait()

    @pl.when(b == 0)
    def _():
        def body(c, carry):
            fetch_chunk(0, 0, c)
            return carry
        lax.fori_loop(0, n_chunks, body, 0)

    lhs_ref[pl.ds(0, HD), :] = wukt_ref[...]
    lhs_ref[pl.ds(HD, 16), :] = jnp.concatenate([qt_ref[0], jnp.zeros_like(qt_ref[0])], axis=0).astype(BF16)
    lhs = lhs_ref[...]
    qr = qr_ref[0][:, :QK_ROPE].astype(BF16)

    def latent_scores(latb):
        xt = _dot(lhs, latb, NT)
        ms = []
        for h in range(H):
            acc = None
            for i in range(QK_NOPE // 8):
                blk = xt[h * QK_NOPE + 8 * i:h * QK_NOPE + 8 * i + 8]
                acc = blk * blk if acc is None else acc + blk * blk
            ms.append(jnp.sum(acc, axis=0, keepdims=True))
        ms = jnp.concatenate(ms, axis=0)
        return xt[HD:HD + H] * lax.rsqrt(ms / QK_NOPE + NORM_EPS)

    nxt = jnp.minimum(b + 1, nb - 1)

    def wait_body(c, carry):
        wait_chunk(b, slot, c)
        return carry
    lax.fori_loop(0, n_chunks, wait_body, 0)

    def score_chunk(c, carry):
        fetch_chunk(nxt, 1 - slot, c)

        p0 = pl.multiple_of(c * chunk_pages, chunk_pages)
        k0 = pl.multiple_of(c * rows, rows)
        s_rope = _dot(qr, krbuf[slot, :, pl.ds(k0, rows)].astype(BF16))
        latb = latbuf[slot, pl.ds(p0, chunk_pages)].reshape(rows, R).astype(BF16)
        s_ref[:, pl.ds(k0, rows)] = (latent_scores(latb) + s_rope) * MLA_SCALE
        return carry

    unroll = 4 if n_chunks % 4 == 0 else 1
    lax.fori_loop(0, n_chunks, score_chunk, 0, unroll=unroll)

    latn = latn_ref[0]
    latnb = jnp.concatenate([latn, jnp.zeros((PAGE - latn.shape[0], R), F32)], axis=0).astype(BF16)
    krn = krn_ref[0][0:1, :QK_ROPE].astype(BF16).astype(F32)
    s_rope = jnp.sum(qr.astype(F32) * krn, axis=-1, keepdims=True)
    s_new = (latent_scores(latnb) + s_rope) * MLA_SCALE
    s_new = jnp.where(lax.broadcasted_iota(jnp.int32, s_new.shape, 1) == 0, s_new, -jnp.inf)

    s_all = s_ref[...]
    m = jnp.maximum(jnp.max(s_all, axis=-1, keepdims=True), jnp.max(s_new, axis=-1, keepdims=True))
    e_new = jnp.exp(s_new - m)
    s_ref[...] = jnp.exp(s_all - m)
    l = jnp.sum(s_ref[...], axis=-1, keepdims=True) + jnp.sum(e_new, axis=-1, keepdims=True)

    def value_chunk(c, acc):
        p0 = pl.multiple_of(c * chunk_pages, chunk_pages)
        k0 = pl.multiple_of(c * rows, rows)
        latb = latbuf[slot, pl.ds(p0, chunk_pages)].reshape(rows, R).astype(BF16)
        return acc + _dot(s_ref[:, pl.ds(k0, rows)].astype(BF16), latb)

    acc = lax.fori_loop(0, n_chunks, value_chunk, _dot(e_new.astype(BF16), latnb), unroll=unroll)
    o_ref[0] = acc / l

    @pl.when(b == nb - 1)
    def _():
        def body(c, carry):
            wait_chunk(nxt, 1 - slot, c)
            return carry
        lax.fori_loop(0, n_chunks, body, 0)


def _paged_attention(page_table, qt, qr, lat_new, kr_new, w_uk_t, cache_latent, cache_krope_t, layer):
    DB, n_pages = page_table.shape
    H = MLA_HEADS
    R = cache_latent.shape[-1]
    chunk_pages = 8 if n_pages % 8 == 0 else 1
    pad8 = lambda a: jnp.pad(a[:, None, :], ((0, 0), (0, 7), (0, 0)))
    seq = lambda w: pl.BlockSpec((1, 8, w), lambda b, pt: (b, 0, 0))
    grid_spec = pltpu.PrefetchScalarGridSpec(
        num_scalar_prefetch=1,
        grid=(DB,),
        in_specs=[seq(LANE), seq(LANE), seq(LANE), seq(LANE),
                  pl.BlockSpec(w_uk_t.shape, lambda b, pt: (0, 0)),
                  pl.BlockSpec(memory_space=pl.ANY),
                  pl.BlockSpec(memory_space=pl.ANY)],
        out_specs=pl.BlockSpec((1, H, R), lambda b, pt: (b, 0, 0)),
        scratch_shapes=[pltpu.VMEM((2, n_pages, PAGE, R), F32),
                        pltpu.VMEM((2, QK_ROPE, n_pages * PAGE), F32),
                        pltpu.VMEM((w_uk_t.shape[0] + 16, R), BF16),
                        pltpu.VMEM((H, n_pages * PAGE), F32),
                        pltpu.SemaphoreType.DMA((2, 2, n_pages // chunk_pages))],
    )
    return _pallas(
        functools.partial(_paged_kernel, layer=layer, n_pages=n_pages, chunk_pages=chunk_pages),
        out_shape=jax.ShapeDtypeStruct((DB, H, R), F32),
        grid_spec=grid_spec,
        compiler_params=_cp("arbitrary"),
        name="paged_attention",
    )(page_table.reshape(-1), qt, qr, pad8(lat_new), pad8(kr_new), w_uk_t, cache_latent, cache_krope_t)


def _uv_kernel(o_ref, wuv_ref, y_ref):
    for h in range(MLA_HEADS):
        y_ref[:, h * V_HEAD:(h + 1) * V_HEAD] = _dot(
            o_ref[h].astype(BF16), wuv_ref[:, h * V_HEAD:(h + 1) * V_HEAD]).astype(y_ref.dtype)


def _uv_out(o_lat, w_uv):
    H, M, _ = o_lat.shape
    return _pallas(
        _uv_kernel,
        out_shape=jax.ShapeDtypeStruct((M, H * V_HEAD), BF16),
        compiler_params=pltpu.CompilerParams(vmem_limit_bytes=VMEM_LIMIT),
        name="uv_out",
    )(o_lat, w_uv)


def _rw_prep_body(pb, prev, mu_ref, w0_ref, wup_ref, a0_ref, aup_ref, gup_ref, kk_ref, ka_ref, seg_ref,
                  r_o, wl_o, k_o, v_o, kk_o, b_o, g_o):
    W = RW_W
    xs = pb + (prev - pb) * mu_ref[...]
    r = xs[:, 0:W]
    k = xs[:, W:2 * W]
    v = xs[:, 2 * W:3 * W]
    wd = xs[:, 3 * W:3 * W + LANE]
    ad = xs[:, 3 * W + LANE:3 * W + 2 * LANE]
    gd = xs[:, 3 * W + 2 * LANE:]
    z = -(w0_ref[...] + _dot(jnp.tanh(wd).astype(BF16), wup_ref[...]))
    softplus = jnp.maximum(z, 0.0) + jnp.log1p(jnp.exp(-jnp.abs(z)))
    w = -softplus - 0.5
    a = _sigmoid(a0_ref[...] + _dot(ad.astype(BF16), aup_ref[...]))
    g = _dot(_sigmoid(gd).astype(BF16), gup_ref[...])
    kk = k * kk_ref[...]
    k2 = k * (1.0 + (a - 1.0) * ka_ref[...])
    n2 = _dot_sel_r(kk * kk, seg_ref[...])
    kkn = kk / jnp.maximum(jnp.sqrt(n2), 1e-12)
    r_o[...] = r
    wl_o[...] = -jnp.exp(w)
    k_o[...] = k2
    v_o[...] = v
    kk_o[...] = kkn
    b_o[...] = kkn * a
    g_o[...] = g


def _rw_prep_seq_kernel(pb_ref, halo_ref, *rest, tpg):
    pb = pb_ref[...]
    i = pl.program_id(0)
    rolled = pltpu.roll(pb, 1, 0)
    row = lax.broadcasted_iota(jnp.int32, pb.shape, 0)
    first_row = jnp.where(i % tpg == 0, 0, -1)
    prev = jnp.where(row == 0, halo_ref[7:8, :], rolled)
    prev = jnp.where(row == first_row, 0.0, prev)
    _rw_prep_body(pb, prev, *rest)


def _rw_prep_step_kernel(pb_ref, prev_ref, *rest):
    _rw_prep_body(pb_ref[...], prev_ref[...], *rest)


def _seg_matrix(n, w):
    return jnp.asarray(np.kron(np.eye(n // w, dtype=np.float32), np.ones((w, w), np.float32)), BF16)


def _rw_prep(p, prev, lw, rpg, tm_pref=256):
    M = p.shape[0]
    tm = _tile(rpg, tm_pref)
    tpg = rpg // tm
    seg = _seg_matrix(RW_W, RW_N)
    weights = (lw["rw_mu"], lw["rw_w0"], lw["rw_w_up"], lw["rw_a0"], lw["rw_a_up"], lw["rw_g_up"], lw["rw_k_k"],
               lw["rw_k_a"], seg)
    const = lambda a: pl.BlockSpec(a.shape, lambda i: (0,) * a.ndim)
    out = jax.ShapeDtypeStruct((M, RW_W), F32)
    ospec = pl.BlockSpec((tm, RW_W), lambda i: (i, 0))
    main = pl.BlockSpec((tm, PB_W), lambda i: (i, 0))
    if prev is None:
        kern = functools.partial(_rw_prep_seq_kernel, tpg=tpg)
        second = pl.BlockSpec((8, PB_W), lambda i: (jnp.maximum(i * (tm // 8) - 1, 0), 0))
        second_arg = p
    else:
        kern = _rw_prep_step_kernel
        second = pl.BlockSpec((tm, PB_W), lambda i: (i, 0))
        second_arg = prev
    return _pallas(
        kern,
        out_shape=(out,) * 7,
        grid=(M // tm,),
        in_specs=[main, second] + [const(a) for a in weights],
        out_specs=(ospec,) * 7,
        compiler_params=_cp("parallel"),
        name="rwkv_prep",
    )(p, second_arg, *weights)


def _rw_chunk_kernel(r_ref, wl_ref, k_ref, v_ref, kk_ref, b_ref, y_ref, s_ref, st_ref, *, C):
    c = pl.program_id(1)

    @pl.when(c == 0)
    def _():
        st_ref[...] = jnp.zeros_like(st_ref)

    N = RW_N
    ri = lax.broadcasted_iota(jnp.int32, (C, C), 0)
    ci = lax.broadcasted_iota(jnp.int32, (C, C), 1)
    incl = ci <= ri
    strict = ci < ri
    eye_c = (ci == ri).astype(F32)
    rn = lax.broadcasted_iota(jnp.int32, (N, N), 0)
    cn = lax.broadcasted_iota(jnp.int32, (N, N), 1)
    eye_n = rn == cn

    wl = wl_ref[...]
    cs = _dot_sel_l(incl.astype(BF16), wl)
    p_in = jnp.exp(cs)
    p_ex = jnp.exp(cs - wl)
    p_inv = jnp.exp(-cs)
    p_end = p_in[C - 1:C, :]
    at = -kk_ref[...] * p_ex
    rp = r_ref[...] * p_in
    bp = b_ref[...] * p_inv
    kp = k_ref[...] * p_inv
    bpp = bp * p_end
    kpp = kp * p_end
    v = v_ref[...]

    heads = range(RW_HEADS)
    cut = lambda x: [x[:, h * N:(h + 1) * N] for h in heads]
    at_h, rp_h, bp_h, kp_h, v_h, bpp_h, kpp_h, pe_h = (cut(x) for x in (at, rp, bp, kp, v, bpp, kpp, p_end))
    lhs = [jnp.concatenate([at_h[h], rp_h[h]], axis=0) for h in heads]
    gb = [_dot3(lhs[h], bp_h[h], NT) for h in heads]
    gk = [_dot3(lhs[h], kp_h[h], NT) for h in heads]
    a_ab = [jnp.where(strict, gb[h][:C], 0.0) for h in heads]
    a_rb = [jnp.where(incl, gb[h][C:], 0.0) for h in heads]
    a_ak = [jnp.where(strict, gk[h][:C], 0.0) for h in heads]
    a_rk = [jnp.where(incl, gk[h][C:], 0.0) for h in heads]
    akv = [_dot3(a_ak[h], v_h[h]) for h in heads]
    rkv = [_dot3(a_rk[h], v_h[h]) for h in heads]
    kv = [_dot3(kpp_h[h], v_h[h], TN) for h in heads]
    t = None
    m = 1
    while m < C:
        sh = m.bit_length() - 1
        low_left = ((ri >> (sh + 1)) == (ci >> (sh + 1))) & (((ri >> sh) & 1) == 1) & (((ci >> sh) & 1) == 0)
        a_off = [jnp.where(low_left, a_ab[h], 0.0) for h in heads]
        if m == 1:
            t = [eye_c + a_off[h] for h in heads]
        else:
            inner = [_dot3(a_off[h], t[h]) for h in heads]
            t = [t[h] + _dot3(t[h], inner[h]) for h in heads]
        m *= 2
    w = [_dot3(t[h], jnp.concatenate([at_h[h], akv[h]], axis=1)) for h in heads]
    g = [_dot3(a_rb[h], w[h]) for h in heads]
    mm = [_dot3(bpp_h[h], w[h], TN) for h in heads]
    top = [jnp.concatenate([rp_h[h] + g[h][:, :N], jnp.where(eye_n, pe_h[h], 0.0) + mm[h][:, :N]], axis=0)
           for h in heads]
    res = [_dot3(top[h], st_ref[h]) for h in heads]
    for h in heads:
        st_ref[h] = res[h][C:] + (mm[h][:, N:] + kv[h])
    y_ref[...] = jnp.concatenate([res[h][:C] + (g[h][:, N:] + rkv[h]) for h in heads], axis=1)

    @pl.when(c == pl.num_programs(1) - 1)
    def _():
        s_ref[0] = st_ref[...]


def _rw_chunked(ops, B, T):
    C = _tile(T, 64)
    nc = T // C
    blk = pl.BlockSpec((C, RW_W), lambda b, c: (b * nc + c, 0))
    return _pallas(
        functools.partial(_rw_chunk_kernel, C=C),
        out_shape=(jax.ShapeDtypeStruct((B * T, RW_W), F32),
                   jax.ShapeDtypeStruct((B, RW_HEADS, RW_N, RW_N), F32)),
        grid=(B, nc),
        in_specs=[blk] * 6,
        out_specs=(blk, pl.BlockSpec((1, RW_HEADS, RW_N, RW_N), lambda b, c: (b, 0, 0, 0))),
        scratch_shapes=[pltpu.VMEM((RW_HEADS, RW_N, RW_N), F32)],
        compiler_params=_cp("parallel", "arbitrary"),
        name="rwkv_chunked",
    )(*ops)


def _rw_step_kernel(s_ref, r_ref, wl_ref, k_ref, v_ref, kk_ref, b_ref, ej_ref, ei_ref, rd_ref, so_ref, y_ref):
    s = s_ref[0]
    ej = ej_ref[...]
    ei = ei_ref[...]
    rd = rd_ref[...]
    over_j = lambda x: _dot_sel_r(x, ej)
    sa = _dot_sel_r(s * over_j(kk_ref[0]), rd)
    s_new = (s * over_j(jnp.exp(wl_ref[0])) - _dot_sel_r(sa, ei) * over_j(b_ref[0])
             + _dot_sel_r(v_ref[0], ei) * over_j(k_ref[0]))
    so_ref[0] = s_new
    y_ref[0] = _dot_sel_r(s_new * over_j(r_ref[0]), rd)


def _rw_step(state_wkv, layer, vecs):
    _, DB, _ = state_wkv.shape
    H, N = RW_HEADS, RW_N
    bt = _tile(DB, 128)
    ej = jnp.asarray(np.kron(np.ones((1, N), np.float32), np.eye(N, dtype=np.float32)), BF16)
    ei = jnp.asarray(np.kron(np.eye(N, dtype=np.float32), np.ones((1, N), np.float32)), BF16)
    rd = ei.T
    sblk = pl.BlockSpec((1, bt, N * N), lambda i, h: (layer, i, h))
    oblk = pl.BlockSpec((1, bt, N * N), lambda i, h: (0, i, h))
    vblk = pl.BlockSpec((1, bt, N), lambda i, h: (h, i, 0))
    const = lambda a: pl.BlockSpec(a.shape, lambda i, h: (0, 0))
    s_new, y = _pallas(
        _rw_step_kernel,
        out_shape=(jax.ShapeDtypeStruct((1, DB, H * N * N), F32), jax.ShapeDtypeStruct((H, DB, N), F32)),
        grid=(DB // bt, H),
        in_specs=[sblk] + [vblk] * 6 + [const(ej), const(ei), const(rd)],
        out_specs=(oblk, vblk),
        compiler_params=_cp("parallel", "parallel"),
        name="rwkv_step",
    )(state_wkv, *vecs, ej, ei, rd)
    return s_new[0], y


def _rw_post_kernel(y_ref, r_ref, k_ref, v_ref, g_ref, lng_ref, lnb_ref, rk_ref, seg_ref, o_ref):
    seg = seg_ref[...]
    y = y_ref[...]
    mu = _dot_sel_r(y, seg) / RW_N
    d = y - mu
    var = _dot_sel_r(d * d, seg) / RW_N
    yn = (d * lax.rsqrt(var + GN_EPS)) * lng_ref[...] + lnb_ref[...]
    bonus = _dot_sel_r(r_ref[...] * k_ref[...] * rk_ref[...], seg) * v_ref[...]
    o_ref[...] = ((yn + bonus) * g_ref[...]).astype(o_ref.dtype)


def _rw_post(y, r, k, v, g, lw, tm_pref=256):
    M = y.shape[0]
    tm = _tile(M, tm_pref)
    seg = _seg_matrix(RW_W, RW_N)
    row = pl.BlockSpec((tm, RW_W), lambda i: (i, 0))
    const = lambda a: pl.BlockSpec(a.shape, lambda i: (0,) * a.ndim)
    weights = (lw["rw_lnx_g"], lw["rw_lnx_b"], lw["rw_r_k"], seg)
    return _pallas(
        _rw_post_kernel,
        out_shape=jax.ShapeDtypeStruct((M, RW_W), BF16),
        grid=(M // tm,),
        in_specs=[row] * 5 + [const(a) for a in weights],
        out_specs=row,
        compiler_params=_cp("parallel"),
        name="rwkv_post",
    )(y, r, k, v, g, *weights)


def _pool_windows(x_ext, halo_rows, count_of, wp_ref, ps_ref, o_ref):
    G = len(POOL_WINDOWS)
    gw = x_ext.shape[1] // G
    for gi, win in enumerate(POOL_WINDOWS):
        xg = x_ext[:, gi * gw:(gi + 1) * gw]
        s = xg
        step = 1
        while step < win:
            s = s + pltpu.roll(s, step, 0)
            step *= 2
        diff = s[halo_rows:] / count_of(win) - xg[halo_rows:]
        o = _dot(diff.astype(BF16), wp_ref[gi]) * ps_ref[:, gi * gw:(gi + 1) * gw]
        o_ref[:, gi * gw:(gi + 1) * gw] = o.astype(o_ref.dtype)


def _pool_seq_kernel(pc_ref, halo_ref, wp_ref, ps_ref, o_ref, *, tm, tpg):
    i = pl.program_id(0)
    t0 = (i % tpg) * tm
    hrow = lax.broadcasted_iota(jnp.int32, (halo_ref.shape[0], 1), 0)
    halo = jnp.where(hrow < jnp.where(t0 == 0, halo_ref.shape[0], 0), 0.0, halo_ref[...])
    x_ext = jnp.concatenate([halo, pc_ref[...]], axis=0)
    pos = t0 + lax.broadcasted_iota(jnp.int32, (tm, 1), 0)
    count_of = lambda win: jnp.minimum(pos + 1, win).astype(F32)
    _pool_windows(x_ext, halo_ref.shape[0], count_of, wp_ref, ps_ref, o_ref)


def _pool_prompt(p, w_pool, pool_scale, rpg, tm_pref=256):
    M = p.shape[0]
    W = 512
    HALO = 16
    tm = _tile(rpg, tm_pref)
    tpg = rpg // tm
    return _pallas(
        functools.partial(_pool_seq_kernel, tm=tm, tpg=tpg),
        out_shape=jax.ShapeDtypeStruct((M, W), BF16),
        grid=(M // tm,),
        in_specs=[pl.BlockSpec((tm, W), lambda i: (i, PC_OFF // W)),
                  pl.BlockSpec((HALO, W), lambda i: (jnp.maximum(i * (tm // HALO) - 1, 0), PC_OFF // W)),
                  pl.BlockSpec(w_pool.shape, lambda i: (0, 0, 0)),
                  pl.BlockSpec(pool_scale.shape, lambda i: (0, 0))],
        out_specs=pl.BlockSpec((tm, W), lambda i: (i, 0)),
        compiler_params=_cp("parallel"),
        name="pool_prompt",
    )(p, p, w_pool, pool_scale)


def _pool_step_kernel(st_ref, pc_ref, wp_ref, ps_ref, o_ref, *, past):
    G = len(POOL_WINDOWS)
    pc = pc_ref[...]
    gw = pc.shape[1] // G
    nbuf = st_ref.shape[0]
    for gi, win in enumerate(POOL_WINDOWS):
        sl = slice(gi * gw, (gi + 1) * gw)
        xg = pc[:, sl]
        total = xg
        for j in range(1, win):
            total = total + st_ref[nbuf - j][:, sl]
        diff = total / float(min(past + 1, win)) - xg
        o = _dot(diff.astype(BF16), wp_ref[gi]) * ps_ref[:, sl]
        o_ref[:, sl] = o.astype(o_ref.dtype)


def _pool_step(state_t, p, w_pool, pool_scale, past):
    nbuf, DB, W = state_t.shape
    return _pallas(
        functools.partial(_pool_step_kernel, past=past),
        out_shape=jax.ShapeDtypeStruct((DB, W), BF16),
        grid=(1,),
        in_specs=[pl.BlockSpec((nbuf, DB, W), lambda i: (0, 0, 0)),
                  pl.BlockSpec((DB, W), lambda i: (0, PC_OFF // W)),
                  pl.BlockSpec(w_pool.shape, lambda i: (0, 0, 0)),
                  pl.BlockSpec(pool_scale.shape, lambda i: (0, 0))],
        out_specs=pl.BlockSpec((DB, W), lambda i: (0, 0)),
        compiler_params=_cp("arbitrary"),
        name="pool_step",
    )(state_t, p, w_pool, pool_scale)


FFN_HALO = 16
FFN_SUB = 256


def _ffn_up_seq_kernel(x_ref, xh_ref, g_ref, sc_ref, sh_ref, wv_ref, wg_ref, cwv_ref, cwg_ref, cbv_ref, cbg_ref,
                       act_ref, tv_ref, tg_ref, h_ref, *, tm, tpg):
    i = pl.program_id(0)

    @pl.when(pl.program_id(1) == 0)
    def _():
        g, sc, sh = g_ref[...], sc_ref[0], sh_ref[0]
        h_ref[pl.ds(0, FFN_HALO), :] = _modnorm(xh_ref[...], g, sc, sh).astype(BF16)
        h_ref[pl.ds(FFN_HALO, tm), :] = _modnorm(x_ref[...], g, sc, sh).astype(BF16)

    h = h_ref[...]
    row = lax.broadcasted_iota(jnp.int32, (tm + FFN_HALO, 1), 0)
    keep = row >= jnp.where(i % tpg == 0, FFN_HALO, 0)

    def branch(w_ref, cw_ref, cb_ref, t_ref, cs):
        u = jnp.where(keep, _dot(h, w_ref[:, cs]), 0.0)
        t_ref[0, :, cs] = u[tm + FFN_HALO - 8:, :]
        cw = cw_ref[:, cs]
        return (cb_ref[:, cs] + pltpu.roll(u, 2, 0)[FFN_HALO:] * cw[0:1] + pltpu.roll(u, 1, 0)[FFN_HALO:] * cw[1:2]
                + u[FFN_HALO:] * cw[2:3])

    tn = act_ref.shape[1]
    for c0 in range(0, tn, FFN_SUB):
        cs = slice(c0, min(c0 + FFN_SUB, tn))
        val = branch(wv_ref, cwv_ref, cbv_ref, tv_ref, cs)
        gate = branch(wg_ref, cwg_ref, cbg_ref, tg_ref, cs)
        act_ref[:, cs] = ((gate * _sigmoid(gate)) * val).astype(act_ref.dtype)


def _ffn_up_prompt(x, g, sc, sh, w_up, conv_w, conv_b, B, T, tm_pref=512, tn_pref=1408):
    M, D = x.shape
    F = w_up.shape[1] // 2
    tm = _tile(T, tm_pref)
    tpg = T // tm
    tn = max(t for t in range(LANE, tn_pref + 1, LANE) if F % t == 0)
    nj = F // tn
    lo = lambda i, j: (0, j)
    hi = lambda i, j: (0, nj + j)
    tail = jax.ShapeDtypeStruct((M // tm, 8, F), F32)
    tspec = pl.BlockSpec((1, 8, tn), lambda i, j: (i, 0, j))
    return _pallas(
        functools.partial(_ffn_up_seq_kernel, tm=tm, tpg=tpg),
        out_shape=(jax.ShapeDtypeStruct((M, F), BF16), tail, tail),
        grid=(M // tm, nj),
        in_specs=[pl.BlockSpec((tm, D), lambda i, j: (i, 0)),
                  pl.BlockSpec((FFN_HALO, D), lambda i, j: (jnp.maximum(i * (tm // FFN_HALO) - 1, 0), 0)),
                  pl.BlockSpec((1, D), lambda i, j: (0, 0)),
                  _mod_spec(tm, D, tpg, 1, False),
                  _mod_spec(tm, D, tpg, 1, False),
                  pl.BlockSpec((D, tn), lo), pl.BlockSpec((D, tn), hi),
                  pl.BlockSpec((3, tn), lo), pl.BlockSpec((3, tn), hi),
                  pl.BlockSpec((1, tn), lo), pl.BlockSpec((1, tn), hi)],
        out_specs=(pl.BlockSpec((tm, tn), lambda i, j: (i, j)), tspec, tspec),
        scratch_shapes=[pltpu.VMEM((tm + FFN_HALO, D), BF16)],
        compiler_params=_cp("arbitrary", "arbitrary"),
        name="ffn_up_prompt",
    )(x, x, g, sc, sh, w_up, w_up, conv_w, conv_w, conv_b, conv_b)


def _ffn_up_step_kernel(x_ref, g_ref, sc_ref, sh_ref, wv_ref, wg_ref, cwv_ref, cwg_ref, cbv_ref, cbg_ref,
                        s0v_ref, s0g_ref, s1v_ref, s1g_ref, act_ref, uv_ref, ug_ref, h_ref):
    @pl.when(pl.program_id(0) == 0)
    def _():
        h_ref[...] = _modnorm(x_ref[...], g_ref[...], sc_ref[0], sh_ref[0]).astype(BF16)

    h = h_ref[...]

    def branch(w_ref, cw_ref, cb_ref, s0_ref, s1_ref, u_ref):
        u = _dot(h, w_ref[...])
        u_ref[...] = u
        cw = cw_ref[...]
        return cb_ref[...] + s0_ref[...] * cw[0:1] + s1_ref[...] * cw[1:2] + u * cw[2:3]

    val = branch(wv_ref, cwv_ref, cbv_ref, s0v_ref, s1v_ref, uv_ref)
    gate = branch(wg_ref, cwg_ref, cbg_ref, s0g_ref, s1g_ref, ug_ref)
    act_ref[...] = ((gate * _sigmoid(gate)) * val).astype(act_ref.dtype)


def _ffn_up_step(x, g, sc, sh, w_up, conv_w, conv_b, st0, st1, tn_pref=512):
    M, D = x.shape
    F = w_up.shape[1] // 2
    tn = _tile(F, tn_pref)
    nj = F // tn
    lo = lambda j: (0, j)
    hi = lambda j: (0, nj + j)
    full = lambda a: pl.BlockSpec(a.shape, lambda j: (0,) * a.ndim)
    u = jax.ShapeDtypeStruct((M, F), F32)
    ospec = pl.BlockSpec((M, tn), lo)
    return _pallas(
        _ffn_up_step_kernel,
        out_shape=(jax.ShapeDtypeStruct((M, F), BF16), u, u),
        grid=(nj,),
        in_specs=[full(x), full(g), full(sc), full(sh),
                  pl.BlockSpec((D, tn), lo), pl.BlockSpec((D, tn), hi),
                  pl.BlockSpec((3, tn), lo), pl.BlockSpec((3, tn), hi),
                  pl.BlockSpec((1, tn), lo), pl.BlockSpec((1, tn), hi),
                  pl.BlockSpec((M, tn), lo), pl.BlockSpec((M, tn), hi),
                  pl.BlockSpec((M, tn), lo), pl.BlockSpec((M, tn), hi)],
        out_specs=(ospec, ospec, ospec),
        scratch_shapes=[pltpu.VMEM((M, D), BF16)],
        compiler_params=_cp("arbitrary"),
        name="ffn_up_step",
    )(x, g, sc, sh, w_up, w_up, conv_w, conv_w, conv_b, conv_b, st0, st0, st1, st1)


def _in_proj_columns():
    a_cols = 512 + 128 + QK_ROPE
    b0 = a_cols
    sizes = (RW_W, DECAY_LORA, RW_W, RW_W, ICLR_LORA, GATE_LORA)
    offs = np.concatenate([[0], np.cumsum(sizes)])[:-1] + b0
    r0, wd0, k0, v0, ad0, gd0 = [int(o) for o in offs]
    b_cols = int(sum(sizes))
    c0 = b0 + b_cols
    pad = lambda n: [-1] * n
    rng = lambda s, n: list(range(s, s + n))
    cols = (rng(r0, RW_W) + rng(k0, RW_W) + rng(v0, RW_W)
            + rng(wd0, DECAY_LORA) + pad(LANE - DECAY_LORA)
            + rng(ad0, ICLR_LORA) + pad(LANE - ICLR_LORA)
            + rng(gd0, GATE_LORA)
            + rng(0, 512)
            + rng(c0, 512)
            + rng(512, 128)
            + rng(640, QK_ROPE) + pad(LANE - QK_ROPE))
    cols = np.asarray(cols, np.int32)
    assert cols.shape[0] == P_COLS
    pb_src = cols[:PB_W] - b0
    return cols, pb_src, b_cols


def _gather_cols(w, cols):
    cols = [int(c) for c in cols]
    runs, i = [], 0
    while i < len(cols):
        j = i + 1
        if cols[i] < 0:
            while j < len(cols) and cols[j] < 0:
                j += 1
            runs.append(jnp.zeros(w.shape[:-1] + (j - i,), w.dtype))
        else:
            while j < len(cols) and cols[j] == cols[j - 1] + 1:
                j += 1
            runs.append(w[..., cols[i]:cols[i] + (j - i)])
        i = j
    return jnp.concatenate(runs, axis=-1)


def _uq_columns():
    cols = []
    for h in range(MLA_HEADS):
        base = h * (QK_NOPE + QK_ROPE)
        cols += list(range(base, base + QK_NOPE + QK_ROPE)) + [-1] * (QK_PAD - QK_NOPE - QK_ROPE)
    return np.asarray(cols, np.int32)


def _pad_lanes(a, n):
    return jnp.pad(a, [(0, 0)] * (a.ndim - 1) + [(0, n - a.shape[-1])])


def _pad_rows(a, n):
    return jnp.pad(a, [(0, 0)] * (a.ndim - 2) + [(0, n - a.shape[-2]), (0, 0)])


def _rope_tables(pos):
    half = QK_ROPE // 2
    inv = jnp.power(ROPE_THETA, -jnp.arange(half, dtype=F32) * 2.0 / QK_ROPE)
    ang = pos.astype(F32)[:, None] * inv[None, :]
    cos, sin = jnp.cos(ang), jnp.sin(ang)
    z = jnp.zeros((pos.shape[0], LANE - QK_ROPE), F32)
    return jnp.concatenate([cos, cos, z], axis=1), jnp.concatenate([sin, sin, z], axis=1)


def kernel(x_prompt, x_sample, cache_latent, cache_krope, state_wkv, state_shift, state_pool, state_conv, page_table, c_prompt, c_sample, ada_w, ada_b, norm_mix_g, norm_ffn_g, w_in, q_lora_g, w_uq, kv_lora_g, w_uk, w_uv, qn_g, qr_g, kn_g, kr_g, rw_mu, rw_w0, rw_w_up, rw_a0, rw_a_up, rw_g_up, rw_k_k, rw_k_a, rw_r_k, rw_lnx_g, rw_lnx_b, w_pool, pool_scale, w_out, ffn_w_up, ffn_conv_w, ffn_conv_b, ffn_w_down):
    B, T, D = x_prompt.shape
    DB, TS, _ = x_sample.shape
    assert TS == 1, "the sample group decodes one token per sequence"
    L = ada_w.shape[0]
    n_pages = page_table.shape[1]
    past = n_pages * PAGE
    H, N = RW_HEADS, RW_N
    F = ffn_w_down.shape[1]

    cols, pb_src, b_cols = _in_proj_columns()
    pb_valid = np.nonzero(pb_src >= 0)[0]
    pb_inverse = pb_valid[np.argsort(pb_src[pb_valid])]

    w_in_p = _gather_cols(w_in, cols).astype(BF16)
    w_uq_p = _gather_cols(w_uq, _uq_columns()).astype(BF16)
    w_uk_f = w_uk.reshape(L, w_uk.shape[1], MLA_HEADS * QK_NOPE)
    w_uk_b = w_uk_f.astype(BF16)
    w_uk_t = jnp.swapaxes(w_uk_b, 1, 2)
    cache_krope_t = jnp.swapaxes(cache_krope, 2, 3)
    w_uv_b = w_uv.reshape(L, w_uv.shape[1], MLA_HEADS * V_HEAD).astype(BF16)
    w_out_b = w_out.astype(BF16)
    w_up_b = ffn_w_up.astype(BF16)
    w_down_b = ffn_w_down.astype(BF16)
    w_pool_b = w_pool.astype(BF16)
    mu_p = _gather_cols(rw_mu, pb_src)
    rw_w_up_p = _pad_rows(rw_w_up, LANE).astype(BF16)
    rw_a_up_p = _pad_rows(rw_a_up, LANE).astype(BF16)
    rw_g_up_b = rw_g_up.astype(BF16)
    qr_g_p = _pad_lanes(qr_g, LANE)
    kr_g_p = _pad_lanes(kr_g, LANE)
    shift_p = _gather_cols(state_shift, pb_src)
    row2 = lambda a, l: a[l].reshape(1, -1)

    mc = DB + B
    c_all = _pad_rows(jnp.concatenate([c_sample, c_prompt], axis=0), mc + (-mc) % 8)
    mod = _ada_mod(c_all, ada_w, ada_b)

    cos_p, sin_p = _rope_tables(jnp.arange(T, dtype=jnp.int32))
    cos_s, sin_s = _rope_tables(jnp.full((DB,), past, jnp.int32))

    state_wkv_flat = state_wkv.reshape(L, DB, H * N * N)

    xp = x_prompt.reshape(B * T, D)
    xs = x_sample.reshape(DB, D)
    outs = [[] for _ in range(12)]

    for l in range(L):
        lw = dict(q_lora_g=row2(q_lora_g, l), w_uq=w_uq_p[l], kv_lora_g=row2(kv_lora_g, l), qn_g=row2(qn_g, l),
                  qr_g=qr_g_p[l].reshape(1, -1), kn_g=row2(kn_g, l), kr_g=kr_g_p[l].reshape(1, -1),
                  w_uk=w_uk_b[l], w_uv=w_uv_b[l],
                  rw_mu=mu_p[l].reshape(1, -1), rw_w0=row2(rw_w0, l), rw_w_up=rw_w_up_p[l], rw_a0=row2(rw_a0, l),
                  rw_a_up=rw_a_up_p[l], rw_g_up=rw_g_up_b[l], rw_k_k=row2(rw_k_k, l), rw_k_a=row2(rw_k_a, l),
                  rw_r_k=row2(rw_r_k, l), rw_lnx_g=row2(rw_lnx_g, l), rw_lnx_b=row2(rw_lnx_b, l))
        g_mix, g_ffn = row2(norm_mix_g, l), row2(norm_ffn_g, l)
        ps = row2(pool_scale, l)
        cw, cb = ffn_conv_w[l], row2(ffn_conv_b, l)
        mods_p = [mod[l, DB:DB + B, k * D:(k + 1) * D][:, None, :] for k in range(6)]
        mods_s = [mod[l, :DB, k * D:(k + 1) * D][None] for k in range(6)]

        sh1, sc1, g1, sh2, sc2, g2 = mods_p
        p = _norm_mod_matmul(xp, g_mix, sc1, sh1, w_in_p[l], T)
        q, k, v, lat, kr = _mla_proj(p, cos_p, sin_p, lw, T, BF16)
        ya = _attn_prompt(q, k, v, B, T)
        r_, wl_, k_, v_, kk_, b_, g_ = _rw_prep(p, None, lw, T)
        y_raw, st_t = _rw_chunked((r_, wl_, k_, v_, kk_, b_), B, T)
        yb = _rw_post(y_raw, r_, k_, v_, g_, lw)
        yc = _pool_prompt(p, w_pool_b[l], ps, T)
        xp = _matmul_gate_res(jnp.concatenate([ya, yb, yc], axis=1), w_out_b[l], g1, xp, T)
        act, tail_v, tail_g = _ffn_up_prompt(xp, g_ffn, sc2, sh2, w_up_b[l], cw, cb, B, T)
        xp = _matmul_gate_res(act, w_down_b[l], g2, xp, T)

        p3 = p.reshape(B, T, P_COLS)
        outs[0].append(lat.reshape(B, T, LANE))
        outs[2].append(kr.reshape(B, T, LANE)[:, :, :QK_ROPE])
        outs[4].append(jnp.swapaxes(st_t, -1, -2))
        outs[6].append(_gather_cols(p3[:, -1, :PB_W], pb_inverse))
        pool_rows = p3[:, max(T - POOL_BUF, 0):, PC_OFF:PC_OFF + 512]
        if T < POOL_BUF:
            pool_rows = jnp.concatenate([jnp.zeros((B, POOL_BUF - T, 512), F32), pool_rows], axis=1)
        outs[8].append(pool_rows)
        last = lambda t: t.reshape(B, -1, 8, F)[:, -1, 6:8]
        outs[10].append(jnp.concatenate([last(tail_v), last(tail_g)], axis=-1))

        sh1, sc1, g1, sh2, sc2, g2 = mods_s
        p = _norm_mod_matmul(xs, g_mix, sc1, sh1, w_in_p[l], DB)
        q, _, _, lat, kr = _mla_proj(p, cos_s, sin_s, lw, DB, F32)
        qt, qrp = _absorb_q(q, lw["kn_g"], w_uk_f[l])
        o_lat = _paged_attention(page_table, jnp.swapaxes(qt, 0, 1), jnp.swapaxes(qrp, 0, 1), lat, kr, w_uk_t[l],
                                 cache_latent, cache_krope_t, l)
        ya = _uv_out(jnp.swapaxes(o_lat, 0, 1), lw["w_uv"])
        r_, wl_, k_, v_, kk_, b_, g_ = _rw_prep(p, shift_p[l], lw, DB)
        heads = lambda a: jnp.swapaxes(a.reshape(DB, H, N), 0, 1)
        s_new, y_h = _rw_step(state_wkv_flat, l, [heads(a) for a in (r_, wl_, k_, v_, kk_, b_)])
        y_raw = jnp.swapaxes(y_h, 0, 1).reshape(DB, H * N)
        yb = _rw_post(y_raw, r_, k_, v_, g_, lw)
        yc = _pool_step(jnp.swapaxes(state_pool[l], 0, 1), p, w_pool_b[l], ps, past)
        xs = _matmul_gate_res(jnp.concatenate([ya, yb, yc], axis=1), w_out_b[l], g1, xs, DB)
        act, u_v, u_g = _ffn_up_step(xs, g_ffn, sc2, sh2, w_up_b[l], cw, cb, state_conv[l, :, 0], state_conv[l, :, 1])
        xs = _matmul_gate_res(act, w_down_b[l], g2, xs, DB)

        outs[1].append(lat.reshape(DB, 1, LANE))
        outs[3].append(kr.reshape(DB, 1, LANE)[:, :, :QK_ROPE])
        outs[5].append(s_new.reshape(DB, H, N, N))
        outs[7].append(_gather_cols(p[:, :PB_W], pb_inverse))
        outs[9].append(jnp.concatenate([state_pool[l][:, 1:], p[:, None, PC_OFF:PC_OFF + 512]], axis=1))
        outs[11].append(jnp.stack([state_conv[l, :, 1], jnp.concatenate([u_v, u_g], axis=-1)], axis=1))

    stacked = [jnp.stack(o) for o in outs]
    return (xp.reshape(B, T, D), xs.reshape(DB, 1, D), *stacked)
```

```python
import functools

import numpy as np
import jax
import jax.numpy as jnp
from jax import lax
from jax.experimental import pallas as pl
from jax.experimental.pallas import tpu as pltpu

F32 = jnp.float32
BF16 = jnp.bfloat16

MLA_HEADS = 8
QK_NOPE = 128
QK_ROPE = 64
V_HEAD = 128
ROPE_THETA = 10000.0
MLA_SCALE = (QK_NOPE + QK_ROPE) ** -0.5
RW_HEADS = 8
RW_N = 64
DECAY_LORA = 96
ICLR_LORA = 96
GATE_LORA = 256
GN_EPS = 64e-5
NORM_EPS = 1e-6
POOL_WINDOWS = (2, 4, 8, 16)
POOL_BUF = 15
PAGE = 128

LANE = 128
QK_PAD = 256
VMEM_LIMIT = 48 * 1024 * 1024

RW_W = RW_HEADS * RW_N
PB_W = 2048
CQ_OFF = PB_W
PC_OFF = CQ_OFF + 512
CKV_OFF = PC_OFF + 512
KR_OFF = CKV_OFF + 128
P_COLS = KR_OFF + 128

NN = (((1,), (0,)), ((), ()))
NT = (((1,), (1,)), ((), ()))
TN = (((0,), (0,)), ((), ()))


def _cp(*sem):
    return pltpu.CompilerParams(dimension_semantics=sem, vmem_limit_bytes=VMEM_LIMIT)


def _pallas(body, **kw):
    call = pl.pallas_call(body, **kw)

    def run(*args):
        if any(isinstance(a, jax.core.Tracer) for a in args):
            args = [a if a.dtype == jnp.int32 else pltpu.with_memory_space_constraint(a, pltpu.HBM) for a in args]
        return call(*args)
    return run


def _tile(n, pref):
    if n <= pref:
        return n
    t = pref - pref % 8
    while t >= 8:
        if n % t == 0:
            return t
        t -= 8
    return n


def _dot(a, b, dims=NN):
    return lax.dot_general(a, b, dims, preferred_element_type=F32)


def _split2(x):
    hi = x.astype(BF16)
    lo = (x - hi.astype(F32)).astype(BF16)
    return hi, lo


def _split3(x):
    hi = x.astype(BF16)
    r1 = x - hi.astype(F32)
    mid = r1.astype(BF16)
    lo = (r1 - mid.astype(F32)).astype(BF16)
    return hi, mid, lo


def _dot3(a, b, dims=NN):
    (ca,), (cb,) = dims[0]
    ah = a.astype(BF16).astype(F32)
    bh = b.astype(BF16).astype(F32)
    a3 = jnp.concatenate([ah, a - ah, ah], axis=ca).astype(BF16)
    b3 = jnp.concatenate([bh, bh, b - bh], axis=cb).astype(BF16)
    return _dot(a3, b3, dims)


def _dot_sel_l(sel, x, dims=NN):
    h, m, l = _split3(x)
    return _dot(sel, h, dims) + (_dot(sel, m, dims) + _dot(sel, l, dims))


def _dot_sel_r(x, sel, dims=NN):
    h, m, l = _split3(x)
    return _dot(h, sel, dims) + (_dot(m, sel, dims) + _dot(l, sel, dims))


def _sigmoid(x):
    return 1.0 / (1.0 + jnp.exp(-x))


def _rms(x, g, n):
    return (x * lax.rsqrt(jnp.sum(x * x, axis=-1, keepdims=True) / n + NORM_EPS)) * g


def _modnorm(x, g, sc, sh):
    return _rms(x, g, x.shape[-1]) * (1.0 + sc) + sh


def _mod_spec(tm, tn, tpg, rows, col_arg):
    if rows == 1:
        if col_arg:
            return pl.BlockSpec((1, 1, tn), lambda i, j: (i // tpg, 0, j))
        return pl.BlockSpec((1, 1, tn), lambda i, j: (i // tpg, 0, 0))
    if col_arg:
        return pl.BlockSpec((1, tm, tn), lambda i, j: (i // tpg, i % tpg, j))
    return pl.BlockSpec((1, tm, tn), lambda i, j: (i // tpg, i % tpg, 0))


def _ada_kernel(c_ref, w_ref, b_ref, o_ref):
    c = c_ref[...]
    s = c * _sigmoid(c)
    o_ref[0] = _dot(s.astype(BF16), w_ref[0].astype(BF16)) + b_ref[0]


def _ada_mod(c_all, ada_w, ada_b):
    L, D, N = ada_w.shape
    mc = c_all.shape[0]
    tn = _tile(N, 1024)
    return _pallas(
        _ada_kernel,
        out_shape=jax.ShapeDtypeStruct((L, mc, N), F32),
        grid=(L, N // tn),
        in_specs=[pl.BlockSpec((mc, D), lambda l, j: (0, 0)),
                  pl.BlockSpec((1, D, tn), lambda l, j: (l, 0, j)),
                  pl.BlockSpec((1, 1, tn), lambda l, j: (l, 0, j))],
        out_specs=pl.BlockSpec((1, mc, tn), lambda l, j: (l, 0, j)),
        compiler_params=_cp("parallel", "parallel"),
        name="ada_mod",
    )(c_all, ada_w, ada_b.reshape(L, 1, N))


def _nmm_kernel(x_ref, g_ref, sc_ref, sh_ref, w_ref, o_ref, h_ref):
    @pl.when(pl.program_id(1) == 0)
    def _():
        h_ref[...] = _modnorm(x_ref[...], g_ref[...], sc_ref[0], sh_ref[0]).astype(BF16)

    o_ref[...] = _dot(h_ref[...], w_ref[...])


def _norm_mod_matmul(x, g, sc, sh, w, layer, rpg, tm_pref=512):
    M, D = x.shape
    N = w.shape[2]
    tm = _tile(rpg, tm_pref)
    tpg = rpg // tm
    tn = N // 2 if (N // 2) % LANE == 0 else N
    rows = sc.shape[1]
    return _pallas(
        _nmm_kernel,
        out_shape=jax.ShapeDtypeStruct((M, N), F32),
        grid=(M // tm, N // tn),
        in_specs=[pl.BlockSpec((tm, D), lambda i, j: (i, 0)),
                  pl.BlockSpec((1, D), lambda i, j: (0, 0)),
                  _mod_spec(tm, D, tpg, rows, False),
                  _mod_spec(tm, D, tpg, rows, False),
                  pl.BlockSpec((None, D, tn), lambda i, j: (layer, 0, j))],
        out_specs=pl.BlockSpec((tm, tn), lambda i, j: (i, j)),
        scratch_shapes=[pltpu.VMEM((tm, D), BF16)],
        compiler_params=_cp("parallel", "arbitrary"),
        name="norm_mod_matmul",
    )(x, g, sc, sh, w)


def _mmres_kernel(*refs, widths):
    n = len(widths)
    a_refs, (w_ref, gate_ref, res_ref, o_ref) = refs[:n], refs[n:]
    acc, k0 = None, 0
    for a_ref, k in zip(a_refs, widths):
        part = _dot(a_ref[...], w_ref[k0:k0 + k, :])
        acc = part if acc is None else acc + part
        k0 += k
    o_ref[...] = res_ref[...] + gate_ref[0] * acc


def _matmul_gate_res(parts, w, layer, gate, res, rpg, tm_pref=512, tn_pref=512):
    M = parts[0].shape[0]
    widths = tuple(a.shape[1] for a in parts)
    K, N = w.shape[1:]
    assert sum(widths) == K
    tm = _tile(rpg, tm_pref)
    tpg = rpg // tm
    tn = _tile(N, tn_pref)
    rows = gate.shape[1]
    return _pallas(
        functools.partial(_mmres_kernel, widths=widths),
        out_shape=jax.ShapeDtypeStruct((M, N), F32),
        grid=(M // tm, N // tn),
        in_specs=[pl.BlockSpec((tm, k), lambda i, j: (i, 0)) for k in widths]
        + [pl.BlockSpec((None, K, tn), lambda i, j: (layer, 0, j)),
           _mod_spec(tm, tn, tpg, rows, True),
           pl.BlockSpec((tm, tn), lambda i, j: (i, j))],
        out_specs=pl.BlockSpec((tm, tn), lambda i, j: (i, j)),
        compiler_params=_cp("parallel", "arbitrary"),
        name="matmul_gate_res",
    )(*parts, w, gate, res)


def _mla_proj_kernel(cq_ref, ckv_ref, kr_ref, cos_ref, sin_ref, qlg_ref, wuq_ref, kvg_ref, qng_ref, qrg_ref,
                     kng_ref, krg_ref, wuk_ref, wuv_ref, q_ref, k_ref, v_ref, lat_ref, kro_ref):
    cos = cos_ref[...]
    sin = sin_ref[...]
    lane = lax.broadcasted_iota(jnp.int32, cos.shape, 1)
    half = QK_ROPE // 2

    def rope(x):
        rot = jnp.where(lane < half, -pltpu.roll(x, LANE - half, 1), pltpu.roll(x, half, 1))
        return x * cos + rot * sin

    cqn = _rms(cq_ref[...], qlg_ref[...], cq_ref.shape[-1])
    q = _dot(cqn.astype(BF16), wuq_ref[...])
    lat = _rms(ckv_ref[...], kvg_ref[...], ckv_ref.shape[-1])
    lat_ref[...] = lat
    kr = rope(_rms(kr_ref[...], krg_ref[...], QK_ROPE))
    kro_ref[...] = kr
    latb = lat.astype(BF16)
    kx = _dot(latb, wuk_ref[...])
    v_ref[...] = _dot(latb, wuv_ref[...]).astype(v_ref.dtype)
    krb = kr.astype(k_ref.dtype)
    for h in range(MLA_HEADS):
        o = h * QK_PAD
        qn = _rms(q[:, o:o + QK_NOPE], qng_ref[...], QK_NOPE)
        qr = rope(_rms(q[:, o + QK_NOPE:o + QK_PAD], qrg_ref[...], QK_ROPE))
        q_ref[:, o:o + QK_NOPE] = qn.astype(q_ref.dtype)
        q_ref[:, o + QK_NOPE:o + QK_PAD] = qr.astype(q_ref.dtype)
        kn = _rms(kx[:, h * QK_NOPE:(h + 1) * QK_NOPE], kng_ref[...], QK_NOPE)
        k_ref[:, o:o + QK_NOPE] = kn.astype(k_ref.dtype)
        k_ref[:, o + QK_NOPE:o + QK_PAD] = krb


def _mla_proj(p, cos_t, sin_t, lw, rpg, q_dtype, tm_pref=256):
    M = p.shape[0]
    tm = _tile(rpg, tm_pref)
    tpg = rpg // tm
    H = MLA_HEADS
    row = lambda w: pl.BlockSpec((tm, w), lambda i: (i, 0))
    const = lambda a: pl.BlockSpec(a.shape, lambda i: (0,) * a.ndim)
    tab = pl.BlockSpec((tm, LANE), lambda i: (i % tpg, 0))
    weights = (lw["q_lora_g"], lw["w_uq"], lw["kv_lora_g"], lw["qn_g"], lw["qr_g"], lw["kn_g"], lw["kr_g"],
               lw["w_uk"], lw["w_uv"])
    return _pallas(
        _mla_proj_kernel,
        out_shape=(jax.ShapeDtypeStruct((M, H * QK_PAD), q_dtype),
                   jax.ShapeDtypeStruct((M, H * QK_PAD), BF16),
                   jax.ShapeDtypeStruct((M, H * V_HEAD), BF16),
                   jax.ShapeDtypeStruct((M, LANE), F32),
                   jax.ShapeDtypeStruct((M, LANE), F32)),
        grid=(M // tm,),
        in_specs=[pl.BlockSpec((tm, 512), lambda i: (i, CQ_OFF // 512)),
                  pl.BlockSpec((tm, LANE), lambda i: (i, CKV_OFF // LANE)),
                  pl.BlockSpec((tm, LANE), lambda i: (i, KR_OFF // LANE)),
                  tab, tab] + [const(a) for a in weights],
        out_specs=(row(H * QK_PAD), row(H * QK_PAD), row(H * V_HEAD), row(LANE), row(LANE)),
        compiler_params=_cp("parallel"),
        name="mla_proj",
    )(p, p, p, cos_t, sin_t, *weights)


def _attn_kernel(q_ref, k_ref, v_ref, o_ref, *, tq):
    qi = pl.program_id(2)
    q = q_ref[...]

    def block(kj, carry, causal):
        m, l, acc = carry
        start = pl.multiple_of(kj * tq, tq)
        k = k_ref[pl.ds(start, tq), :]
        v = v_ref[pl.ds(start, tq), :]
        s = _dot(q, k, NT) * MLA_SCALE
        if causal:
            row = lax.broadcasted_iota(jnp.int32, (tq, tq), 0)
            col = lax.broadcasted_iota(jnp.int32, (tq, tq), 1)
            s = jnp.where(col <= row, s, -jnp.inf)
        m_new = jnp.maximum(m, jnp.max(s, axis=-1, keepdims=True))
        alpha = jnp.exp(m - m_new)
        e = jnp.exp(s - m_new)
        l = alpha * l + jnp.sum(e, axis=-1, keepdims=True)
        acc = alpha * acc + _dot(e.astype(BF16), v)
        return m_new, l, acc

    init = (jnp.full((tq, 1), -jnp.inf, F32), jnp.zeros((tq, 1), F32), jnp.zeros((tq, V_HEAD), F32))
    carry = block(qi, init, True)
    m, l, acc = lax.fori_loop(0, qi, lambda kj, c: block(kj, c, False), carry)
    o_ref[...] = (acc / l).astype(o_ref.dtype)


def _attn_prompt(q, k, v, B, T, tq_pref=512):
    H = MLA_HEADS
    tq = _tile(T, tq_pref)
    nq = T // tq
    return _pallas(
        functools.partial(_attn_kernel, tq=tq),
        out_shape=jax.ShapeDtypeStruct((B * T, H * V_HEAD), BF16),
        grid=(B, H, nq),
        in_specs=[pl.BlockSpec((tq, QK_PAD), lambda b, h, i: (b * nq + i, h)),
                  pl.BlockSpec((T, QK_PAD), lambda b, h, i: (b, h)),
                  pl.BlockSpec((T, V_HEAD), lambda b, h, i: (b, h))],
        out_specs=pl.BlockSpec((tq, V_HEAD), lambda b, h, i: (b * nq + i, h)),
        compiler_params=_cp("parallel", "parallel", "arbitrary"),
        name="attn_prompt",
    )(q, k, v)


def _absorb_kernel(q_ref, kng_ref, wuk_ref, qt_ref, qr_ref):
    g = kng_ref[...]
    for h in range(MLA_HEADS):
        o = h * QK_PAD
        qn = q_ref[:, o:o + QK_NOPE] * g
        wh = wuk_ref[:, h * QK_NOPE:(h + 1) * QK_NOPE]
        qt_ref[h] = _dot3(qn, wh, NT)
        qr_ref[h] = q_ref[:, o + QK_NOPE:o + QK_PAD]


def _absorb_q(q, kn_g, w_uk):
    M = q.shape[0]
    H = MLA_HEADS
    return _pallas(
        _absorb_kernel,
        out_shape=(jax.ShapeDtypeStruct((H, M, LANE), F32), jax.ShapeDtypeStruct((H, M, LANE), F32)),
        compiler_params=pltpu.CompilerParams(vmem_limit_bytes=VMEM_LIMIT),
        name="absorb_q",
    )(q, kn_g, w_uk)


def _paged_kernel(pt_ref, qt_ref, qr_ref, latn_ref, krn_ref, wukt_ref, lat_hbm, kr_hbm, o_ref,
                  latbuf, krbuf, lhs_ref, s_ref, sem, *, layer, n_pages, chunk_pages):
    b = pl.program_id(0)
    nb = pl.num_programs(0)
    slot = b % 2
    H = MLA_HEADS
    HD = H * QK_NOPE
    n_chunks = n_pages // chunk_pages
    rows = chunk_pages * PAGE
    R = latbuf.shape[-1]

    def page_copies(seq, sl, c, j):
        p = c * chunk_pages + j
        pg = pt_ref[seq * n_pages + p]
        return (pltpu.make_async_copy(lat_hbm.at[layer, pg], latbuf.at[sl, p], sem.at[sl, 0, c]),
                pltpu.make_async_copy(kr_hbm.at[layer, pg],
                                      krbuf.at[sl, :, pl.ds(pl.multiple_of(p * PAGE, PAGE), PAGE)], sem.at[sl, 1, c]))

    def fetch_chunk(seq, sl, c):
        for j in range(chunk_pages):
            for cp in page_copies(seq, sl, c, j):
                cp.start()

    def wait_chunk(seq, sl, c):
        for j in range(chunk_pages):
            for cp in page_copies(seq, sl, c, j):
                cp.wait()

    @pl.when(b == 0)
    def _():
        def body(c, carry):
            fetch_chunk(0, 0, c)
            return carry
        lax.fori_loop(0, n_chunks, body, 0)

    lhs_ref[pl.ds(0, HD), :] = wukt_ref[...]
    lhs_ref[pl.ds(HD, 16), :] = jnp.concatenate([qt_ref[0], jnp.zeros_like(qt_ref[0])], axis=0).astype(BF16)
    lhs = lhs_ref[...]
    qr = qr_ref[0][:, :QK_ROPE].astype(BF16)

    def latent_scores(latb):
        xt = _dot(lhs, latb, NT)
        ms = []
        for h in range(H):
            acc = None
            for i in range(QK_NOPE // 8):
                blk = xt[h * QK_NOPE + 8 * i:h * QK_NOPE + 8 * i + 8]
                acc = blk * blk if acc is None else acc + blk * blk
            ms.append(jnp.sum(acc, axis=0, keepdims=True))
        ms = jnp.concatenate(ms, axis=0)
        return xt[HD:HD + H] * lax.rsqrt(ms / QK_NOPE + NORM_EPS)

    nxt = jnp.minimum(b + 1, nb - 1)

    def wait_body(c, carry):
        wait_chunk(b, slot, c)
        return carry
    lax.fori_loop(0, n_chunks, wait_body, 0)

    def score_chunk(c, carry):
        fetch_chunk(nxt, 1 - slot, c)

        p0 = pl.multiple_of(c * chunk_pages, chunk_pages)
        k0 = pl.multiple_of(c * rows, rows)
        s_rope = _dot(qr, krbuf[slot, :, pl.ds(k0, rows)].astype(BF16))
        latb = latbuf[slot, pl.ds(p0, chunk_pages)].reshape(rows, R).astype(BF16)
        s_ref[:, pl.ds(k0, rows)] = (latent_scores(latb) + s_rope) * MLA_SCALE
        return carry

    unroll = 4 if n_chunks % 4 == 0 else 1
    lax.fori_loop(0, n_chunks, score_chunk, 0, unroll=unroll)

    latn = latn_ref[0]
    latnb = jnp.concatenate([latn, jnp.zeros((PAGE - latn.shape[0], R), F32)], axis=0).astype(BF16)
    krn = krn_ref[0][0:1, :QK_ROPE].astype(BF16).astype(F32)
    s_rope = jnp.sum(qr.astype(F32) * krn, axis=-1, keepdims=True)
    s_new = (latent_scores(latnb) + s_rope) * MLA_SCALE
    s_new = jnp.where(lax.broadcasted_iota(jnp.int32, s_new.shape, 1) == 0, s_new, -jnp.inf)

    s_all = s_ref[...]
    m = jnp.maximum(jnp.max(s_all, axis=-1, keepdims=True), jnp.max(s_new, axis=-1, keepdims=True))
    e_new = jnp.exp(s_new - m)
    s_ref[...] = jnp.exp(s_all - m)
    l = jnp.sum(s_ref[...], axis=-1, keepdims=True) + jnp.sum(e_new, axis=-1, keepdims=True)

    def value_chunk(c, acc):
        p0 = pl.multiple_of(c * chunk_pages, chunk_pages)
        k0 = pl.multiple_of(c * rows, rows)
        latb = latbuf[slot, pl.ds(p0, chunk_pages)].reshape(rows, R).astype(BF16)
        return acc + _dot(s_ref[:, pl.ds(k0, rows)].astype(BF16), latb)

    acc = lax.fori_loop(0, n_chunks, value_chunk, _dot(e_new.astype(BF16), latnb), unroll=unroll)
    o_ref[0] = acc / l

    @pl.when(b == nb - 1)
    def _():
        def body(c, carry):
            wait_chunk(nxt, 1 - slot, c)
            return carry
        lax.fori_loop(0, n_chunks, body, 0)


def _paged_attention(page_table, qt, qr, lat_new, kr_new, w_uk_t, cache_latent, cache_krope_t, layer):
    DB, n_pages = page_table.shape
    H = MLA_HEADS
    R = cache_latent.shape[-1]
    chunk_pages = 8 if n_pages % 8 == 0 else 1
    pad8 = lambda a: jnp.pad(a[:, None, :], ((0, 0), (0, 7), (0, 0)))
    seq = lambda w: pl.BlockSpec((1, 8, w), lambda b, pt: (b, 0, 0))
    grid_spec = pltpu.PrefetchScalarGridSpec(
        num_scalar_prefetch=1,
        grid=(DB,),
        in_specs=[seq(LANE), seq(LANE), seq(LANE), seq(LANE),
                  pl.BlockSpec(w_uk_t.shape, lambda b, pt: (0, 0)),
                  pl.BlockSpec(memory_space=pl.ANY),
                  pl.BlockSpec(memory_space=pl.ANY)],
        out_specs=pl.BlockSpec((1, H, R), lambda b, pt: (b, 0, 0)),
        scratch_shapes=[pltpu.VMEM((2, n_pages, PAGE, R), F32),
                        pltpu.VMEM((2, QK_ROPE, n_pages * PAGE), F32),
                        pltpu.VMEM((w_uk_t.shape[0] + 16, R), BF16),
                        pltpu.VMEM((H, n_pages * PAGE), F32),
                        pltpu.SemaphoreType.DMA((2, 2, n_pages // chunk_pages))],
    )
    return _pallas(
        functools.partial(_paged_kernel, layer=layer, n_pages=n_pages, chunk_pages=chunk_pages),
        out_shape=jax.ShapeDtypeStruct((DB, H, R), F32),
        grid_spec=grid_spec,
        compiler_params=_cp("arbitrary"),
        name="paged_attention",
    )(page_table.reshape(-1), qt, qr, pad8(lat_new), pad8(kr_new), w_uk_t, cache_latent, cache_krope_t)


def _uv_kernel(o_ref, wuv_ref, y_ref):
    for h in range(MLA_HEADS):
        y_ref[:, h * V_HEAD:(h + 1) * V_HEAD] = _dot(
            o_ref[h].astype(BF16), wuv_ref[:, h * V_HEAD:(h + 1) * V_HEAD]).astype(y_ref.dtype)


def _uv_out(o_lat, w_uv):
    H, M, _ = o_lat.shape
    return _pallas(
        _uv_kernel,
        out_shape=jax.ShapeDtypeStruct((M, H * V_HEAD), BF16),
        compiler_params=pltpu.CompilerParams(vmem_limit_bytes=VMEM_LIMIT),
        name="uv_out",
    )(o_lat, w_uv)


def _rw_prep_body(pb, prev, mu_ref, w0_ref, wup_ref, a0_ref, aup_ref, gup_ref, kk_ref, ka_ref, seg_ref,
                  r_o, wl_o, k_o, v_o, kk_o, b_o, g_o):
    W = RW_W
    xs = pb + (prev - pb) * mu_ref[...]
    r = xs[:, 0:W]
    k = xs[:, W:2 * W]
    v = xs[:, 2 * W:3 * W]
    wd = xs[:, 3 * W:3 * W + LANE]
    ad = xs[:, 3 * W + LANE:3 * W + 2 * LANE]
    gd = xs[:, 3 * W + 2 * LANE:]
    z = -(w0_ref[...] + _dot(jnp.tanh(wd).astype(BF16), wup_ref[...]))
    softplus = jnp.maximum(z, 0.0) + jnp.log1p(jnp.exp(-jnp.abs(z)))
    w = -softplus - 0.5
    a = _sigmoid(a0_ref[...] + _dot(ad.astype(BF16), aup_ref[...]))
    g = _dot(_sigmoid(gd).astype(BF16), gup_ref[...])
    kk = k * kk_ref[...]
    k2 = k * (1.0 + (a - 1.0) * ka_ref[...])
    n2 = _dot_sel_r(kk * kk, seg_ref[...])
    kkn = kk / jnp.maximum(jnp.sqrt(n2), 1e-12)
    r_o[...] = r
    wl_o[...] = -jnp.exp(w)
    k_o[...] = k2
    v_o[...] = v
    kk_o[...] = kkn
    b_o[...] = kkn * a
    g_o[...] = g


def _rw_prep_seq_kernel(pb_ref, halo_ref, *rest, tpg):
    pb = pb_ref[...]
    i = pl.program_id(0)
    rolled = pltpu.roll(pb, 1, 0)
    row = lax.broadcasted_iota(jnp.int32, pb.shape, 0)
    first_row = jnp.where(i % tpg == 0, 0, -1)
    prev = jnp.where(row == 0, halo_ref[7:8, :], rolled)
    prev = jnp.where(row == first_row, 0.0, prev)
    _rw_prep_body(pb, prev, *rest)


def _rw_prep_step_kernel(pb_ref, prev_ref, *rest):
    _rw_prep_body(pb_ref[...], prev_ref[...], *rest)


def _seg_matrix(n, w):
    return jnp.asarray(np.kron(np.eye(n // w, dtype=np.float32), np.ones((w, w), np.float32)), BF16)


def _rw_prep(p, prev, lw, rpg, tm_pref=256):
    M = p.shape[0]
    tm = _tile(rpg, tm_pref)
    tpg = rpg // tm
    seg = _seg_matrix(RW_W, RW_N)
    weights = (lw["rw_mu"], lw["rw_w0"], lw["rw_w_up"], lw["rw_a0"], lw["rw_a_up"], lw["rw_g_up"], lw["rw_k_k"],
               lw["rw_k_a"], seg)
    const = lambda a: pl.BlockSpec(a.shape, lambda i: (0,) * a.ndim)
    out = jax.ShapeDtypeStruct((M, RW_W), F32)
    ospec = pl.BlockSpec((tm, RW_W), lambda i: (i, 0))
    main = pl.BlockSpec((tm, PB_W), lambda i: (i, 0))
    if prev is None:
        kern = functools.partial(_rw_prep_seq_kernel, tpg=tpg)
        second = pl.BlockSpec((8, PB_W), lambda i: (jnp.maximum(i * (tm // 8) - 1, 0), 0))
        second_arg = p
    else:
        kern = _rw_prep_step_kernel
        second = pl.BlockSpec((tm, PB_W), lambda i: (i, 0))
        second_arg = prev
    return _pallas(
        kern,
        out_shape=(out,) * 7,
        grid=(M // tm,),
        in_specs=[main, second] + [const(a) for a in weights],
        out_specs=(ospec,) * 7,
        compiler_params=_cp("parallel"),
        name="rwkv_prep",
    )(p, second_arg, *weights)


def _rw_chunk_kernel(r_ref, wl_ref, k_ref, v_ref, kk_ref, b_ref, y_ref, s_ref, st_ref, *, C, G):
    step = pl.program_id(1)

    @pl.when(step == 0)
    def _():
        st_ref[...] = jnp.zeros_like(st_ref)

    N = RW_N
    ri = lax.broadcasted_iota(jnp.int32, (C, C), 0)
    ci = lax.broadcasted_iota(jnp.int32, (C, C), 1)
    incl = ci <= ri
    strict = ci < ri
    eye_c = (ci == ri).astype(F32)
    rn = lax.broadcasted_iota(jnp.int32, (N, N), 0)
    cn = lax.broadcasted_iota(jnp.int32, (N, N), 1)
    eye_n = rn == cn
    rg = lax.broadcasted_iota(jnp.int32, (G * C, G * C), 0)
    cg = lax.broadcasted_iota(jnp.int32, (G * C, G * C), 1)
    same_chunk_incl = (cg <= rg) & (rg // C == cg // C)

    wl = wl_ref[...]
    cs = _dot_sel_l(same_chunk_incl.astype(BF16), wl)
    p_in = jnp.exp(cs)
    p_ex = jnp.exp(cs - wl)
    p_inv = jnp.exp(-cs)
    at = -kk_ref[...] * p_ex
    rp = r_ref[...] * p_in
    bp = b_ref[...] * p_inv
    kp = k_ref[...] * p_inv
    v = v_ref[...]

    items = [(g, h) for g in range(G) for h in range(RW_HEADS)]
    cut = lambda x: [x[g * C:(g + 1) * C, h * N:(h + 1) * N] for g, h in items]
    p_end = [p_in[(g + 1) * C - 1:(g + 1) * C, h * N:(h + 1) * N] for g, h in items]
    at_i, rp_i, bp_i, kp_i, v_i = (cut(x) for x in (at, rp, bp, kp, v))
    n = range(len(items))
    lhs = [jnp.concatenate([at_i[i], rp_i[i]], axis=0) for i in n]
    gb = [_dot3(lhs[i], bp_i[i], NT) for i in n]
    gk = [_dot3(lhs[i], kp_i[i], NT) for i in n]
    a_ab = [jnp.where(strict, gb[i][:C], 0.0) for i in n]
    a_rb = [jnp.where(incl, gb[i][C:], 0.0) for i in n]
    a_ak = [jnp.where(strict, gk[i][:C], 0.0) for i in n]
    a_rk = [jnp.where(incl, gk[i][C:], 0.0) for i in n]
    akv = [_dot3(a_ak[i], v_i[i]) for i in n]
    rkv = [_dot3(a_rk[i], v_i[i]) for i in n]
    kv = [_dot3(kp_i[i] * p_end[i], v_i[i], TN) for i in n]
    t = None
    m = 1
    while m < C:
        sh = m.bit_length() - 1
        low_left = ((ri >> (sh + 1)) == (ci >> (sh + 1))) & (((ri >> sh) & 1) == 1) & (((ci >> sh) & 1) == 0)
        a_off = [jnp.where(low_left, a_ab[i], 0.0) for i in n]
        if m == 1:
            t = [eye_c + a_off[i] for i in n]
        else:
            inner = [_dot3(a_off[i], t[i]) for i in n]
            t = [t[i] + _dot3(t[i], inner[i]) for i in n]
        m *= 2
    w = [_dot3(t[i], jnp.concatenate([at_i[i], akv[i]], axis=1)) for i in n]
    gg = [_dot3(a_rb[i], w[i]) for i in n]
    mm = [_dot3(bp_i[i] * p_end[i], w[i], TN) for i in n]
    top = [jnp.concatenate([rp_i[i] + gg[i][:, :N], jnp.where(eye_n, p_end[i], 0.0) + mm[i][:, :N]], axis=0)
           for i in n]
    st = [st_ref[h] for h in range(RW_HEADS)]
    for g in range(G):
        idx = [g * RW_HEADS + h for h in range(RW_HEADS)]
        res = [_dot3(top[i], st[h]) for h, i in enumerate(idx)]
        st = [res[h][C:] + (mm[i][:, N:] + kv[i]) for h, i in enumerate(idx)]
        y_ref[pl.ds(g * C, C), :] = jnp.concatenate(
            [res[h][:C] + (gg[i][:, N:] + rkv[i]) for h, i in enumerate(idx)], axis=1)
    for h in range(RW_HEADS):
        st_ref[h] = st[h]

    @pl.when(step == pl.num_programs(1) - 1)
    def _():
        s_ref[0] = st_ref[...]


def _rw_chunked(ops, B, T):
    C = _tile(T, 64)
    nc = T // C
    G = 4 if nc % 4 == 0 else (2 if nc % 2 == 0 else 1)
    ns = nc // G
    blk = pl.BlockSpec((G * C, RW_W), lambda b, s: (b * ns + s, 0))
    return _pallas(
        functools.partial(_rw_chunk_kernel, C=C, G=G),
        out_shape=(jax.ShapeDtypeStruct((B * T, RW_W), F32),
                   jax.ShapeDtypeStruct((B, RW_HEADS, RW_N, RW_N), F32)),
        grid=(B, ns),
        in_specs=[blk] * 6,
        out_specs=(blk, pl.BlockSpec((1, RW_HEADS, RW_N, RW_N), lambda b, s: (b, 0, 0, 0))),
        scratch_shapes=[pltpu.VMEM((RW_HEADS, RW_N, RW_N), F32)],
        compiler_params=_cp("parallel", "arbitrary"),
        name="rwkv_chunked",
    )(*ops)


def _rw_step_kernel(s_ref, r_ref, wl_ref, k_ref, v_ref, kk_ref, b_ref, ej_ref, ei_ref, rd_ref, so_ref, y_ref):
    s = s_ref[0]
    ej = ej_ref[...]
    ei = ei_ref[...]
    rd = rd_ref[...]
    over_j = lambda x: _dot_sel_r(x, ej)
    sa = _dot_sel_r(s * over_j(kk_ref[0]), rd)
    s_new = (s * over_j(jnp.exp(wl_ref[0])) - _dot_sel_r(sa, ei) * over_j(b_ref[0])
             + _dot_sel_r(v_ref[0], ei) * over_j(k_ref[0]))
    so_ref[0] = s_new
    y_ref[0] = _dot_sel_r(s_new * over_j(r_ref[0]), rd)


def _rw_step(state_wkv, layer, vecs):
    _, DB, _ = state_wkv.shape
    H, N = RW_HEADS, RW_N
    bt = _tile(DB, 128)
    ej = jnp.asarray(np.kron(np.ones((1, N), np.float32), np.eye(N, dtype=np.float32)), BF16)
    ei = jnp.asarray(np.kron(np.eye(N, dtype=np.float32), np.ones((1, N), np.float32)), BF16)
    rd = ei.T
    sblk = pl.BlockSpec((1, bt, N * N), lambda i, h: (layer, i, h))
    oblk = pl.BlockSpec((1, bt, N * N), lambda i, h: (0, i, h))
    vblk = pl.BlockSpec((1, bt, N), lambda i, h: (h, i, 0))
    const = lambda a: pl.BlockSpec(a.shape, lambda i, h: (0, 0))
    s_new, y = _pallas(
        _rw_step_kernel,
        out_shape=(jax.ShapeDtypeStruct((1, DB, H * N * N), F32), jax.ShapeDtypeStruct((H, DB, N), F32)),
        grid=(DB // bt, H),
        in_specs=[sblk] + [vblk] * 6 + [const(ej), const(ei), const(rd)],
        out_specs=(oblk, vblk),
        compiler_params=_cp("parallel", "parallel"),
        name="rwkv_step",
    )(state_wkv, *vecs, ej, ei, rd)
    return s_new[0], y


def _rw_post_kernel(y_ref, r_ref, k_ref, v_ref, g_ref, lng_ref, lnb_ref, rk_ref, seg_ref, o_ref):
    seg = seg_ref[...]
    y = y_ref[...]
    mu = _dot_sel_r(y, seg) / RW_N
    d = y - mu
    var = _dot_sel_r(d * d, seg) / RW_N
    yn = (d * lax.rsqrt(var + GN_EPS)) * lng_ref[...] + lnb_ref[...]
    bonus = _dot_sel_r(r_ref[...] * k_ref[...] * rk_ref[...], seg) * v_ref[...]
    o_ref[...] = ((yn + bonus) * g_ref[...]).astype(o_ref.dtype)


def _rw_post(y, r, k, v, g, lw, tm_pref=256):
    M = y.shape[0]
    tm = _tile(M, tm_pref)
    seg = _seg_matrix(RW_W, RW_N)
    row = pl.BlockSpec((tm, RW_W), lambda i: (i, 0))
    const = lambda a: pl.BlockSpec(a.shape, lambda i: (0,) * a.ndim)
    weights = (lw["rw_lnx_g"], lw["rw_lnx_b"], lw["rw_r_k"], seg)
    return _pallas(
        _rw_post_kernel,
        out_shape=jax.ShapeDtypeStruct((M, RW_W), BF16),
        grid=(M // tm,),
        in_specs=[row] * 5 + [const(a) for a in weights],
        out_specs=row,
        compiler_params=_cp("parallel"),
        name="rwkv_post",
    )(y, r, k, v, g, *weights)


def _pool_windows(x_ext, halo_rows, count_of, wp_ref, ps_ref, o_ref):
    G = len(POOL_WINDOWS)
    gw = x_ext.shape[1] // G
    for gi, win in enumerate(POOL_WINDOWS):
        xg = x_ext[:, gi * gw:(gi + 1) * gw]
        s = xg
        step = 1
        while step < win:
            s = s + pltpu.roll(s, step, 0)
            step *= 2
        diff = s[halo_rows:] / count_of(win) - xg[halo_rows:]
        o = _dot(diff.astype(BF16), wp_ref[gi]) * ps_ref[:, gi * gw:(gi + 1) * gw]
        o_ref[:, gi * gw:(gi + 1) * gw] = o.astype(o_ref.dtype)


def _pool_seq_kernel(pc_ref, halo_ref, wp_ref, ps_ref, o_ref, *, tm, tpg):
    i = pl.program_id(0)
    t0 = (i % tpg) * tm
    hrow = lax.broadcasted_iota(jnp.int32, (halo_ref.shape[0], 1), 0)
    halo = jnp.where(hrow < jnp.where(t0 == 0, halo_ref.shape[0], 0), 0.0, halo_ref[...])
    x_ext = jnp.concatenate([halo, pc_ref[...]], axis=0)
    pos = t0 + lax.broadcasted_iota(jnp.int32, (tm, 1), 0)
    count_of = lambda win: jnp.minimum(pos + 1, win).astype(F32)
    _pool_windows(x_ext, halo_ref.shape[0], count_of, wp_ref, ps_ref, o_ref)


def _pool_prompt(p, w_pool, pool_scale, rpg, tm_pref=256):
    M = p.shape[0]
    W = 512
    HALO = 16
    tm = _tile(rpg, tm_pref)
    tpg = rpg // tm
    return _pallas(
        functools.partial(_pool_seq_kernel, tm=tm, tpg=tpg),
        out_shape=jax.ShapeDtypeStruct((M, W), BF16),
        grid=(M // tm,),
        in_specs=[pl.BlockSpec((tm, W), lambda i: (i, PC_OFF // W)),
                  pl.BlockSpec((HALO, W), lambda i: (jnp.maximum(i * (tm // HALO) - 1, 0), PC_OFF // W)),
                  pl.BlockSpec(w_pool.shape, lambda i: (0, 0, 0)),
                  pl.BlockSpec(pool_scale.shape, lambda i: (0, 0))],
        out_specs=pl.BlockSpec((tm, W), lambda i: (i, 0)),
        compiler_params=_cp("parallel"),
        name="pool_prompt",
    )(p, p, w_pool, pool_scale)


def _pool_step_kernel(st_ref, pc_ref, wp_ref, ps_ref, o_ref, *, past):
    G = len(POOL_WINDOWS)
    pc = pc_ref[...]
    gw = pc.shape[1] // G
    nbuf = st_ref.shape[0]
    for gi, win in enumerate(POOL_WINDOWS):
        sl = slice(gi * gw, (gi + 1) * gw)
        xg = pc[:, sl]
        total = xg
        for j in range(1, win):
            total = total + st_ref[nbuf - j][:, sl]
        diff = total / float(min(past + 1, win)) - xg
        o = _dot(diff.astype(BF16), wp_ref[gi]) * ps_ref[:, sl]
        o_ref[:, sl] = o.astype(o_ref.dtype)


def _pool_step(state_t, p, w_pool, pool_scale, past):
    nbuf, DB, W = state_t.shape
    return _pallas(
        functools.partial(_pool_step_kernel, past=past),
        out_shape=jax.ShapeDtypeStruct((DB, W), BF16),
        grid=(1,),
        in_specs=[pl.BlockSpec((nbuf, DB, W), lambda i: (0, 0, 0)),
                  pl.BlockSpec((DB, W), lambda i: (0, PC_OFF // W)),
                  pl.BlockSpec(w_pool.shape, lambda i: (0, 0, 0)),
                  pl.BlockSpec(pool_scale.shape, lambda i: (0, 0))],
        out_specs=pl.BlockSpec((DB, W), lambda i: (0, 0)),
        compiler_params=_cp("arbitrary"),
        name="pool_step",
    )(state_t, p, w_pool, pool_scale)


FFN_HALO = 16
FFN_SUB = 256


def _ffn_up_seq_kernel(x_ref, xh_ref, g_ref, sc_ref, sh_ref, wv_ref, wg_ref, cwv_ref, cwg_ref, cbv_ref, cbg_ref,
                       act_ref, tv_ref, tg_ref, h_ref, *, tm, tpg):
    i = pl.program_id(0)

    @pl.when(pl.program_id(1) == 0)
    def _():
        g, sc, sh = g_ref[...], sc_ref[0], sh_ref[0]
        h_ref[pl.ds(0, FFN_HALO), :] = _modnorm(xh_ref[...], g, sc, sh).astype(BF16)
        h_ref[pl.ds(FFN_HALO, tm), :] = _modnorm(x_ref[...], g, sc, sh).astype(BF16)

    h = h_ref[...]
    row = lax.broadcasted_iota(jnp.int32, (tm + FFN_HALO, 1), 0)
    keep = row >= jnp.where(i % tpg == 0, FFN_HALO, 0)

    def branch(w_ref, cw_ref, cb_ref, t_ref, cs):
        u = jnp.where(keep, _dot(h, w_ref[:, cs]), 0.0)
        t_ref[0, :, cs] = u[tm + FFN_HALO - 8:, :]
        cw = cw_ref[:, cs]
        return (cb_ref[:, cs] + pltpu.roll(u, 2, 0)[FFN_HALO:] * cw[0:1] + pltpu.roll(u, 1, 0)[FFN_HALO:] * cw[1:2]
                + u[FFN_HALO:] * cw[2:3])

    tn = act_ref.shape[1]
    for c0 in range(0, tn, FFN_SUB):
        cs = slice(c0, min(c0 + FFN_SUB, tn))
        val = branch(wv_ref, cwv_ref, cbv_ref, tv_ref, cs)
        gate = branch(wg_ref, cwg_ref, cbg_ref, tg_ref, cs)
        act_ref[:, cs] = ((gate * _sigmoid(gate)) * val).astype(act_ref.dtype)


def _ffn_up_prompt(x, g, sc, sh, w_up, layer, conv_w, conv_b, B, T, tm_pref=512, tn_pref=1408):
    M, D = x.shape
    F = w_up.shape[2] // 2
    tm = _tile(T, tm_pref)
    tpg = T // tm
    tn = max(t for t in range(LANE, tn_pref + 1, LANE) if F % t == 0)
    nj = F // tn
    lo = lambda i, j: (0, j)
    hi = lambda i, j: (0, nj + j)
    tail = jax.ShapeDtypeStruct((M // tm, 8, F), F32)
    tspec = pl.BlockSpec((1, 8, tn), lambda i, j: (i, 0, j))
    return _pallas(
        functools.partial(_ffn_up_seq_kernel, tm=tm, tpg=tpg),
        out_shape=(jax.ShapeDtypeStruct((M, F), BF16), tail, tail),
        grid=(M // tm, nj),
        in_specs=[pl.BlockSpec((tm, D), lambda i, j: (i, 0)),
                  pl.BlockSpec((FFN_HALO, D), lambda i, j: (jnp.maximum(i * (tm // FFN_HALO) - 1, 0), 0)),
                  pl.BlockSpec((1, D), lambda i, j: (0, 0)),
                  _mod_spec(tm, D, tpg, 1, False),
                  _mod_spec(tm, D, tpg, 1, False),
                  pl.BlockSpec((None, D, tn), lambda i, j: (layer, 0, j)),
                  pl.BlockSpec((None, D, tn), lambda i, j: (layer, 0, nj + j)),
                  pl.BlockSpec((3, tn), lo), pl.BlockSpec((3, tn), hi),
                  pl.BlockSpec((1, tn), lo), pl.BlockSpec((1, tn), hi)],
        out_specs=(pl.BlockSpec((tm, tn), lambda i, j: (i, j)), tspec, tspec),
        scratch_shapes=[pltpu.VMEM((tm + FFN_HALO, D), BF16)],
        compiler_params=_cp("arbitrary", "arbitrary"),
        name="ffn_up_prompt",
    )(x, x, g, sc, sh, w_up, w_up, conv_w, conv_w, conv_b, conv_b)


def _ffn_up_step_kernel(x_ref, g_ref, sc_ref, sh_ref, wv_ref, wg_ref, cwv_ref, cwg_ref, cbv_ref, cbg_ref,
                        s0v_ref, s0g_ref, s1v_ref, s1g_ref, act_ref, uv_ref, ug_ref, h_ref):
    @pl.when(pl.program_id(0) == 0)
    def _():
        h_ref[...] = _modnorm(x_ref[...], g_ref[...], sc_ref[0], sh_ref[0]).astype(BF16)

    h = h_ref[...]

    def branch(w_ref, cw_ref, cb_ref, s0_ref, s1_ref, u_ref):
        u = _dot(h, w_ref[...])
        u_ref[...] = u
        cw = cw_ref[...]
        return cb_ref[...] + s0_ref[...] * cw[0:1] + s1_ref[...] * cw[1:2] + u * cw[2:3]

    val = branch(wv_ref, cwv_ref, cbv_ref, s0v_ref, s1v_ref, uv_ref)
    gate = branch(wg_ref, cwg_ref, cbg_ref, s0g_ref, s1g_ref, ug_ref)
    act_ref[...] = ((gate * _sigmoid(gate)) * val).astype(act_ref.dtype)


def _ffn_up_step(x, g, sc, sh, w_up, layer, conv_w, conv_b, st0, st1, tn_pref=512):
    M, D = x.shape
    F = w_up.shape[2] // 2
    tn = _tile(F, tn_pref)
    nj = F // tn
    lo = lambda j: (0, j)
    hi = lambda j: (0, nj + j)
    full = lambda a: pl.BlockSpec(a.shape, lambda j: (0,) * a.ndim)
    u = jax.ShapeDtypeStruct((M, F), F32)
    ospec = pl.BlockSpec((M, tn), lo)
    return _pallas(
        _ffn_up_step_kernel,
        out_shape=(jax.ShapeDtypeStruct((M, F), BF16), u, u),
        grid=(nj,),
        in_specs=[full(x), full(g), full(sc), full(sh),
                  pl.BlockSpec((None, D, tn), lambda j: (layer, 0, j)),
                  pl.BlockSpec((None, D, tn), lambda j: (layer, 0, nj + j)),
                  pl.BlockSpec((3, tn), lo), pl.BlockSpec((3, tn), hi),
                  pl.BlockSpec((1, tn), lo), pl.BlockSpec((1, tn), hi),
                  pl.BlockSpec((M, tn), lo), pl.BlockSpec((M, tn), hi),
                  pl.BlockSpec((M, tn), lo), pl.BlockSpec((M, tn), hi)],
        out_specs=(ospec, ospec, ospec),
        scratch_shapes=[pltpu.VMEM((M, D), BF16)],
        compiler_params=_cp("arbitrary"),
        name="ffn_up_step",
    )(x, g, sc, sh, w_up, w_up, conv_w, conv_w, conv_b, conv_b, st0, st0, st1, st1)


def _in_proj_columns():
    a_cols = 512 + 128 + QK_ROPE
    b0 = a_cols
    sizes = (RW_W, DECAY_LORA, RW_W, RW_W, ICLR_LORA, GATE_LORA)
    offs = np.concatenate([[0], np.cumsum(sizes)])[:-1] + b0
    r0, wd0, k0, v0, ad0, gd0 = [int(o) for o in offs]
    b_cols = int(sum(sizes))
    c0 = b0 + b_cols
    pad = lambda n: [-1] * n
    rng = lambda s, n: list(range(s, s + n))
    cols = (rng(r0, RW_W) + rng(k0, RW_W) + rng(v0, RW_W)
            + rng(wd0, DECAY_LORA) + pad(LANE - DECAY_LORA)
            + rng(ad0, ICLR_LORA) + pad(LANE - ICLR_LORA)
            + rng(gd0, GATE_LORA)
            + rng(0, 512)
            + rng(c0, 512)
            + rng(512, 128)
            + rng(640, QK_ROPE) + pad(LANE - QK_ROPE))
    cols = np.asarray(cols, np.int32)
    assert cols.shape[0] == P_COLS
    pb_src = cols[:PB_W] - b0
    return cols, pb_src, b_cols


def _gather_cols(w, cols):
    cols = [int(c) for c in cols]
    runs, i = [], 0
    while i < len(cols):
        j = i + 1
        if cols[i] < 0:
            while j < len(cols) and cols[j] < 0:
                j += 1
            runs.append(jnp.zeros(w.shape[:-1] + (j - i,), w.dtype))
        else:
            while j < len(cols) and cols[j] == cols[j - 1] + 1:
                j += 1
            runs.append(w[..., cols[i]:cols[i] + (j - i)])
        i = j
    return jnp.concatenate(runs, axis=-1)


def _uq_columns():
    cols = []
    for h in range(MLA_HEADS):
        base = h * (QK_NOPE + QK_ROPE)
        cols += list(range(base, base + QK_NOPE + QK_ROPE)) + [-1] * (QK_PAD - QK_NOPE - QK_ROPE)
    return np.asarray(cols, np.int32)


def _pad_lanes(a, n):
    return jnp.pad(a, [(0, 0)] * (a.ndim - 1) + [(0, n - a.shape[-1])])


def _pad_rows(a, n):
    return jnp.pad(a, [(0, 0)] * (a.ndim - 2) + [(0, n - a.shape[-2]), (0, 0)])


def _rope_tables(pos):
    half = QK_ROPE // 2
    inv = jnp.power(ROPE_THETA, -jnp.arange(half, dtype=F32) * 2.0 / QK_ROPE)
    ang = pos.astype(F32)[:, None] * inv[None, :]
    cos, sin = jnp.cos(ang), jnp.sin(ang)
    z = jnp.zeros((pos.shape[0], LANE - QK_ROPE), F32)
    return jnp.concatenate([cos, cos, z], axis=1), jnp.concatenate([sin, sin, z], axis=1)


def kernel(x_prompt, x_sample, cache_latent, cache_krope, state_wkv, state_shift, state_pool, state_conv, page_table, c_prompt, c_sample, ada_w, ada_b, norm_mix_g, norm_ffn_g, w_in, q_lora_g, w_uq, kv_lora_g, w_uk, w_uv, qn_g, qr_g, kn_g, kr_g, rw_mu, rw_w0, rw_w_up, rw_a0, rw_a_up, rw_g_up, rw_k_k, rw_k_a, rw_r_k, rw_lnx_g, rw_lnx_b, w_pool, pool_scale, w_out, ffn_w_up, ffn_conv_w, ffn_conv_b, ffn_w_down):
    B, T, D = x_prompt.shape
    DB, TS, _ = x_sample.shape
    assert TS == 1, "the sample group decodes one token per sequence"
    L = ada_w.shape[0]
    n_pages = page_table.shape[1]
    past = n_pages * PAGE
    H, N = RW_HEADS, RW_N
    F = ffn_w_down.shape[1]

    cols, pb_src, b_cols = _in_proj_columns()
    pb_valid = np.nonzero(pb_src >= 0)[0]
    pb_inverse = pb_valid[np.argsort(pb_src[pb_valid])]

    w_in_p = _gather_cols(w_in, cols).astype(BF16)
    w_uq_p = _gather_cols(w_uq, _uq_columns()).astype(BF16)
    w_uk_f = w_uk.reshape(L, w_uk.shape[1], MLA_HEADS * QK_NOPE)
    w_uk_b = w_uk_f.astype(BF16)
    w_uk_t = jnp.swapaxes(w_uk_b, 1, 2)
    cache_krope_t = jnp.swapaxes(cache_krope, 2, 3)
    w_uv_b = w_uv.reshape(L, w_uv.shape[1], MLA_HEADS * V_HEAD).astype(BF16)
    w_out_b = w_out.astype(BF16)
    w_up_b = ffn_w_up.astype(BF16)
    w_down_b = ffn_w_down.astype(BF16)
    w_pool_b = w_pool.astype(BF16)
    mu_p = _gather_cols(rw_mu, pb_src)
    rw_w_up_p = _pad_rows(rw_w_up, LANE).astype(BF16)
    rw_a_up_p = _pad_rows(rw_a_up, LANE).astype(BF16)
    rw_g_up_b = rw_g_up.astype(BF16)
    qr_g_p = _pad_lanes(qr_g, LANE)
    kr_g_p = _pad_lanes(kr_g, LANE)
    shift_p = _gather_cols(state_shift, pb_src)
    row2 = lambda a, l: a[l].reshape(1, -1)

    mc = DB + B
    c_all = _pad_rows(jnp.concatenate([c_sample, c_prompt], axis=0), mc + (-mc) % 8)
    mod = _ada_mod(c_all, ada_w, ada_b)

    cos_p, sin_p = _rope_tables(jnp.arange(T, dtype=jnp.int32))
    cos_s, sin_s = _rope_tables(jnp.full((DB,), past, jnp.int32))

    state_wkv_flat = state_wkv.reshape(L, DB, H * N * N)

    xp = x_prompt.reshape(B * T, D)
    xs = x_sample.reshape(DB, D)
    outs = [[] for _ in range(12)]

    for l in range(L):
        lw = dict(q_lora_g=row2(q_lora_g, l), w_uq=w_uq_p[l], kv_lora_g=row2(kv_lora_g, l), qn_g=row2(qn_g, l),
                  qr_g=qr_g_p[l].reshape(1, -1), kn_g=row2(kn_g, l), kr_g=kr_g_p[l].reshape(1, -1),
                  w_uk=w_uk_b[l], w_uv=w_uv_b[l],
                  rw_mu=mu_p[l].reshape(1, -1), rw_w0=row2(rw_w0, l), rw_w_up=rw_w_up_p[l], rw_a0=row2(rw_a0, l),
                  rw_a_up=rw_a_up_p[l], rw_g_up=rw_g_up_b[l], rw_k_k=row2(rw_k_k, l), rw_k_a=row2(rw_k_a, l),
                  rw_r_k=row2(rw_r_k, l), rw_lnx_g=row2(rw_lnx_g, l), rw_lnx_b=row2(rw_lnx_b, l))
        g_mix, g_ffn = row2(norm_mix_g, l), row2(norm_ffn_g, l)
        ps = row2(pool_scale, l)
        cw, cb = ffn_conv_w[l], row2(ffn_conv_b, l)
        mods_p = [mod[l, DB:DB + B, k * D:(k + 1) * D][:, None, :] for k in range(6)]
        mods_s = [mod[l, :DB, k * D:(k + 1) * D][None] for k in range(6)]

        sh1, sc1, g1, sh2, sc2, g2 = mods_p
        p = _norm_mod_matmul(xp, g_mix, sc1, sh1, w_in_p, l, T)
        q, k, v, lat, kr = _mla_proj(p, cos_p, sin_p, lw, T, BF16)
        ya = _attn_prompt(q, k, v, B, T)
        r_, wl_, k_, v_, kk_, b_, g_ = _rw_prep(p, None, lw, T)
        y_raw, st_t = _rw_chunked((r_, wl_, k_, v_, kk_, b_), B, T)
        yb = _rw_post(y_raw, r_, k_, v_, g_, lw)
        yc = _pool_prompt(p, w_pool_b[l], ps, T)
        xp = _matmul_gate_res([ya, yb, yc], w_out_b, l, g1, xp, T)
        act, tail_v, tail_g = _ffn_up_prompt(xp, g_ffn, sc2, sh2, w_up_b, l, cw, cb, B, T)
        xp = _matmul_gate_res([act], w_down_b, l, g2, xp, T)

        p3 = p.reshape(B, T, P_COLS)
        outs[0].append(lat.reshape(B, T, LANE))
        outs[2].append(kr.reshape(B, T, LANE)[:, :, :QK_ROPE])
        outs[4].append(jnp.swapaxes(st_t, -1, -2))
        outs[6].append(_gather_cols(p3[:, -1, :PB_W], pb_inverse))
        pool_rows = p3[:, max(T - POOL_BUF, 0):, PC_OFF:PC_OFF + 512]
        if T < POOL_BUF:
            pool_rows = jnp.concatenate([jnp.zeros((B, POOL_BUF - T, 512), F32), pool_rows], axis=1)
        outs[8].append(pool_rows)
        last = lambda t: t.reshape(B, -1, 8, F)[:, -1, 6:8]
        outs[10].append(jnp.concatenate([last(tail_v), last(tail_g)], axis=-1))

        sh1, sc1, g1, sh2, sc2, g2 = mods_s
        p = _norm_mod_matmul(xs, g_mix, sc1, sh1, w_in_p, l, DB)
        q, _, _, lat, kr = _mla_proj(p, cos_s, sin_s, lw, DB, F32)
        qt, qrp = _absorb_q(q, lw["kn_g"], w_uk_f[l])
        o_lat = _paged_attention(page_table, jnp.swapaxes(qt, 0, 1), jnp.swapaxes(qrp, 0, 1), lat, kr, w_uk_t[l],
                                 cache_latent, cache_krope_t, l)
        ya = _uv_out(jnp.swapaxes(o_lat, 0, 1), lw["w_uv"])
        r_, wl_, k_, v_, kk_, b_, g_ = _rw_prep(p, shift_p[l], lw, DB)
        heads = lambda a: jnp.swapaxes(a.reshape(DB, H, N), 0, 1)
        s_new, y_h = _rw_step(state_wkv_flat, l, [heads(a) for a in (r_, wl_, k_, v_, kk_, b_)])
        y_raw = jnp.swapaxes(y_h, 0, 1).reshape(DB, H * N)
        yb = _rw_post(y_raw, r_, k_, v_, g_, lw)
        yc = _pool_step(jnp.swapaxes(state_pool[l], 0, 1), p, w_pool_b[l], ps, past)
        xs = _matmul_gate_res([ya, yb, yc], w_out_b, l, g1, xs, DB)
        act, u_v, u_g = _ffn_up_step(xs, g_ffn, sc2, sh2, w_up_b, l, cw, cb, state_conv[l, :, 0], state_conv[l, :, 1])
        xs = _matmul_gate_res([act], w_down_b, l, g2, xs, DB)

        outs[1].append(lat.reshape(DB, 1, LANE))
        outs[3].append(kr.reshape(DB, 1, LANE)[:, :, :QK_ROPE])
        outs[5].append(s_new.reshape(DB, H, N, N))
        outs[7].append(_gather_cols(p[:, :PB_W], pb_inverse))
        outs[9].append(jnp.concatenate([state_pool[l][:, 1:], p[:, None, PC_OFF:PC_OFF + 512]], axis=1))
        outs[11].append(jnp.stack([state_conv[l, :, 1], jnp.concatenate([u_v, u_g], axis=-1)], axis=1))

    stacked = [jnp.stack(o) for o in outs]
    return (xp.reshape(B, T, D), xs.reshape(DB, 1, D), *stacked)
```

```python
import functools

import numpy as np
import jax
import jax.numpy as jnp
from jax import lax
from jax.experimental import pallas as pl
from jax.experimental.pallas import tpu as pltpu

F32 = jnp.float32
BF16 = jnp.bfloat16

MLA_HEADS = 8
QK_NOPE = 128
QK_ROPE = 64
V_HEAD = 128
ROPE_THETA = 10000.0
MLA_SCALE = (QK_NOPE + QK_ROPE) ** -0.5
RW_HEADS = 8
RW_N = 64
DECAY_LORA = 96
ICLR_LORA = 96
GATE_LORA = 256
GN_EPS = 64e-5
NORM_EPS = 1e-6
POOL_WINDOWS = (2, 4, 8, 16)
POOL_BUF = 15
PAGE = 128

LANE = 128
QK_PAD = 256
VMEM_LIMIT = 48 * 1024 * 1024

RW_W = RW_HEADS * RW_N
PB_W = 2048
CQ_OFF = PB_W
PC_OFF = CQ_OFF + 512
CKV_OFF = PC_OFF + 512
KR_OFF = CKV_OFF + 128
P_COLS = KR_OFF + 128

NN = (((1,), (0,)), ((), ()))
NT = (((1,), (1,)), ((), ()))
TN = (((0,), (0,)), ((), ()))


def _cp(*sem):
    return pltpu.CompilerParams(dimension_semantics=sem, vmem_limit_bytes=VMEM_LIMIT)


def _pallas(body, **kw):
    call = pl.pallas_call(body, **kw)

    def run(*args):
        if any(isinstance(a, jax.core.Tracer) for a in args):
            args = [a if a.dtype == jnp.int32 else pltpu.with_memory_space_constraint(a, pltpu.HBM) for a in args]
        return call(*args)
    return run


def _tile(n, pref):
    if n <= pref:
        return n
    t = pref - pref % 8
    while t >= 8:
        if n % t == 0:
            return t
        t -= 8
    return n


def _dot(a, b, dims=NN):
    return lax.dot_general(a, b, dims, preferred_element_type=F32)


def _split2(x):
    hi = x.astype(BF16)
    lo = (x - hi.astype(F32)).astype(BF16)
    return hi, lo


def _split3(x):
    hi = x.astype(BF16)
    r1 = x - hi.astype(F32)
    mid = r1.astype(BF16)
    lo = (r1 - mid.astype(F32)).astype(BF16)
    return hi, mid, lo


def _dot3(a, b, dims=NN):
    (ca,), (cb,) = dims[0]
    ah = a.astype(BF16).astype(F32)
    bh = b.astype(BF16).astype(F32)
    a3 = jnp.concatenate([ah, a - ah, ah], axis=ca).astype(BF16)
    b3 = jnp.concatenate([bh, bh, b - bh], axis=cb).astype(BF16)
    return _dot(a3, b3, dims)


def _dot_sel_l(sel, x, dims=NN):
    h, m, l = _split3(x)
    return _dot(sel, h, dims) + (_dot(sel, m, dims) + _dot(sel, l, dims))


def _dot_sel_r(x, sel, dims=NN):
    h, l = _split2(x)
    return _dot(h, sel, dims) + _dot(l, sel, dims)


def _sigmoid(x):
    return 1.0 / (1.0 + jnp.exp(-x))


def _rms(x, g, n):
    return (x * lax.rsqrt(jnp.sum(x * x, axis=-1, keepdims=True) / n + NORM_EPS)) * g


def _modnorm(x, g, sc, sh):
    return _rms(x, g, x.shape[-1]) * (1.0 + sc) + sh


def _mod_spec(tm, tn, tpg, rows, col_arg):
    if rows == 1:
        if col_arg:
            return pl.BlockSpec((1, 1, tn), lambda i, j: (i // tpg, 0, j))
        return pl.BlockSpec((1, 1, tn), lambda i, j: (i // tpg, 0, 0))
    if col_arg:
        return pl.BlockSpec((1, tm, tn), lambda i, j: (i // tpg, i % tpg, j))
    return pl.BlockSpec((1, tm, tn), lambda i, j: (i // tpg, i % tpg, 0))


def _ada_kernel(c_ref, w_ref, b_ref, o_ref):
    c = c_ref[...]
    s = c * _sigmoid(c)
    o_ref[0] = _dot(s.astype(BF16), w_ref[0].astype(BF16)) + b_ref[0]


def _ada_mod(c_all, ada_w, ada_b):
    L, D, N = ada_w.shape
    mc = c_all.shape[0]
    tn = _tile(N, 1024)
    return _pallas(
        _ada_kernel,
        out_shape=jax.ShapeDtypeStruct((L, mc, N), F32),
        grid=(L, N // tn),
        in_specs=[pl.BlockSpec((mc, D), lambda l, j: (0, 0)),
                  pl.BlockSpec((1, D, tn), lambda l, j: (l, 0, j)),
                  pl.BlockSpec((1, 1, tn), lambda l, j: (l, 0, j))],
        out_specs=pl.BlockSpec((1, mc, tn), lambda l, j: (l, 0, j)),
        compiler_params=_cp("parallel", "parallel"),
        name="ada_mod",
    )(c_all, ada_w, ada_b.reshape(L, 1, N))


def _nmm_kernel(x_ref, g_ref, sc_ref, sh_ref, w_ref, o_ref, h_ref):
    @pl.when(pl.program_id(1) == 0)
    def _():
        h_ref[...] = _modnorm(x_ref[...], g_ref[...], sc_ref[0], sh_ref[0]).astype(BF16)

    o_ref[...] = _dot(h_ref[...], w_ref[...])


def _norm_mod_matmul(x, g, sc, sh, w, layer, rpg, tm_pref=1024, tn_pref=256):
    M, D = x.shape
    N = w.shape[2]
    tm = _tile(rpg, tm_pref)
    tpg = rpg // tm
    tn = max(t for t in range(LANE, tn_pref + 1, LANE) if N % t == 0)
    rows = sc.shape[1]
    return _pallas(
        _nmm_kernel,
        out_shape=jax.ShapeDtypeStruct((M, N), F32),
        grid=(M // tm, N // tn),
        in_specs=[pl.BlockSpec((tm, D), lambda i, j: (i, 0)),
                  pl.BlockSpec((1, D), lambda i, j: (0, 0)),
                  _mod_spec(tm, D, tpg, rows, False),
                  _mod_spec(tm, D, tpg, rows, False),
                  pl.BlockSpec((None, D, tn), lambda i, j: (layer, 0, j))],
        out_specs=pl.BlockSpec((tm, tn), lambda i, j: (i, j)),
        scratch_shapes=[pltpu.VMEM((tm, D), BF16)],
        compiler_params=_cp("parallel", "arbitrary"),
        name="norm_mod_matmul",
    )(x, g, sc, sh, w)


def _mmres_kernel(*refs, widths):
    n = len(widths)
    a_refs, (w_ref, gate_ref, res_ref, o_ref) = refs[:n], refs[n:]
    acc, k0 = None, 0
    for a_ref, k in zip(a_refs, widths):
        part = _dot(a_ref[...], w_ref[k0:k0 + k, :])
        acc = part if acc is None else acc + part
        k0 += k
    o_ref[...] = res_ref[...] + gate_ref[0] * acc


def _matmul_gate_res(parts, w, layer, gate, res, rpg, tm_pref=1024, tn_pref=512):
    M = parts[0].shape[0]
    widths = tuple(a.shape[1] for a in parts)
    K, N = w.shape[1:]
    assert sum(widths) == K
    tm = _tile(rpg, tm_pref)
    tpg = rpg // tm
    tn = _tile(N, tn_pref)
    rows = gate.shape[1]
    return _pallas(
        functools.partial(_mmres_kernel, widths=widths),
        out_shape=jax.ShapeDtypeStruct((M, N), F32),
        grid=(M // tm, N // tn),
        in_specs=[pl.BlockSpec((tm, k), lambda i, j: (i, 0)) for k in widths]
        + [pl.BlockSpec((None, K, tn), lambda i, j: (layer, 0, j)),
           _mod_spec(tm, tn, tpg, rows, True),
           pl.BlockSpec((tm, tn), lambda i, j: (i, j))],
        out_specs=pl.BlockSpec((tm, tn), lambda i, j: (i, j)),
        compiler_params=_cp("parallel", "arbitrary"),
        name="matmul_gate_res",
    )(*parts, w, gate, res)


def _mla_proj_kernel(cq_ref, ckv_ref, kr_ref, cos_ref, sin_ref, qlg_ref, wuq_ref, kvg_ref, qng_ref, qrg_ref,
                     kng_ref, krg_ref, wuk_ref, wuv_ref, q_ref, k_ref, v_ref, lat_ref, kro_ref):
    cos = cos_ref[...]
    sin = sin_ref[...]
    lane = lax.broadcasted_iota(jnp.int32, cos.shape, 1)
    half = QK_ROPE // 2

    def rope(x):
        rot = jnp.where(lane < half, -pltpu.roll(x, LANE - half, 1), pltpu.roll(x, half, 1))
        return x * cos + rot * sin

    cqn = _rms(cq_ref[...], qlg_ref[...], cq_ref.shape[-1])
    q = _dot(cqn.astype(BF16), wuq_ref[...])
    lat = _rms(ckv_ref[...], kvg_ref[...], ckv_ref.shape[-1])
    lat_ref[...] = lat
    kr = rope(_rms(kr_ref[...], krg_ref[...], QK_ROPE))
    kro_ref[...] = kr
    latb = lat.astype(BF16)
    kx = _dot(latb, wuk_ref[...])
    v_ref[...] = _dot(latb, wuv_ref[...]).astype(v_ref.dtype)
    krb = kr.astype(k_ref.dtype)
    for h in range(MLA_HEADS):
        o = h * QK_PAD
        qn = _rms(q[:, o:o + QK_NOPE], qng_ref[...], QK_NOPE)
        qr = rope(_rms(q[:, o + QK_NOPE:o + QK_PAD], qrg_ref[...], QK_ROPE))
        q_ref[:, o:o + QK_NOPE] = qn.astype(q_ref.dtype)
        q_ref[:, o + QK_NOPE:o + QK_PAD] = qr.astype(q_ref.dtype)
        kn = _rms(kx[:, h * QK_NOPE:(h + 1) * QK_NOPE], kng_ref[...], QK_NOPE)
        k_ref[:, o:o + QK_NOPE] = kn.astype(k_ref.dtype)
        k_ref[:, o + QK_NOPE:o + QK_PAD] = krb


def _mla_proj(p, cos_t, sin_t, lw, rpg, q_dtype, tm_pref=256):
    M = p.shape[0]
    tm = _tile(rpg, tm_pref)
    tpg = rpg // tm
    H = MLA_HEADS
    row = lambda w: pl.BlockSpec((tm, w), lambda i: (i, 0))
    const = lambda a: pl.BlockSpec(a.shape, lambda i: (0,) * a.ndim)
    tab = pl.BlockSpec((tm, LANE), lambda i: (i % tpg, 0))
    weights = (lw["q_lora_g"], lw["w_uq"], lw["kv_lora_g"], lw["qn_g"], lw["qr_g"], lw["kn_g"], lw["kr_g"],
               lw["w_uk"], lw["w_uv"])
    return _pallas(
        _mla_proj_kernel,
        out_shape=(jax.ShapeDtypeStruct((M, H * QK_PAD), q_dtype),
                   jax.ShapeDtypeStruct((M, H * QK_PAD), BF16),
                   jax.ShapeDtypeStruct((M, H * V_HEAD), BF16),
                   jax.ShapeDtypeStruct((M, LANE), F32),
                   jax.ShapeDtypeStruct((M, LANE), F32)),
        grid=(M // tm,),
        in_specs=[pl.BlockSpec((tm, 512), lambda i: (i, CQ_OFF // 512)),
                  pl.BlockSpec((tm, LANE), lambda i: (i, CKV_OFF // LANE)),
                  pl.BlockSpec((tm, LANE), lambda i: (i, KR_OFF // LANE)),
                  tab, tab] + [const(a) for a in weights],
        out_specs=(row(H * QK_PAD), row(H * QK_PAD), row(H * V_HEAD), row(LANE), row(LANE)),
        compiler_params=_cp("parallel"),
        name="mla_proj",
    )(p, p, p, cos_t, sin_t, *weights)


def _attn_kernel(q_ref, k_ref, v_ref, o_ref, *, tq):
    qi = pl.program_id(2)
    q = q_ref[...]

    def block(kj, carry, causal):
        m, l, acc = carry
        start = pl.multiple_of(kj * tq, tq)
        k = k_ref[pl.ds(start, tq), :]
        v = v_ref[pl.ds(start, tq), :]
        s = _dot(q, k, NT) * MLA_SCALE
        if causal:
            row = lax.broadcasted_iota(jnp.int32, (tq, tq), 0)
            col = lax.broadcasted_iota(jnp.int32, (tq, tq), 1)
            s = jnp.where(col <= row, s, -jnp.inf)
        m_new = jnp.maximum(m, jnp.max(s, axis=-1, keepdims=True))
        alpha = jnp.exp(m - m_new)
        e = jnp.exp(s - m_new)
        l = alpha * l + jnp.sum(e, axis=-1, keepdims=True)
        acc = alpha * acc + _dot(e.astype(BF16), v)
        return m_new, l, acc

    init = (jnp.full((tq, 1), -jnp.inf, F32), jnp.zeros((tq, 1), F32), jnp.zeros((tq, V_HEAD), F32))
    carry = block(qi, init, True)
    m, l, acc = lax.fori_loop(0, qi, lambda kj, c: block(kj, c, False), carry)
    o_ref[...] = (acc / l).astype(o_ref.dtype)


def _attn_prompt(q, k, v, B, T, tq_pref=512):
    H = MLA_HEADS
    tq = _tile(T, tq_pref)
    nq = T // tq
    return _pallas(
        functools.partial(_attn_kernel, tq=tq),
        out_shape=jax.ShapeDtypeStruct((B * T, H * V_HEAD), BF16),
        grid=(B, H, nq),
        in_specs=[pl.BlockSpec((tq, QK_PAD), lambda b, h, i: (b * nq + i, h)),
                  pl.BlockSpec((T, QK_PAD), lambda b, h, i: (b, h)),
                  pl.BlockSpec((T, V_HEAD), lambda b, h, i: (b, h))],
        out_specs=pl.BlockSpec((tq, V_HEAD), lambda b, h, i: (b * nq + i, h)),
        compiler_params=_cp("parallel", "parallel", "arbitrary"),
        name="attn_prompt",
    )(q, k, v)


def _absorb_kernel(q_ref, kng_ref, wuk_ref, qt_ref, qr_ref):
    g = kng_ref[...]
    for h in range(MLA_HEADS):
        o = h * QK_PAD
        qn = q_ref[:, o:o + QK_NOPE] * g
        wh = wuk_ref[:, h * QK_NOPE:(h + 1) * QK_NOPE]
        qt_ref[h] = _dot3(qn, wh, NT)
        qr_ref[h] = q_ref[:, o + QK_NOPE:o + QK_PAD]


def _absorb_q(q, kn_g, w_uk):
    M = q.shape[0]
    H = MLA_HEADS
    return _pallas(
        _absorb_kernel,
        out_shape=(jax.ShapeDtypeStruct((H, M, LANE), F32), jax.ShapeDtypeStruct((H, M, LANE), F32)),
        compiler_params=pltpu.CompilerParams(vmem_limit_bytes=VMEM_LIMIT),
        name="absorb_q",
    )(q, kn_g, w_uk)


def _paged_kernel(pt_ref, qt_ref, qr_ref, latn_ref, krn_ref, wukt_ref, lat_hbm, kr_hbm, o_ref,
                  latbuf, krbuf, lhs_ref, s_ref, sem, *, layer, n_pages, chunk_pages):
    b = pl.program_id(0)
    nb = pl.num_programs(0)
    slot = b % 2
    H = MLA_HEADS
    HD = H * QK_NOPE
    n_chunks = n_pages // chunk_pages
    rows = chunk_pages * PAGE
    R = latbuf.shape[-1]

    def page_copies(seq, sl, c, j):
        p = c * chunk_pages + j
        pg = pt_ref[seq * n_pages + p]
        return (pltpu.make_async_copy(lat_hbm.at[layer, pg], latbuf.at[sl, p], sem.at[sl, 0, c]),
                pltpu.make_async_copy(kr_hbm.at[layer, pg],
                                      krbuf.at[sl, :, pl.ds(pl.multiple_of(p * PAGE, PAGE), PAGE)], sem.at[sl, 1, c]))

    def fetch_chunk(seq, sl, c):
        for j in range(chunk_pages):
            for cp in page_copies(seq, sl, c, j):
                cp.start()

    def wait_chunk(seq, sl, c):
        for j in range(chunk_pages):
            for cp in page_copies(seq, sl, c, j):
                cp.wait()

    @pl.when(b == 0)
    def _():
        def body(c, carry):
            fetch_chunk(0, 0, c)
            return carry
        lax.fori_loop(0, n_chunks, body, 0)

    lhs_ref[pl.ds(0, HD), :] = wukt_ref[...]
    lhs_ref[pl.ds(HD, 16), :] = jnp.concatenate([qt_ref[0], jnp.zeros_like(qt_ref[0])], axis=0).astype(BF16)
    lhs = lhs_ref[...]
    qr = qr_ref[0][:, :QK_ROPE].astype(BF16)

    def latent_scores(latb):
        xt = _dot(lhs, latb, NT)
        ms = []
        for h in range(H):
            acc = None
            for i in range(QK_NOPE // 8):
                blk = xt[h * QK_NOPE + 8 * i:h * QK_NOPE + 8 * i + 8]
                acc = blk * blk if acc is None else acc + blk * blk
            ms.append(jnp.sum(acc, axis=0, keepdims=True))
        ms = jnp.concatenate(ms, axis=0)
        return xt[HD:HD + H] * lax.rsqrt(ms / QK_NOPE + NORM_EPS)

    nxt = jnp.minimum(b + 1, nb - 1)

    def wait_body(c, carry):
        wait_chunk(b, slot, c)
        return carry
    lax.fori_loop(0, n_chunks, wait_body, 0)

    def score_chunk(c, carry):
        fetch_chunk(nxt, 1 - slot, c)

        p0 = pl.multiple_of(c * chunk_pages, chunk_pages)
        k0 = pl.multiple_of(c * rows, rows)
        s_rope = _dot(qr, krbuf[slot, :, pl.ds(k0, rows)].astype(BF16))
        latb = latbuf[slot, pl.ds(p0, chunk_pages)].reshape(rows, R).astype(BF16)
        s_ref[:, pl.ds(k0, rows)] = (latent_scores(latb) + s_rope) * MLA_SCALE
        return carry

    unroll = 4 if n_chunks % 4 == 0 else 1
    lax.fori_loop(0, n_chunks, score_chunk, 0, unroll=unroll)

    latn = latn_ref[0]
    latnb = jnp.concatenate([latn, jnp.zeros((PAGE - latn.shape[0], R), F32)], axis=0).astype(BF16)
    krn = krn_ref[0][0:1, :QK_ROPE].astype(BF16).astype(F32)
    s_rope = jnp.sum(qr.astype(F32) * krn, axis=-1, keepdims=True)
    s_new = (latent_scores(latnb) + s_rope) * MLA_SCALE
    s_new = jnp.where(lax.broadcasted_iota(jnp.int32, s_new.shape, 1) == 0, s_new, -jnp.inf)

    s_all = s_ref[...]
    m = jnp.maximum(jnp.max(s_all, axis=-1, keepdims=True), jnp.max(s_new, axis=-1, keepdims=True))
    e_new = jnp.exp(s_new - m)
    s_ref[...] = jnp.exp(s_all - m)
    l = jnp.sum(s_ref[...], axis=-1, keepdims=True) + jnp.sum(e_new, axis=-1, keepdims=True)

    def value_chunk(c, acc):
        p0 = pl.multiple_of(c * chunk_pages, chunk_pages)
        k0 = pl.multiple_of(c * rows, rows)
        latb = latbuf[slot, pl.ds(p0, chunk_pages)].reshape(rows, R).astype(BF16)
        return acc + _dot(s_ref[:, pl.ds(k0, rows)].astype(BF16), latb)

    acc = lax.fori_loop(0, n_chunks, value_chunk, _dot(e_new.astype(BF16), latnb), unroll=unroll)
    o_ref[0] = acc / l

    @pl.when(b == nb - 1)
    def _():
        def body(c, carry):
            wait_chunk(nxt, 1 - slot, c)
            return carry
        lax.fori_loop(0, n_chunks, body, 0)


def _paged_attention(page_table, qt, qr, lat_new, kr_new, w_uk_t, cache_latent, cache_krope_t, layer):
    DB, n_pages = page_table.shape
    H = MLA_HEADS
    R = cache_latent.shape[-1]
    chunk_pages = 8 if n_pages % 8 == 0 else 1
    pad8 = lambda a: jnp.pad(a[:, None, :], ((0, 0), (0, 7), (0, 0)))
    seq = lambda w: pl.BlockSpec((1, 8, w), lambda b, pt: (b, 0, 0))
    grid_spec = pltpu.PrefetchScalarGridSpec(
        num_scalar_prefetch=1,
        grid=(DB,),
        in_specs=[seq(LANE), seq(LANE), seq(LANE), seq(LANE),
                  pl.BlockSpec(w_uk_t.shape, lambda b, pt: (0, 0)),
                  pl.BlockSpec(memory_space=pl.ANY),
                  pl.BlockSpec(memory_space=pl.ANY)],
        out_specs=pl.BlockSpec((1, H, R), lambda b, pt: (b, 0, 0)),
        scratch_shapes=[pltpu.VMEM((2, n_pages, PAGE, R), F32),
                        pltpu.VMEM((2, QK_ROPE, n_pages * PAGE), F32),
                        pltpu.VMEM((w_uk_t.shape[0] + 16, R), BF16),
                        pltpu.VMEM((H, n_pages * PAGE), F32),
                        pltpu.SemaphoreType.DMA((2, 2, n_pages // chunk_pages))],
    )
    return _pallas(
        functools.partial(_paged_kernel, layer=layer, n_pages=n_pages, chunk_pages=chunk_pages),
        out_shape=jax.ShapeDtypeStruct((DB, H, R), F32),
        grid_spec=grid_spec,
        compiler_params=_cp("arbitrary"),
        name="paged_attention",
    )(page_table.reshape(-1), qt, qr, pad8(lat_new), pad8(kr_new), w_uk_t, cache_latent, cache_krope_t)


def _uv_kernel(o_ref, wuv_ref, y_ref):
    for h in range(MLA_HEADS):
        y_ref[:, h * V_HEAD:(h + 1) * V_HEAD] = _dot(
            o_ref[h].astype(BF16), wuv_ref[:, h * V_HEAD:(h + 1) * V_HEAD]).astype(y_ref.dtype)


def _uv_out(o_lat, w_uv):
    H, M, _ = o_lat.shape
    return _pallas(
        _uv_kernel,
        out_shape=jax.ShapeDtypeStruct((M, H * V_HEAD), BF16),
        compiler_params=pltpu.CompilerParams(vmem_limit_bytes=VMEM_LIMIT),
        name="uv_out",
    )(o_lat, w_uv)


def _rw_prep_body(pb, prev, mu_ref, w0_ref, wup_ref, a0_ref, aup_ref, gup_ref, kk_ref, ka_ref, seg_ref,
                  r_o, wl_o, k_o, v_o, kk_o, b_o, g_o):
    W = RW_W
    xs = pb + (prev - pb) * mu_ref[...]
    r = xs[:, 0:W]
    k = xs[:, W:2 * W]
    v = xs[:, 2 * W:3 * W]
    wd = xs[:, 3 * W:3 * W + LANE]
    ad = xs[:, 3 * W + LANE:3 * W + 2 * LANE]
    gd = xs[:, 3 * W + 2 * LANE:]
    z = -(w0_ref[...] + _dot(jnp.tanh(wd).astype(BF16), wup_ref[...]))
    softplus = jnp.maximum(z, 0.0) + jnp.log1p(jnp.exp(-jnp.abs(z)))
    w = -softplus - 0.5
    a = _sigmoid(a0_ref[...] + _dot(ad.astype(BF16), aup_ref[...]))
    g = _dot(_sigmoid(gd).astype(BF16), gup_ref[...])
    kk = k * kk_ref[...]
    k2 = k * (1.0 + (a - 1.0) * ka_ref[...])
    n2 = _dot_sel_r(kk * kk, seg_ref[...])
    kkn = kk / jnp.maximum(jnp.sqrt(n2), 1e-12)
    r_o[...] = r
    wl_o[...] = -jnp.exp(w)
    k_o[...] = k2
    v_o[...] = v
    kk_o[...] = kkn
    b_o[...] = kkn * a
    g_o[...] = g


def _rw_prep_seq_kernel(pb_ref, halo_ref, *rest, tpg):
    pb = pb_ref[...]
    i = pl.program_id(0)
    rolled = pltpu.roll(pb, 1, 0)
    row = lax.broadcasted_iota(jnp.int32, pb.shape, 0)
    first_row = jnp.where(i % tpg == 0, 0, -1)
    prev = jnp.where(row == 0, halo_ref[7:8, :], rolled)
    prev = jnp.where(row == first_row, 0.0, prev)
    _rw_prep_body(pb, prev, *rest)


def _rw_prep_step_kernel(pb_ref, prev_ref, *rest):
    _rw_prep_body(pb_ref[...], prev_ref[...], *rest)


def _seg_matrix(n, w):
    return jnp.asarray(np.kron(np.eye(n // w, dtype=np.float32), np.ones((w, w), np.float32)), BF16)


def _rw_prep(p, prev, lw, rpg, tm_pref=256):
    M = p.shape[0]
    tm = _tile(rpg, tm_pref)
    tpg = rpg // tm
    seg = _seg_matrix(RW_W, RW_N)
    weights = (lw["rw_mu"], lw["rw_w0"], lw["rw_w_up"], lw["rw_a0"], lw["rw_a_up"], lw["rw_g_up"], lw["rw_k_k"],
               lw["rw_k_a"], seg)
    const = lambda a: pl.BlockSpec(a.shape, lambda i: (0,) * a.ndim)
    out = jax.ShapeDtypeStruct((M, RW_W), F32)
    ospec = pl.BlockSpec((tm, RW_W), lambda i: (i, 0))
    main = pl.BlockSpec((tm, PB_W), lambda i: (i, 0))
    if prev is None:
        kern = functools.partial(_rw_prep_seq_kernel, tpg=tpg)
        second = pl.BlockSpec((8, PB_W), lambda i: (jnp.maximum(i * (tm // 8) - 1, 0), 0))
        second_arg = p
    else:
        kern = _rw_prep_step_kernel
        second = pl.BlockSpec((tm, PB_W), lambda i: (i, 0))
        second_arg = prev
    return _pallas(
        kern,
        out_shape=(out,) * 7,
        grid=(M // tm,),
        in_specs=[main, second] + [const(a) for a in weights],
        out_specs=(ospec,) * 7,
        compiler_params=_cp("parallel"),
        name="rwkv_prep",
    )(p, second_arg, *weights)


def _rw_chunk_kernel(r_ref, wl_ref, k_ref, v_ref, kk_ref, b_ref, y_ref, s_ref, st_ref, *, C, G):
    step = pl.program_id(1)

    @pl.when(step == 0)
    def _():
        st_ref[...] = jnp.zeros_like(st_ref)

    N = RW_N
    ri = lax.broadcasted_iota(jnp.int32, (C, C), 0)
    ci = lax.broadcasted_iota(jnp.int32, (C, C), 1)
    incl = ci <= ri
    strict = ci < ri
    eye_c = (ci == ri).astype(F32)
    rn = lax.broadcasted_iota(jnp.int32, (N, N), 0)
    cn = lax.broadcasted_iota(jnp.int32, (N, N), 1)
    eye_n = rn == cn
    rg = lax.broadcasted_iota(jnp.int32, (G * C, G * C), 0)
    cg = lax.broadcasted_iota(jnp.int32, (G * C, G * C), 1)
    same_chunk_incl = (cg <= rg) & (rg // C == cg // C)

    wl = wl_ref[...]
    cs = _dot_sel_l(same_chunk_incl.astype(BF16), wl)
    p_in = jnp.exp(cs)
    p_ex = jnp.exp(cs - wl)
    p_inv = jnp.exp(-cs)
    at = -kk_ref[...] * p_ex
    rp = r_ref[...] * p_in
    bp = b_ref[...] * p_inv
    kp = k_ref[...] * p_inv
    v = v_ref[...]

    items = [(g, h) for g in range(G) for h in range(RW_HEADS)]
    cut = lambda x: [x[g * C:(g + 1) * C, h * N:(h + 1) * N] for g, h in items]
    p_end = [p_in[(g + 1) * C - 1:(g + 1) * C, h * N:(h + 1) * N] for g, h in items]
    at_i, rp_i, bp_i, kp_i, v_i = (cut(x) for x in (at, rp, bp, kp, v))
    n = range(len(items))
    lhs = [jnp.concatenate([at_i[i], rp_i[i]], axis=0) for i in n]
    gb = [_dot3(lhs[i], bp_i[i], NT) for i in n]
    gk = [_dot3(lhs[i], kp_i[i], NT) for i in n]
    a_ab = [jnp.where(strict, gb[i][:C], 0.0) for i in n]
    a_rb = [jnp.where(incl, gb[i][C:], 0.0) for i in n]
    a_ak = [jnp.where(strict, gk[i][:C], 0.0) for i in n]
    a_rk = [jnp.where(incl, gk[i][C:], 0.0) for i in n]
    akv = [_dot3(a_ak[i], v_i[i]) for i in n]
    rkv = [_dot3(a_rk[i], v_i[i]) for i in n]
    kv = [_dot3(kp_i[i] * p_end[i], v_i[i], TN) for i in n]
    t = None
    m = 1
    while m < C:
        sh = m.bit_length() - 1
        low_left = ((ri >> (sh + 1)) == (ci >> (sh + 1))) & (((ri >> sh) & 1) == 1) & (((ci >> sh) & 1) == 0)
        a_off = [jnp.where(low_left, a_ab[i], 0.0) for i in n]
        if m == 1:
            t = [eye_c + a_off[i] for i in n]
        else:
            inner = [_dot3(a_off[i], t[i]) for i in n]
            t = [t[i] + _dot3(t[i], inner[i]) for i in n]
        m *= 2
    w = [_dot3(t[i], jnp.concatenate([at_i[i], akv[i]], axis=1)) for i in n]
    gg = [_dot3(a_rb[i], w[i]) for i in n]
    mm = [_dot3(bp_i[i] * p_end[i], w[i], TN) for i in n]
    top = [jnp.concatenate([rp_i[i] + gg[i][:, :N], jnp.where(eye_n, p_end[i], 0.0) + mm[i][:, :N]], axis=0)
           for i in n]
    st = [st_ref[h] for h in range(RW_HEADS)]
    for g in range(G):
        idx = [g * RW_HEADS + h for h in range(RW_HEADS)]
        res = [_dot3(top[i], st[h]) for h, i in enumerate(idx)]
        st = [res[h][C:] + (mm[i][:, N:] + kv[i]) for h, i in enumerate(idx)]
        y_ref[pl.ds(g * C, C), :] = jnp.concatenate(
            [res[h][:C] + (gg[i][:, N:] + rkv[i]) for h, i in enumerate(idx)], axis=1)
    for h in range(RW_HEADS):
        st_ref[h] = st[h]

    @pl.when(step == pl.num_programs(1) - 1)
    def _():
        s_ref[0] = st_ref[...]


def _rw_chunked(ops, B, T):
    C = _tile(T, 64)
    nc = T // C
    G = 4 if nc % 4 == 0 else (2 if nc % 2 == 0 else 1)
    ns = nc // G
    blk = pl.BlockSpec((G * C, RW_W), lambda b, s: (b * ns + s, 0))
    return _pallas(
        functools.partial(_rw_chunk_kernel, C=C, G=G),
        out_shape=(jax.ShapeDtypeStruct((B * T, RW_W), F32),
                   jax.ShapeDtypeStruct((B, RW_HEADS, RW_N, RW_N), F32)),
        grid=(B, ns),
        in_specs=[blk] * 6,
        out_specs=(blk, pl.BlockSpec((1, RW_HEADS, RW_N, RW_N), lambda b, s: (b, 0, 0, 0))),
        scratch_shapes=[pltpu.VMEM((RW_HEADS, RW_N, RW_N), F32)],
        compiler_params=_cp("parallel", "arbitrary"),
        name="rwkv_chunked",
    )(*ops)


def _rw_step_kernel(s_ref, r_ref, wl_ref, k_ref, v_ref, kk_ref, b_ref, ej_ref, ei_ref, rd_ref, so_ref, y_ref):
    s = s_ref[0]
    ej = ej_ref[...]
    ei = ei_ref[...]
    rd = rd_ref[...]
    over_j = lambda x: _dot_sel_r(x, ej)
    sa = _dot_sel_r(s * over_j(kk_ref[0]), rd)
    s_new = (s * over_j(jnp.exp(wl_ref[0])) - _dot_sel_r(sa, ei) * over_j(b_ref[0])
             + _dot_sel_r(v_ref[0], ei) * over_j(k_ref[0]))
    so_ref[0] = s_new
    y_ref[0] = _dot_sel_r(s_new * over_j(r_ref[0]), rd)


def _rw_step(state_wkv, layer, vecs):
    _, DB, _ = state_wkv.shape
    H, N = RW_HEADS, RW_N
    bt = _tile(DB, 128)
    ej = jnp.asarray(np.kron(np.ones((1, N), np.float32), np.eye(N, dtype=np.float32)), BF16)
    ei = jnp.asarray(np.kron(np.eye(N, dtype=np.float32), np.ones((1, N), np.float32)), BF16)
    rd = ei.T
    sblk = pl.BlockSpec((1, bt, N * N), lambda i, h: (layer, i, h))
    oblk = pl.BlockSpec((1, bt, N * N), lambda i, h: (0, i, h))
    vblk = pl.BlockSpec((1, bt, N), lambda i, h: (h, i, 0))
    const = lambda a: pl.BlockSpec(a.shape, lambda i, h: (0, 0))
    s_new, y = _pallas(
        _rw_step_kernel,
        out_shape=(jax.ShapeDtypeStruct((1, DB, H * N * N), F32), jax.ShapeDtypeStruct((H, DB, N), F32)),
        grid=(DB // bt, H),
        in_specs=[sblk] + [vblk] * 6 + [const(ej), const(ei), const(rd)],
        out_specs=(oblk, vblk),
        compiler_params=_cp("parallel", "parallel"),
        name="rwkv_step",
    )(state_wkv, *vecs, ej, ei, rd)
    return s_new[0], y


def _rw_post_kernel(y_ref, r_ref, k_ref, v_ref, g_ref, lng_ref, lnb_ref, rk_ref, seg_ref, o_ref):
    seg = seg_ref[...]
    y = y_ref[...]
    mu = _dot_sel_r(y, seg) / RW_N
    d = y - mu
    var = _dot_sel_r(d * d, seg) / RW_N
    yn = (d * lax.rsqrt(var + GN_EPS)) * lng_ref[...] + lnb_ref[...]
    bonus = _dot_sel_r(r_ref[...] * k_ref[...] * rk_ref[...], seg) * v_ref[...]
    o_ref[...] = ((yn + bonus) * g_ref[...]).astype(o_ref.dtype)


def _rw_post(y, r, k, v, g, lw, tm_pref=256):
    M = y.shape[0]
    tm = _tile(M, tm_pref)
    seg = _seg_matrix(RW_W, RW_N)
    row = pl.BlockSpec((tm, RW_W), lambda i: (i, 0))
    const = lambda a: pl.BlockSpec(a.shape, lambda i: (0,) * a.ndim)
    weights = (lw["rw_lnx_g"], lw["rw_lnx_b"], lw["rw_r_k"], seg)
    return _pallas(
        _rw_post_kernel,
        out_shape=jax.ShapeDtypeStruct((M, RW_W), BF16),
        grid=(M // tm,),
        in_specs=[row] * 5 + [const(a) for a in weights],
        out_specs=row,
        compiler_params=_cp("parallel"),
        name="rwkv_post",
    )(y, r, k, v, g, *weights)


def _pool_windows(x_ext, halo_rows, count_of, wp_ref, ps_ref, o_ref):
    G = len(POOL_WINDOWS)
    gw = x_ext.shape[1] // G
    for gi, win in enumerate(POOL_WINDOWS):
        xg = x_ext[:, gi * gw:(gi + 1) * gw]
        s = xg
        step = 1
        while step < win:
            s = s + pltpu.roll(s, step, 0)
            step *= 2
        diff = s[halo_rows:] / count_of(win) - xg[halo_rows:]
        o = _dot(diff.astype(BF16), wp_ref[gi]) * ps_ref[:, gi * gw:(gi + 1) * gw]
        o_ref[:, gi * gw:(gi + 1) * gw] = o.astype(o_ref.dtype)


def _pool_seq_kernel(pc_ref, halo_ref, wp_ref, ps_ref, o_ref, *, tm, tpg):
    i = pl.program_id(0)
    t0 = (i % tpg) * tm
    hrow = lax.broadcasted_iota(jnp.int32, (halo_ref.shape[0], 1), 0)
    halo = jnp.where(hrow < jnp.where(t0 == 0, halo_ref.shape[0], 0), 0.0, halo_ref[...])
    x_ext = jnp.concatenate([halo, pc_ref[...]], axis=0)
    pos = t0 + lax.broadcasted_iota(jnp.int32, (tm, 1), 0)
    count_of = lambda win: jnp.minimum(pos + 1, win).astype(F32)
    _pool_windows(x_ext, halo_ref.shape[0], count_of, wp_ref, ps_ref, o_ref)


def _pool_prompt(p, w_pool, pool_scale, rpg, tm_pref=256):
    M = p.shape[0]
    W = 512
    HALO = 16
    tm = _tile(rpg, tm_pref)
    tpg = rpg // tm
    return _pallas(
        functools.partial(_pool_seq_kernel, tm=tm, tpg=tpg),
        out_shape=jax.ShapeDtypeStruct((M, W), BF16),
        grid=(M // tm,),
        in_specs=[pl.BlockSpec((tm, W), lambda i: (i, PC_OFF // W)),
                  pl.BlockSpec((HALO, W), lambda i: (jnp.maximum(i * (tm // HALO) - 1, 0), PC_OFF // W)),
                  pl.BlockSpec(w_pool.shape, lambda i: (0, 0, 0)),
                  pl.BlockSpec(pool_scale.shape, lambda i: (0, 0))],
        out_specs=pl.BlockSpec((tm, W), lambda i: (i, 0)),
        compiler_params=_cp("parallel"),
        name="pool_prompt",
    )(p, p, w_pool, pool_scale)


def _pool_step_kernel(st_ref, pc_ref, wp_ref, ps_ref, o_ref, *, past):
    G = len(POOL_WINDOWS)
    pc = pc_ref[...]
    gw = pc.shape[1] // G
    nbuf = st_ref.shape[0]
    for gi, win in enumerate(POOL_WINDOWS):
        sl = slice(gi * gw, (gi + 1) * gw)
        xg = pc[:, sl]
        total = xg
        for j in range(1, win):
            total = total + st_ref[nbuf - j][:, sl]
        diff = total / float(min(past + 1, win)) - xg
        o = _dot(diff.astype(BF16), wp_ref[gi]) * ps_ref[:, sl]
        o_ref[:, sl] = o.astype(o_ref.dtype)


def _pool_step(state_t, p, w_pool, pool_scale, past):
    nbuf, DB, W = state_t.shape
    return _pallas(
        functools.partial(_pool_step_kernel, past=past),
        out_shape=jax.ShapeDtypeStruct((DB, W), BF16),
        grid=(1,),
        in_specs=[pl.BlockSpec((nbuf, DB, W), lambda i: (0, 0, 0)),
                  pl.BlockSpec((DB, W), lambda i: (0, PC_OFF // W)),
                  pl.BlockSpec(w_pool.shape, lambda i: (0, 0, 0)),
                  pl.BlockSpec(pool_scale.shape, lambda i: (0, 0))],
        out_specs=pl.BlockSpec((DB, W), lambda i: (0, 0)),
        compiler_params=_cp("arbitrary"),
        name="pool_step",
    )(state_t, p, w_pool, pool_scale)


FFN_HALO = 16
FFN_SUB = 256


def _ffn_up_seq_kernel(x_ref, xh_ref, g_ref, sc_ref, sh_ref, wv_ref, wg_ref, cwv_ref, cwg_ref, cbv_ref, cbg_ref,
                       act_ref, tv_ref, tg_ref, h_ref, *, tm, tpg):
    i = pl.program_id(0)

    @pl.when(pl.program_id(1) == 0)
    def _():
        g, sc, sh = g_ref[...], sc_ref[0], sh_ref[0]
        h_ref[pl.ds(0, FFN_HALO), :] = _modnorm(xh_ref[...], g, sc, sh).astype(BF16)
        h_ref[pl.ds(FFN_HALO, tm), :] = _modnorm(x_ref[...], g, sc, sh).astype(BF16)

    h = h_ref[...]
    row = lax.broadcasted_iota(jnp.int32, (tm + FFN_HALO, 1), 0)
    keep = row >= jnp.where(i % tpg == 0, FFN_HALO, 0)

    def branch(w_ref, cw_ref, cb_ref, t_ref, cs):
        u = jnp.where(keep, _dot(h, w_ref[:, cs]), 0.0)
        t_ref[0, :, cs] = u[tm + FFN_HALO - 8:, :]
        cw = cw_ref[:, cs]
        return (cb_ref[:, cs] + pltpu.roll(u, 2, 0)[FFN_HALO:] * cw[0:1] + pltpu.roll(u, 1, 0)[FFN_HALO:] * cw[1:2]
                + u[FFN_HALO:] * cw[2:3])

    tn = act_ref.shape[1]
    for c0 in range(0, tn, FFN_SUB):
        cs = slice(c0, min(c0 + FFN_SUB, tn))
        val = branch(wv_ref, cwv_ref, cbv_ref, tv_ref, cs)
        gate = branch(wg_ref, cwg_ref, cbg_ref, tg_ref, cs)
        act_ref[:, cs] = ((gate * _sigmoid(gate)) * val).astype(act_ref.dtype)


def _ffn_up_prompt(x, g, sc, sh, w_up, layer, conv_w, conv_b, B, T, tm_pref=1024, tn_pref=704):
    M, D = x.shape
    F = w_up.shape[2] // 2
    tm = _tile(T, tm_pref)
    tpg = T // tm
    tn = max(t for t in range(LANE, tn_pref + 1, LANE) if F % t == 0)
    nj = F // tn
    lo = lambda i, j: (0, j)
    hi = lambda i, j: (0, nj + j)
    tail = jax.ShapeDtypeStruct((M // tm, 8, F), F32)
    tspec = pl.BlockSpec((1, 8, tn), lambda i, j: (i, 0, j))
    return _pallas(
        functools.partial(_ffn_up_seq_kernel, tm=tm, tpg=tpg),
        out_shape=(jax.ShapeDtypeStruct((M, F), BF16), tail, tail),
        grid=(M // tm, nj),
        in_specs=[pl.BlockSpec((tm, D), lambda i, j: (i, 0)),
                  pl.BlockSpec((FFN_HALO, D), lambda i, j: (jnp.maximum(i * (tm // FFN_HALO) - 1, 0), 0)),
                  pl.BlockSpec((1, D), lambda i, j: (0, 0)),
                  _mod_spec(tm, D, tpg, 1, False),
                  _mod_spec(tm, D, tpg, 1, False),
                  pl.BlockSpec((None, D, tn), lambda i, j: (layer, 0, j)),
                  pl.BlockSpec((None, D, tn), lambda i, j: (layer, 0, nj + j)),
                  pl.BlockSpec((3, tn), lo), pl.BlockSpec((3, tn), hi),
                  pl.BlockSpec((1, tn), lo), pl.BlockSpec((1, tn), hi)],
        out_specs=(pl.BlockSpec((tm, tn), lambda i, j: (i, j)), tspec, tspec),
        scratch_shapes=[pltpu.VMEM((tm + FFN_HALO, D), BF16)],
        compiler_params=_cp("arbitrary", "arbitrary"),
        name="ffn_up_prompt",
    )(x, x, g, sc, sh, w_up, w_up, conv_w, conv_w, conv_b, conv_b)


def _ffn_up_step_kernel(x_ref, g_ref, sc_ref, sh_ref, wv_ref, wg_ref, cwv_ref, cwg_ref, cbv_ref, cbg_ref,
                        s0v_ref, s0g_ref, s1v_ref, s1g_ref, act_ref, uv_ref, ug_ref, h_ref):
    @pl.when(pl.program_id(0) == 0)
    def _():
        h_ref[...] = _modnorm(x_ref[...], g_ref[...], sc_ref[0], sh_ref[0]).astype(BF16)

    h = h_ref[...]

    def branch(w_ref, cw_ref, cb_ref, s0_ref, s1_ref, u_ref):
        u = _dot(h, w_ref[...])
        u_ref[...] = u
        cw = cw_ref[...]
        return cb_ref[...] + s0_ref[...] * cw[0:1] + s1_ref[...] * cw[1:2] + u * cw[2:3]

    val = branch(wv_ref, cwv_ref, cbv_ref, s0v_ref, s1v_ref, uv_ref)
    gate = branch(wg_ref, cwg_ref, cbg_ref, s0g_ref, s1g_ref, ug_ref)
    act_ref[...] = ((gate * _sigmoid(gate)) * val).astype(act_ref.dtype)


def _ffn_up_step(x, g, sc, sh, w_up, layer, conv_w, conv_b, st0, st1, tn_pref=512):
    M, D = x.shape
    F = w_up.shape[2] // 2
    tn = _tile(F, tn_pref)
    nj = F // tn
    lo = lambda j: (0, j)
    hi = lambda j: (0, nj + j)
    full = lambda a: pl.BlockSpec(a.shape, lambda j: (0,) * a.ndim)
    u = jax.ShapeDtypeStruct((M, F), F32)
    ospec = pl.BlockSpec((M, tn), lo)
    return _pallas(
        _ffn_up_step_kernel,
        out_shape=(jax.ShapeDtypeStruct((M, F), BF16), u, u),
        grid=(nj,),
        in_specs=[full(x), full(g), full(sc), full(sh),
                  pl.BlockSpec((None, D, tn), lambda j: (layer, 0, j)),
                  pl.BlockSpec((None, D, tn), lambda j: (layer, 0, nj + j)),
                  pl.BlockSpec((3, tn), lo), pl.BlockSpec((3, tn), hi),
                  pl.BlockSpec((1, tn), lo), pl.BlockSpec((1, tn), hi),
                  pl.BlockSpec((M, tn), lo), pl.BlockSpec((M, tn), hi),
                  pl.BlockSpec((M, tn), lo), pl.BlockSpec((M, tn), hi)],
        out_specs=(ospec, ospec, ospec),
        scratch_shapes=[pltpu.VMEM((M, D), BF16)],
        compiler_params=_cp("arbitrary"),
        name="ffn_up_step",
    )(x, g, sc, sh, w_up, w_up, conv_w, conv_w, conv_b, conv_b, st0, st0, st1, st1)


def _in_proj_columns():
    a_cols = 512 + 128 + QK_ROPE
    b0 = a_cols
    sizes = (RW_W, DECAY_LORA, RW_W, RW_W, ICLR_LORA, GATE_LORA)
    offs = np.concatenate([[0], np.cumsum(sizes)])[:-1] + b0
    r0, wd0, k0, v0, ad0, gd0 = [int(o) for o in offs]
    b_cols = int(sum(sizes))
    c0 = b0 + b_cols
    pad = lambda n: [-1] * n
    rng = lambda s, n: list(range(s, s + n))
    cols = (rng(r0, RW_W) + rng(k0, RW_W) + rng(v0, RW_W)
            + rng(wd0, DECAY_LORA) + pad(LANE - DECAY_LORA)
            + rng(ad0, ICLR_LORA) + pad(LANE - ICLR_LORA)
            + rng(gd0, GATE_LORA)
            + rng(0, 512)
            + rng(c0, 512)
            + rng(512, 128)
            + rng(640, QK_ROPE) + pad(LANE - QK_ROPE))
    cols = np.asarray(cols, np.int32)
    assert cols.shape[0] == P_COLS
    pb_src = cols[:PB_W] - b0
    return cols, pb_src, b_cols


def _gather_cols(w, cols):
    cols = [int(c) for c in cols]
    runs, i = [], 0
    while i < len(cols):
        j = i + 1
        if cols[i] < 0:
            while j < len(cols) and cols[j] < 0:
                j += 1
            runs.append(jnp.zeros(w.shape[:-1] + (j - i,), w.dtype))
        else:
            while j < len(cols) and cols[j] == cols[j - 1] + 1:
                j += 1
            runs.append(w[..., cols[i]:cols[i] + (j - i)])
        i = j
    return jnp.concatenate(runs, axis=-1)


def _uq_columns():
    cols = []
    for h in range(MLA_HEADS):
        base = h * (QK_NOPE + QK_ROPE)
        cols += list(range(base, base + QK_NOPE + QK_ROPE)) + [-1] * (QK_PAD - QK_NOPE - QK_ROPE)
    return np.asarray(cols, np.int32)


def _pad_lanes(a, n):
    return jnp.pad(a, [(0, 0)] * (a.ndim - 1) + [(0, n - a.shape[-1])])


def _pad_rows(a, n):
    return jnp.pad(a, [(0, 0)] * (a.ndim - 2) + [(0, n - a.shape[-2]), (0, 0)])


def _rope_tables(pos):
    half = QK_ROPE // 2
    inv = jnp.power(ROPE_THETA, -jnp.arange(half, dtype=F32) * 2.0 / QK_ROPE)
    ang = pos.astype(F32)[:, None] * inv[None, :]
    cos, sin = jnp.cos(ang), jnp.sin(ang)
    z = jnp.zeros((pos.shape[0], LANE - QK_ROPE), F32)
    return jnp.concatenate([cos, cos, z], axis=1), jnp.concatenate([sin, sin, z], axis=1)


def kernel(x_prompt, x_sample, cache_latent, cache_krope, state_wkv, state_shift, state_pool, state_conv, page_table, c_prompt, c_sample, ada_w, ada_b, norm_mix_g, norm_ffn_g, w_in, q_lora_g, w_uq, kv_lora_g, w_uk, w_uv, qn_g, qr_g, kn_g, kr_g, rw_mu, rw_w0, rw_w_up, rw_a0, rw_a_up, rw_g_up, rw_k_k, rw_k_a, rw_r_k, rw_lnx_g, rw_lnx_b, w_pool, pool_scale, w_out, ffn_w_up, ffn_conv_w, ffn_conv_b, ffn_w_down):
    B, T, D = x_prompt.shape
    DB, TS, _ = x_sample.shape
    assert TS == 1, "the sample group decodes one token per sequence"
    L = ada_w.shape[0]
    n_pages = page_table.shape[1]
    past = n_pages * PAGE
    H, N = RW_HEADS, RW_N
    F = ffn_w_down.shape[1]

    cols, pb_src, b_cols = _in_proj_columns()
    pb_valid = np.nonzero(pb_src >= 0)[0]
    pb_inverse = pb_valid[np.argsort(pb_src[pb_valid])]

    w_in_p = _gather_cols(w_in, cols).astype(BF16)
    w_uq_p = _gather_cols(w_uq, _uq_columns()).astype(BF16)
    w_uk_f = w_uk.reshape(L, w_uk.shape[1], MLA_HEADS * QK_NOPE)
    w_uk_b = w_uk_f.astype(BF16)
    w_uk_t = jnp.swapaxes(w_uk_b, 1, 2)
    cache_krope_t = jnp.swapaxes(cache_krope, 2, 3)
    w_uv_b = w_uv.reshape(L, w_uv.shape[1], MLA_HEADS * V_HEAD).astype(BF16)
    w_out_b = w_out.astype(BF16)
    w_up_b = ffn_w_up.astype(BF16)
    w_down_b = ffn_w_down.astype(BF16)
    w_pool_b = w_pool.astype(BF16)
    mu_p = _gather_cols(rw_mu, pb_src)
    rw_w_up_p = _pad_rows(rw_w_up, LANE).astype(BF16)
    rw_a_up_p = _pad_rows(rw_a_up, LANE).astype(BF16)
    rw_g_up_b = rw_g_up.astype(BF16)
    qr_g_p = _pad_lanes(qr_g, LANE)
    kr_g_p = _pad_lanes(kr_g, LANE)
    shift_p = _gather_cols(state_shift, pb_src)
    row2 = lambda a, l: a[l].reshape(1, -1)

    mc = DB + B
    c_all = _pad_rows(jnp.concatenate([c_sample, c_prompt], axis=0), mc + (-mc) % 8)
    mod = _ada_mod(c_all, ada_w, ada_b)

    cos_p, sin_p = _rope_tables(jnp.arange(T, dtype=jnp.int32))
    cos_s, sin_s = _rope_tables(jnp.full((DB,), past, jnp.int32))

    state_wkv_flat = state_wkv.reshape(L, DB, H * N * N)

    xp = x_prompt.reshape(B * T, D)
    xs = x_sample.reshape(DB, D)
    outs = [[] for _ in range(12)]

    for l in range(L):
        lw = dict(q_lora_g=row2(q_lora_g, l), w_uq=w_uq_p[l], kv_lora_g=row2(kv_lora_g, l), qn_g=row2(qn_g, l),
                  qr_g=qr_g_p[l].reshape(1, -1), kn_g=row2(kn_g, l), kr_g=kr_g_p[l].reshape(1, -1),
                  w_uk=w_uk_b[l], w_uv=w_uv_b[l],
                  rw_mu=mu_p[l].reshape(1, -1), rw_w0=row2(rw_w0, l), rw_w_up=rw_w_up_p[l], rw_a0=row2(rw_a0, l),
                  rw_a_up=rw_a_up_p[l], rw_g_up=rw_g_up_b[l], rw_k_k=row2(rw_k_k, l), rw_k_a=row2(rw_k_a, l),
                  rw_r_k=row2(rw_r_k, l), rw_lnx_g=row2(rw_lnx_g, l), rw_lnx_b=row2(rw_lnx_b, l))
        g_mix, g_ffn = row2(norm_mix_g, l), row2(norm_ffn_g, l)
        ps = row2(pool_scale, l)
        cw, cb = ffn_conv_w[l], row2(ffn_conv_b, l)
        mods_p = [mod[l, DB:DB + B, k * D:(k + 1) * D][:, None, :] for k in range(6)]
        mods_s = [mod[l, :DB, k * D:(k + 1) * D][None] for k in range(6)]

        sh1, sc1, g1, sh2, sc2, g2 = mods_p
        p = _norm_mod_matmul(xp, g_mix, sc1, sh1, w_in_p, l, T)
        q, k, v, lat, kr = _mla_proj(p, cos_p, sin_p, lw, T, BF16)
        ya = _attn_prompt(q, k, v, B, T)
        r_, wl_, k_, v_, kk_, b_, g_ = _rw_prep(p, None, lw, T)
        y_raw, st_t = _rw_chunked((r_, wl_, k_, v_, kk_, b_), B, T)
        yb = _rw_post(y_raw, r_, k_, v_, g_, lw)
        yc = _pool_prompt(p, w_pool_b[l], ps, T)
        xp = _matmul_gate_res([ya, yb, yc], w_out_b, l, g1, xp, T)
        act, tail_v, tail_g = _ffn_up_prompt(xp, g_ffn, sc2, sh2, w_up_b, l, cw, cb, B, T)
        xp = _matmul_gate_res([act], w_down_b, l, g2, xp, T)

        p3 = p.reshape(B, T, P_COLS)
        outs[0].append(lat.reshape(B, T, LANE))
        outs[2].append(kr.reshape(B, T, LANE)[:, :, :QK_ROPE])
        outs[4].append(jnp.swapaxes(st_t, -1, -2))
        outs[6].append(_gather_cols(p3[:, -1, :PB_W], pb_inverse))
        pool_rows = p3[:, max(T - POOL_BUF, 0):, PC_OFF:PC_OFF + 512]
        if T < POOL_BUF:
            pool_rows = jnp.concatenate([jnp.zeros((B, POOL_BUF - T, 512), F32), pool_rows], axis=1)
        outs[8].append(pool_rows)
        last = lambda t: t.reshape(B, -1, 8, F)[:, -1, 6:8]
        outs[10].append(jnp.concatenate([last(tail_v), last(tail_g)], axis=-1))

        sh1, sc1, g1, sh2, sc2, g2 = mods_s
        p = _norm_mod_matmul(xs, g_mix, sc1, sh1, w_in_p, l, DB)
        q, _, _, lat, kr = _mla_proj(p, cos_s, sin_s, lw, DB, F32)
        qt, qrp = _absorb_q(q, lw["kn_g"], w_uk_f[l])
        o_lat = _paged_attention(page_table, jnp.swapaxes(qt, 0, 1), jnp.swapaxes(qrp, 0, 1), lat, kr, w_uk_t[l],
                                 cache_latent, cache_krope_t, l)
        ya = _uv_out(jnp.swapaxes(o_lat, 0, 1), lw["w_uv"])
        r_, wl_, k_, v_, kk_, b_, g_ = _rw_prep(p, shift_p[l], lw, DB)
        heads = lambda a: jnp.swapaxes(a.reshape(DB, H, N), 0, 1)
        s_new, y_h = _rw_step(state_wkv_flat, l, [heads(a) for a in (r_, wl_, k_, v_, kk_, b_)])
        y_raw = jnp.swapaxes(y_h, 0, 1).reshape(DB, H * N)
        yb = _rw_post(y_raw, r_, k_, v_, g_, lw)
        yc = _pool_step(jnp.swapaxes(state_pool[l], 0, 1), p, w_pool_b[l], ps, past)
        xs = _matmul_gate_res([ya, yb, yc], w_out_b, l, g1, xs, DB)
        act, u_v, u_g = _ffn_up_step(xs, g_ffn, sc2, sh2, w_up_b, l, cw, cb, state_conv[l, :, 0], state_conv[l, :, 1])
        xs = _matmul_gate_res([act], w_down_b, l, g2, xs, DB)

        outs[1].append(lat.reshape(DB, 1, LANE))
        outs[3].append(kr.reshape(DB, 1, LANE)[:, :, :QK_ROPE])
        outs[5].append(s_new.reshape(DB, H, N, N))
        outs[7].append(_gather_cols(p[:, :PB_W], pb_inverse))
        outs[9].append(jnp.concatenate([state_pool[l][:, 1:], p[:, None, PC_OFF:PC_OFF + 512]], axis=1))
        outs[11].append(jnp.stack([state_conv[l, :, 1], jnp.concatenate([u_v, u_g], axis=-1)], axis=1))

    stacked = [jnp.stack(o) for o in outs]
    return (xp.reshape(B, T, D), xs.reshape(DB, 1, D), *stacked)
```

```python
import functools

import numpy as np
import jax
import jax.numpy as jnp
from jax import lax
from jax.experimental import pallas as pl
from jax.experimental.pallas import tpu as pltpu

F32 = jnp.float32
BF16 = jnp.bfloat16

MLA_HEADS = 8
QK_NOPE = 128
QK_ROPE = 64
V_HEAD = 128
ROPE_THETA = 10000.0
MLA_SCALE = (QK_NOPE + QK_ROPE) ** -0.5
RW_HEADS = 8
RW_N = 64
DECAY_LORA = 96
ICLR_LORA = 96
GATE_LORA = 256
GN_EPS = 64e-5
NORM_EPS = 1e-6
POOL_WINDOWS = (2, 4, 8, 16)
POOL_BUF = 15
PAGE = 128

LANE = 128
QK_PAD = 256
VMEM_LIMIT = 48 * 1024 * 1024

RW_W = RW_HEADS * RW_N
PB_W = 2048
CQ_OFF = PB_W
PC_OFF = CQ_OFF + 512
CKV_OFF = PC_OFF + 512
KR_OFF = CKV_OFF + 128
P_COLS = KR_OFF + 128

NN = (((1,), (0,)), ((), ()))
NT = (((1,), (1,)), ((), ()))
TN = (((0,), (0,)), ((), ()))


def _cp(*sem):
    return pltpu.CompilerParams(dimension_semantics=sem, vmem_limit_bytes=VMEM_LIMIT)


def _pallas(body, **kw):
    call = pl.pallas_call(body, **kw)

    def run(*args):
        if any(isinstance(a, jax.core.Tracer) for a in args):
            args = [a if a.dtype == jnp.int32 else pltpu.with_memory_space_constraint(a, pltpu.HBM) for a in args]
        return call(*args)
    return run


def _tile(n, pref):
    if n <= pref:
        return n
    t = pref - pref % 8
    while t >= 8:
        if n % t == 0:
            return t
        t -= 8
    return n


def _dot(a, b, dims=NN):
    return lax.dot_general(a, b, dims, preferred_element_type=F32)


def _split2(x):
    hi = x.astype(BF16)
    lo = (x - hi.astype(F32)).astype(BF16)
    return hi, lo


def _split3(x):
    hi = x.astype(BF16)
    r1 = x - hi.astype(F32)
    mid = r1.astype(BF16)
    lo = (r1 - mid.astype(F32)).astype(BF16)
    return hi, mid, lo


def _dot3(a, b, dims=NN):
    (ca,), (cb,) = dims[0]
    ah = a.astype(BF16).astype(F32)
    bh = b.astype(BF16).astype(F32)
    a3 = jnp.concatenate([ah, a - ah, ah], axis=ca).astype(BF16)
    b3 = jnp.concatenate([bh, bh, b - bh], axis=cb).astype(BF16)
    return _dot(a3, b3, dims)


def _dot_sel_l(sel, x, dims=NN):
    h, m, l = _split3(x)
    return _dot(sel, h, dims) + (_dot(sel, m, dims) + _dot(sel, l, dims))


def _dot_sel_r(x, sel, dims=NN):
    h, l = _split2(x)
    return _dot(h, sel, dims) + _dot(l, sel, dims)


def _sigmoid(x):
    return 1.0 / (1.0 + jnp.exp(-x))


def _rms(x, g, n):
    return (x * lax.rsqrt(jnp.sum(x * x, axis=-1, keepdims=True) / n + NORM_EPS)) * g


def _modnorm(x, g, sc, sh):
    return _rms(x, g, x.shape[-1]) * (1.0 + sc) + sh


def _mod_spec(tm, tn, tpg, rows, col_arg):
    if rows == 1:
        if col_arg:
            return pl.BlockSpec((1, 1, tn), lambda i, j: (i // tpg, 0, j))
        return pl.BlockSpec((1, 1, tn), lambda i, j: (i // tpg, 0, 0))
    if col_arg:
        return pl.BlockSpec((1, tm, tn), lambda i, j: (i // tpg, i % tpg, j))
    return pl.BlockSpec((1, tm, tn), lambda i, j: (i // tpg, i % tpg, 0))


def _ada_kernel(c_ref, w_ref, b_ref, o_ref):
    c = c_ref[...]
    s = c * _sigmoid(c)
    o_ref[0] = _dot(s.astype(BF16), w_ref[0].astype(BF16)) + b_ref[0]


def _ada_mod(c_all, ada_w, ada_b):
    L, D, N = ada_w.shape
    mc = c_all.shape[0]
    tn = _tile(N, 1024)
    return _pallas(
        _ada_kernel,
        out_shape=jax.ShapeDtypeStruct((L, mc, N), F32),
        grid=(L, N // tn),
        in_specs=[pl.BlockSpec((mc, D), lambda l, j: (0, 0)),
                  pl.BlockSpec((1, D, tn), lambda l, j: (l, 0, j)),
                  pl.BlockSpec((1, 1, tn), lambda l, j: (l, 0, j))],
        out_specs=pl.BlockSpec((1, mc, tn), lambda l, j: (l, 0, j)),
        compiler_params=_cp("parallel", "parallel"),
        name="ada_mod",
    )(c_all, ada_w, ada_b.reshape(L, 1, N))


def _nmm_kernel(x_ref, g_ref, sc_ref, sh_ref, w_ref, o_ref, h_ref):
    @pl.when(pl.program_id(1) == 0)
    def _():
        h_ref[...] = _modnorm(x_ref[...], g_ref[...], sc_ref[0], sh_ref[0]).astype(BF16)

    o_ref[...] = _dot(h_ref[...], w_ref[...])


def _norm_mod_matmul(x, g, sc, sh, w, layer, rpg, tm_pref=512, tn_pref=1664):
    M, D = x.shape
    N = w.shape[2]
    tm = _tile(rpg, tm_pref)
    tpg = rpg // tm
    tn = max(t for t in range(LANE, tn_pref + 1, LANE) if N % t == 0)
    rows = sc.shape[1]
    return _pallas(
        _nmm_kernel,
        out_shape=jax.ShapeDtypeStruct((M, N), F32),
        grid=(M // tm, N // tn),
        in_specs=[pl.BlockSpec((tm, D), lambda i, j: (i, 0)),
                  pl.BlockSpec((1, D), lambda i, j: (0, 0)),
                  _mod_spec(tm, D, tpg, rows, False),
                  _mod_spec(tm, D, tpg, rows, False),
                  pl.BlockSpec((None, D, tn), lambda i, j: (layer, 0, j))],
        out_specs=pl.BlockSpec((tm, tn), lambda i, j: (i, j)),
        scratch_shapes=[pltpu.VMEM((tm, D), BF16)],
        compiler_params=_cp("parallel", "arbitrary"),
        name="norm_mod_matmul",
    )(x, g, sc, sh, w)


def _mmres_kernel(*refs, widths):
    n = len(widths)
    a_refs, (w_ref, gate_ref, res_ref, o_ref) = refs[:n], refs[n:]
    acc, k0 = None, 0
    for a_ref, k in zip(a_refs, widths):
        part = _dot(a_ref[...], w_ref[k0:k0 + k, :])
        acc = part if acc is None else acc + part
        k0 += k
    o_ref[...] = res_ref[...] + gate_ref[0] * acc


def _matmul_gate_res(parts, w, layer, gate, res, rpg, tm_pref=1024, tn_pref=512):
    M = parts[0].shape[0]
    widths = tuple(a.shape[1] for a in parts)
    K, N = w.shape[1:]
    assert sum(widths) == K
    tm = _tile(rpg, tm_pref)
    tpg = rpg // tm
    tn = _tile(N, tn_pref)
    rows = gate.shape[1]
    return _pallas(
        functools.partial(_mmres_kernel, widths=widths),
        out_shape=jax.ShapeDtypeStruct((M, N), F32),
        grid=(M // tm, N // tn),
        in_specs=[pl.BlockSpec((tm, k), lambda i, j: (i, 0)) for k in widths]
        + [pl.BlockSpec((None, K, tn), lambda i, j: (layer, 0, j)),
           _mod_spec(tm, tn, tpg, rows, True),
           pl.BlockSpec((tm, tn), lambda i, j: (i, j))],
        out_specs=pl.BlockSpec((tm, tn), lambda i, j: (i, j)),
        compiler_params=_cp("parallel", "arbitrary"),
        name="matmul_gate_res",
    )(*parts, w, gate, res)


def _mla_proj_kernel(cq_ref, ckv_ref, kr_ref, cos_ref, sin_ref, qlg_ref, wuq_ref, kvg_ref, qng_ref, qrg_ref,
                     kng_ref, krg_ref, wuk_ref, wuv_ref, q_ref, k_ref, v_ref, lat_ref, kro_ref):
    cos = cos_ref[...]
    sin = sin_ref[...]
    lane = lax.broadcasted_iota(jnp.int32, cos.shape, 1)
    half = QK_ROPE // 2

    def rope(x):
        rot = jnp.where(lane < half, -pltpu.roll(x, LANE - half, 1), pltpu.roll(x, half, 1))
        return x * cos + rot * sin

    cqn = _rms(cq_ref[...], qlg_ref[...], cq_ref.shape[-1])
    q = _dot(cqn.astype(BF16), wuq_ref[...])
    lat = _rms(ckv_ref[...], kvg_ref[...], ckv_ref.shape[-1])
    lat_ref[...] = lat
    kr = rope(_rms(kr_ref[...], krg_ref[...], QK_ROPE))
    kro_ref[...] = kr
    latb = lat.astype(BF16)
    kx = _dot(latb, wuk_ref[...])
    v_ref[...] = _dot(latb, wuv_ref[...]).astype(v_ref.dtype)
    krb = kr.astype(k_ref.dtype)
    for h in range(MLA_HEADS):
        o = h * QK_PAD
        qn = _rms(q[:, o:o + QK_NOPE], qng_ref[...], QK_NOPE)
        qr = rope(_rms(q[:, o + QK_NOPE:o + QK_PAD], qrg_ref[...], QK_ROPE))
        q_ref[:, o:o + QK_NOPE] = qn.astype(q_ref.dtype)
        q_ref[:, o + QK_NOPE:o + QK_PAD] = qr.astype(q_ref.dtype)
        kn = _rms(kx[:, h * QK_NOPE:(h + 1) * QK_NOPE], kng_ref[...], QK_NOPE)
        k_ref[:, o:o + QK_NOPE] = kn.astype(k_ref.dtype)
        k_ref[:, o + QK_NOPE:o + QK_PAD] = krb


def _mla_proj(p, cos_t, sin_t, lw, rpg, q_dtype, tm_pref=256):
    M = p.shape[0]
    tm = _tile(rpg, tm_pref)
    tpg = rpg // tm
    H = MLA_HEADS
    row = lambda w: pl.BlockSpec((tm, w), lambda i: (i, 0))
    const = lambda a: pl.BlockSpec(a.shape, lambda i: (0,) * a.ndim)
    tab = pl.BlockSpec((tm, LANE), lambda i: (i % tpg, 0))
    weights = (lw["q_lora_g"], lw["w_uq"], lw["kv_lora_g"], lw["qn_g"], lw["qr_g"], lw["kn_g"], lw["kr_g"],
               lw["w_uk"], lw["w_uv"])
    return _pallas(
        _mla_proj_kernel,
        out_shape=(jax.ShapeDtypeStruct((M, H * QK_PAD), q_dtype),
                   jax.ShapeDtypeStruct((M, H * QK_PAD), BF16),
                   jax.ShapeDtypeStruct((M, H * V_HEAD), BF16),
                   jax.ShapeDtypeStruct((M, LANE), F32),
                   jax.ShapeDtypeStruct((M, LANE), F32)),
        grid=(M // tm,),
        in_specs=[pl.BlockSpec((tm, 512), lambda i: (i, CQ_OFF // 512)),
                  pl.BlockSpec((tm, LANE), lambda i: (i, CKV_OFF // LANE)),
                  pl.BlockSpec((tm, LANE), lambda i: (i, KR_OFF // LANE)),
                  tab, tab] + [const(a) for a in weights],
        out_specs=(row(H * QK_PAD), row(H * QK_PAD), row(H * V_HEAD), row(LANE), row(LANE)),
        compiler_params=_cp("parallel"),
        name="mla_proj",
    )(p, p, p, cos_t, sin_t, *weights)


ATTN_HEADS_PER_STEP = 2


def _attn_kernel(q_ref, k_ref, v_ref, o_ref, *, tq):
    qi = pl.program_id(2)
    hs = range(ATTN_HEADS_PER_STEP)
    q = [q_ref[:, h * QK_PAD:(h + 1) * QK_PAD] for h in hs]

    def block(kj, carry, causal):
        start = pl.multiple_of(kj * tq, tq)
        s = [_dot(q[h], k_ref[pl.ds(start, tq), h * QK_PAD:(h + 1) * QK_PAD], NT) * MLA_SCALE for h in hs]
        if causal:
            row = lax.broadcasted_iota(jnp.int32, (tq, tq), 0)
            col = lax.broadcasted_iota(jnp.int32, (tq, tq), 1)
            s = [jnp.where(col <= row, s[h], -jnp.inf) for h in hs]
        out = []
        for h in hs:
            m, l, acc = carry[h]
            m_new = jnp.maximum(m, jnp.max(s[h], axis=-1, keepdims=True))
            alpha = jnp.exp(m - m_new)
            e = jnp.exp(s[h] - m_new)
            l = alpha * l + jnp.sum(e, axis=-1, keepdims=True)
            v = v_ref[pl.ds(start, tq), h * V_HEAD:(h + 1) * V_HEAD]
            out.append((m_new, l, alpha * acc + _dot(e.astype(BF16), v)))
        return tuple(out)

    init = tuple((jnp.full((tq, 1), -jnp.inf, F32), jnp.zeros((tq, 1), F32), jnp.zeros((tq, V_HEAD), F32))
                 for _ in hs)
    carry = block(qi, init, True)
    carry = lax.fori_loop(0, qi, lambda kj, c: block(kj, c, False), carry)
    for h in hs:
        m, l, acc = carry[h]
        o_ref[:, h * V_HEAD:(h + 1) * V_HEAD] = (acc / l).astype(o_ref.dtype)


def _attn_prompt(q, k, v, B, T, tq_pref=512):
    HP = MLA_HEADS // ATTN_HEADS_PER_STEP
    qw, vw = ATTN_HEADS_PER_STEP * QK_PAD, ATTN_HEADS_PER_STEP * V_HEAD
    tq = _tile(T, tq_pref)
    nq = T // tq
    return _pallas(
        functools.partial(_attn_kernel, tq=tq),
        out_shape=jax.ShapeDtypeStruct((B * T, MLA_HEADS * V_HEAD), BF16),
        grid=(B, HP, nq),
        in_specs=[pl.BlockSpec((tq, qw), lambda b, h, i: (b * nq + i, h)),
                  pl.BlockSpec((T, qw), lambda b, h, i: (b, h)),
                  pl.BlockSpec((T, vw), lambda b, h, i: (b, h))],
        out_specs=pl.BlockSpec((tq, vw), lambda b, h, i: (b * nq + i, h)),
        compiler_params=_cp("parallel", "parallel", "arbitrary"),
        name="attn_prompt",
    )(q, k, v)


def _absorb_kernel(q_ref, kng_ref, wuk_ref, qt_ref, qr_ref):
    g = kng_ref[...]
    for h in range(MLA_HEADS):
        o = h * QK_PAD
        qn = q_ref[:, o:o + QK_NOPE] * g
        wh = wuk_ref[:, h * QK_NOPE:(h + 1) * QK_NOPE]
        qt_ref[h] = _dot3(qn, wh, NT)
        qr_ref[h] = q_ref[:, o + QK_NOPE:o + QK_PAD]


def _absorb_q(q, kn_g, w_uk):
    M = q.shape[0]
    H = MLA_HEADS
    return _pallas(
        _absorb_kernel,
        out_shape=(jax.ShapeDtypeStruct((H, M, LANE), F32), jax.ShapeDtypeStruct((H, M, LANE), F32)),
        compiler_params=pltpu.CompilerParams(vmem_limit_bytes=VMEM_LIMIT),
        name="absorb_q",
    )(q, kn_g, w_uk)


def _paged_kernel(pt_ref, qt_ref, qr_ref, latn_ref, krn_ref, wukt_ref, lat_hbm, kr_hbm, o_ref,
                  latbuf, krbuf, lhs_ref, s_ref, sem, *, layer, n_pages, chunk_pages):
    b = pl.program_id(0)
    nb = pl.num_programs(0)
    slot = b % 2
    H = MLA_HEADS
    HD = H * QK_NOPE
    n_chunks = n_pages // chunk_pages
    rows = chunk_pages * PAGE
    R = latbuf.shape[-1]

    def page_copies(seq, sl, c, j):
        p = c * chunk_pages + j
        pg = pt_ref[seq * n_pages + p]
        return (pltpu.make_async_copy(lat_hbm.at[layer, pg], latbuf.at[sl, p], sem.at[sl, 0, c]),
                pltpu.make_async_copy(kr_hbm.at[layer, pg],
                                      krbuf.at[sl, :, pl.ds(pl.multiple_of(p * PAGE, PAGE), PAGE)], sem.at[sl, 1, c]))

    def fetch_chunk(seq, sl, c):
        for j in range(chunk_pages):
            for cp in page_copies(seq, sl, c, j):
                cp.start()

    def wait_chunk(seq, sl, c):
        for j in range(chunk_pages):
            for cp in page_copies(seq, sl, c, j):
                cp.wait()

    @pl.when(b == 0)
    def _():
        def body(c, carry):
            fetch_chunk(0, 0, c)
            return carry
        lax.fori_loop(0, n_chunks, body, 0)

    lhs_ref[pl.ds(0, HD), :] = wukt_ref[...]
    lhs_ref[pl.ds(HD, 16), :] = jnp.concatenate([qt_ref[0], jnp.zeros_like(qt_ref[0])], axis=0).astype(BF16)
    lhs = lhs_ref[...]
    qr = qr_ref[0][:, :QK_ROPE].astype(BF16)

    def latent_scores(latb):
        xt = _dot(lhs, latb, NT)
        ms = []
        for h in range(H):
            acc = None
            for i in range(QK_NOPE // 8):
                blk = xt[h * QK_NOPE + 8 * i:h * QK_NOPE + 8 * i + 8]
                acc = blk * blk if acc is None else acc + blk * blk
            ms.append(jnp.sum(acc, axis=0, keepdims=True))
        ms = jnp.concatenate(ms, axis=0)
        return xt[HD:HD + H] * lax.rsqrt(ms / QK_NOPE + NORM_EPS)

    nxt = jnp.minimum(b + 1, nb - 1)

    def wait_body(c, carry):
        wait_chunk(b, slot, c)
        return carry
    lax.fori_loop(0, n_chunks, wait_body, 0)

    def score_chunk(c, carry):
        fetch_chunk(nxt, 1 - slot, c)

        p0 = pl.multiple_of(c * chunk_pages, chunk_pages)
        k0 = pl.multiple_of(c * rows, rows)
        s_rope = _dot(qr, krbuf[slot, :, pl.ds(k0, rows)].astype(BF16))
        latb = latbuf[slot, pl.ds(p0, chunk_pages)].reshape(rows, R).astype(BF16)
        s_ref[:, pl.ds(k0, rows)] = (latent_scores(latb) + s_rope) * MLA_SCALE
        return carry

    unroll = 4 if n_chunks % 4 == 0 else 1
    lax.fori_loop(0, n_chunks, score_chunk, 0, unroll=unroll)

    latn = latn_ref[0]
    latnb = jnp.concatenate([latn, jnp.zeros((PAGE - latn.shape[0], R), F32)], axis=0).astype(BF16)
    krn = krn_ref[0][0:1, :QK_ROPE].astype(BF16).astype(F32)
    s_rope = jnp.sum(qr.astype(F32) * krn, axis=-1, keepdims=True)
    s_new = (latent_scores(latnb) + s_rope) * MLA_SCALE
    s_new = jnp.where(lax.broadcasted_iota(jnp.int32, s_new.shape, 1) == 0, s_new, -jnp.inf)

    s_all = s_ref[...]
    m = jnp.maximum(jnp.max(s_all, axis=-1, keepdims=True), jnp.max(s_new, axis=-1, keepdims=True))
    e_new = jnp.exp(s_new - m)
    s_ref[...] = jnp.exp(s_all - m)
    l = jnp.sum(s_ref[...], axis=-1, keepdims=True) + jnp.sum(e_new, axis=-1, keepdims=True)

    def value_chunk(c, acc):
        p0 = pl.multiple_of(c * chunk_pages, chunk_pages)
        k0 = pl.multiple_of(c * rows, rows)
        latb = latbuf[slot, pl.ds(p0, chunk_pages)].reshape(rows, R).astype(BF16)
        return acc + _dot(s_ref[:, pl.ds(k0, rows)].astype(BF16), latb)

    acc = lax.fori_loop(0, n_chunks, value_chunk, _dot(e_new.astype(BF16), latnb), unroll=unroll)
    o_ref[0] = acc / l

    @pl.when(b == nb - 1)
    def _():
        def body(c, carry):
            wait_chunk(nxt, 1 - slot, c)
            return carry
        lax.fori_loop(0, n_chunks, body, 0)


def _paged_attention(page_table, qt, qr, lat_new, kr_new, w_uk_t, cache_latent, cache_krope_t, layer):
    DB, n_pages = page_table.shape
    H = MLA_HEADS
    R = cache_latent.shape[-1]
    chunk_pages = 8 if n_pages % 8 == 0 else 1
    pad8 = lambda a: jnp.pad(a[:, None, :], ((0, 0), (0, 7), (0, 0)))
    seq = lambda w: pl.BlockSpec((1, 8, w), lambda b, pt: (b, 0, 0))
    grid_spec = pltpu.PrefetchScalarGridSpec(
        num_scalar_prefetch=1,
        grid=(DB,),
        in_specs=[seq(LANE), seq(LANE), seq(LANE), seq(LANE),
                  pl.BlockSpec(w_uk_t.shape, lambda b, pt: (0, 0)),
                  pl.BlockSpec(memory_space=pl.ANY),
                  pl.BlockSpec(memory_space=pl.ANY)],
        out_specs=pl.BlockSpec((1, H, R), lambda b, pt: (b, 0, 0)),
        scratch_shapes=[pltpu.VMEM((2, n_pages, PAGE, R), F32),
                        pltpu.VMEM((2, QK_ROPE, n_pages * PAGE), F32),
                        pltpu.VMEM((w_uk_t.shape[0] + 16, R), BF16),
                        pltpu.VMEM((H, n_pages * PAGE), F32),
                        pltpu.SemaphoreType.DMA((2, 2, n_pages // chunk_pages))],
    )
    return _pallas(
        functools.partial(_paged_kernel, layer=layer, n_pages=n_pages, chunk_pages=chunk_pages),
        out_shape=jax.ShapeDtypeStruct((DB, H, R), F32),
        grid_spec=grid_spec,
        compiler_params=_cp("arbitrary"),
        name="paged_attention",
    )(page_table.reshape(-1), qt, qr, pad8(lat_new), pad8(kr_new), w_uk_t, cache_latent, cache_krope_t)


def _uv_kernel(o_ref, wuv_ref, y_ref):
    for h in range(MLA_HEADS):
        y_ref[:, h * V_HEAD:(h + 1) * V_HEAD] = _dot(
            o_ref[h].astype(BF16), wuv_ref[:, h * V_HEAD:(h + 1) * V_HEAD]).astype(y_ref.dtype)


def _uv_out(o_lat, w_uv):
    H, M, _ = o_lat.shape
    return _pallas(
        _uv_kernel,
        out_shape=jax.ShapeDtypeStruct((M, H * V_HEAD), BF16),
        compiler_params=pltpu.CompilerParams(vmem_limit_bytes=VMEM_LIMIT),
        name="uv_out",
    )(o_lat, w_uv)


def _rw_prep_body(pb, prev, mu_ref, w0_ref, wup_ref, a0_ref, aup_ref, gup_ref, kk_ref, ka_ref, seg_ref,
                  r_o, wl_o, k_o, v_o, kk_o, b_o, g_o):
    W = RW_W
    xs = pb + (prev - pb) * mu_ref[...]
    r = xs[:, 0:W]
    k = xs[:, W:2 * W]
    v = xs[:, 2 * W:3 * W]
    wd = xs[:, 3 * W:3 * W + LANE]
    ad = xs[:, 3 * W + LANE:3 * W + 2 * LANE]
    gd = xs[:, 3 * W + 2 * LANE:]
    z = -(w0_ref[...] + _dot(jnp.tanh(wd).astype(BF16), wup_ref[...]))
    softplus = jnp.maximum(z, 0.0) + jnp.log1p(jnp.exp(-jnp.abs(z)))
    w = -softplus - 0.5
    a = _sigmoid(a0_ref[...] + _dot(ad.astype(BF16), aup_ref[...]))
    g = _dot(_sigmoid(gd).astype(BF16), gup_ref[...])
    kk = k * kk_ref[...]
    k2 = k * (1.0 + (a - 1.0) * ka_ref[...])
    n2 = _dot_sel_r(kk * kk, seg_ref[...])
    kkn = kk / jnp.maximum(jnp.sqrt(n2), 1e-12)
    r_o[...] = r
    wl_o[...] = -jnp.exp(w)
    k_o[...] = k2
    v_o[...] = v
    kk_o[...] = kkn
    b_o[...] = kkn * a
    g_o[...] = g


def _rw_prep_seq_kernel(pb_ref, halo_ref, *rest, tpg):
    pb = pb_ref[...]
    i = pl.program_id(0)
    rolled = pltpu.roll(pb, 1, 0)
    row = lax.broadcasted_iota(jnp.int32, pb.shape, 0)
    first_row = jnp.where(i % tpg == 0, 0, -1)
    prev = jnp.where(row == 0, halo_ref[7:8, :], rolled)
    prev = jnp.where(row == first_row, 0.0, prev)
    _rw_prep_body(pb, prev, *rest)


def _rw_prep_step_kernel(pb_ref, prev_ref, *rest):
    _rw_prep_body(pb_ref[...], prev_ref[...], *rest)


def _seg_matrix(n, w):
    return jnp.asarray(np.kron(np.eye(n // w, dtype=np.float32), np.ones((w, w), np.float32)), BF16)


def _rw_prep(p, prev, lw, rpg, tm_pref=256):
    M = p.shape[0]
    tm = _tile(rpg, tm_pref)
    tpg = rpg // tm
    seg = _seg_matrix(RW_W, RW_N)
    weights = (lw["rw_mu"], lw["rw_w0"], lw["rw_w_up"], lw["rw_a0"], lw["rw_a_up"], lw["rw_g_up"], lw["rw_k_k"],
               lw["rw_k_a"], seg)
    const = lambda a: pl.BlockSpec(a.shape, lambda i: (0,) * a.ndim)
    out = jax.ShapeDtypeStruct((M, RW_W), F32)
    ospec = pl.BlockSpec((tm, RW_W), lambda i: (i, 0))
    main = pl.BlockSpec((tm, PB_W), lambda i: (i, 0))
    if prev is None:
        kern = functools.partial(_rw_prep_seq_kernel, tpg=tpg)
        second = pl.BlockSpec((8, PB_W), lambda i: (jnp.maximum(i * (tm // 8) - 1, 0), 0))
        second_arg = p
    else:
        kern = _rw_prep_step_kernel
        second = pl.BlockSpec((tm, PB_W), lambda i: (i, 0))
        second_arg = prev
    return _pallas(
        kern,
        out_shape=(out,) * 7,
        grid=(M // tm,),
        in_specs=[main, second] + [const(a) for a in weights],
        out_specs=(ospec,) * 7,
        compiler_params=_cp("parallel"),
        name="rwkv_prep",
    )(p, second_arg, *weights)


def _rw_chunk_kernel(r_ref, wl_ref, k_ref, v_ref, kk_ref, b_ref, y_ref, s_ref, st_ref, *, C, G):
    step = pl.program_id(1)

    @pl.when(step == 0)
    def _():
        st_ref[...] = jnp.zeros_like(st_ref)

    N = RW_N
    ri = lax.broadcasted_iota(jnp.int32, (C, C), 0)
    ci = lax.broadcasted_iota(jnp.int32, (C, C), 1)
    incl = ci <= ri
    strict = ci < ri
    eye_c = (ci == ri).astype(F32)
    rn = lax.broadcasted_iota(jnp.int32, (N, N), 0)
    cn = lax.broadcasted_iota(jnp.int32, (N, N), 1)
    eye_n = rn == cn
    rg = lax.broadcasted_iota(jnp.int32, (G * C, G * C), 0)
    cg = lax.broadcasted_iota(jnp.int32, (G * C, G * C), 1)
    same_chunk_incl = (cg <= rg) & (rg // C == cg // C)

    wl = wl_ref[...]
    cs = _dot_sel_l(same_chunk_incl.astype(BF16), wl)
    p_in = jnp.exp(cs)
    p_ex = jnp.exp(cs - wl)
    p_inv = jnp.exp(-cs)
    at = -kk_ref[...] * p_ex
    rp = r_ref[...] * p_in
    bp = b_ref[...] * p_inv
    kp = k_ref[...] * p_inv
    v = v_ref[...]

    items = [(g, h) for g in range(G) for h in range(RW_HEADS)]
    cut = lambda x: [x[g * C:(g + 1) * C, h * N:(h + 1) * N] for g, h in items]
    p_end = [p_in[(g + 1) * C - 1:(g + 1) * C, h * N:(h + 1) * N] for g, h in items]
    at_i, rp_i, bp_i, kp_i, v_i = (cut(x) for x in (at, rp, bp, kp, v))
    n = range(len(items))
    lhs = [jnp.concatenate([at_i[i], rp_i[i]], axis=0) for i in n]
    gb = [_dot3(lhs[i], bp_i[i], NT) for i in n]
    gk = [_dot3(lhs[i], kp_i[i], NT) for i in n]
    a_ab = [jnp.where(strict, gb[i][:C], 0.0) for i in n]
    a_rb = [jnp.where(incl, gb[i][C:], 0.0) for i in n]
    a_ak = [jnp.where(strict, gk[i][:C], 0.0) for i in n]
    a_rk = [jnp.where(incl, gk[i][C:], 0.0) for i in n]
    akv = [_dot3(a_ak[i], v_i[i]) for i in n]
    rkv = [_dot3(a_rk[i], v_i[i]) for i in n]
    kv = [_dot3(kp_i[i] * p_end[i], v_i[i], TN) for i in n]
    t = None
    m = 1
    while m < C:
        sh = m.bit_length() - 1
        low_left = ((ri >> (sh + 1)) == (ci >> (sh + 1))) & (((ri >> sh) & 1) == 1) & (((ci >> sh) & 1) == 0)
        a_off = [jnp.where(low_left, a_ab[i], 0.0) for i in n]
        if m == 1:
            t = [eye_c + a_off[i] for i in n]
        else:
            inner = [_dot3(a_off[i], t[i]) for i in n]
            t = [t[i] + _dot3(t[i], inner[i]) for i in n]
        m *= 2
    w = [_dot3(t[i], jnp.concatenate([at_i[i], akv[i]], axis=1)) for i in n]
    gg = [_dot3(a_rb[i], w[i]) for i in n]
    mm = [_dot3(bp_i[i] * p_end[i], w[i], TN) for i in n]
    top = [jnp.concatenate([rp_i[i] + gg[i][:, :N], jnp.where(eye_n, p_end[i], 0.0) + mm[i][:, :N]], axis=0)
           for i in n]
    st = [st_ref[h] for h in range(RW_HEADS)]
    for g in range(G):
        idx = [g * RW_HEADS + h for h in range(RW_HEADS)]
        res = [_dot3(top[i], st[h]) for h, i in enumerate(idx)]
        st = [res[h][C:] + (mm[i][:, N:] + kv[i]) for h, i in enumerate(idx)]
        y_ref[pl.ds(g * C, C), :] = jnp.concatenate(
            [res[h][:C] + (gg[i][:, N:] + rkv[i]) for h, i in enumerate(idx)], axis=1)
    for h in range(RW_HEADS):
        st_ref[h] = st[h]

    @pl.when(step == pl.num_programs(1) - 1)
    def _():
        s_ref[0] = st_ref[...]


def _rw_chunked(ops, B, T):
    C = _tile(T, 64)
    nc = T // C
    G = 4 if nc % 4 == 0 else (2 if nc % 2 == 0 else 1)
    ns = nc // G
    blk = pl.BlockSpec((G * C, RW_W), lambda b, s: (b * ns + s, 0))
    return _pallas(
        functools.partial(_rw_chunk_kernel, C=C, G=G),
        out_shape=(jax.ShapeDtypeStruct((B * T, RW_W), F32),
                   jax.ShapeDtypeStruct((B, RW_HEADS, RW_N, RW_N), F32)),
        grid=(B, ns),
        in_specs=[blk] * 6,
        out_specs=(blk, pl.BlockSpec((1, RW_HEADS, RW_N, RW_N), lambda b, s: (b, 0, 0, 0))),
        scratch_shapes=[pltpu.VMEM((RW_HEADS, RW_N, RW_N), F32)],
        compiler_params=_cp("parallel", "arbitrary"),
        name="rwkv_chunked",
    )(*ops)


def _rw_step_kernel(s_ref, r_ref, wl_ref, k_ref, v_ref, kk_ref, b_ref, ej_ref, ei_ref, rd_ref, so_ref, y_ref):
    s = s_ref[0]
    ej = ej_ref[...]
    ei = ei_ref[...]
    rd = rd_ref[...]
    over_j = lambda x: _dot_sel_r(x, ej)
    sa = _dot_sel_r(s * over_j(kk_ref[0]), rd)
    s_new = (s * over_j(jnp.exp(wl_ref[0])) - _dot_sel_r(sa, ei) * over_j(b_ref[0])
             + _dot_sel_r(v_ref[0], ei) * over_j(k_ref[0]))
    so_ref[0] = s_new
    y_ref[0] = _dot_sel_r(s_new * over_j(r_ref[0]), rd)


def _rw_step(state_wkv, layer, vecs):
    _, DB, _ = state_wkv.shape
    H, N = RW_HEADS, RW_N
    bt = _tile(DB, 128)
    ej = jnp.asarray(np.kron(np.ones((1, N), np.float32), np.eye(N, dtype=np.float32)), BF16)
    ei = jnp.asarray(np.kron(np.eye(N, dtype=np.float32), np.ones((1, N), np.float32)), BF16)
    rd = ei.T
    sblk = pl.BlockSpec((1, bt, N * N), lambda i, h: (layer, i, h))
    oblk = pl.BlockSpec((1, bt, N * N), lambda i, h: (0, i, h))
    vblk = pl.BlockSpec((1, bt, N), lambda i, h: (h, i, 0))
    const = lambda a: pl.BlockSpec(a.shape, lambda i, h: (0, 0))
    s_new, y = _pallas(
        _rw_step_kernel,
        out_shape=(jax.ShapeDtypeStruct((1, DB, H * N * N), F32), jax.ShapeDtypeStruct((H, DB, N), F32)),
        grid=(DB // bt, H),
        in_specs=[sblk] + [vblk] * 6 + [const(ej), const(ei), const(rd)],
        out_specs=(oblk, vblk),
        compiler_params=_cp("parallel", "parallel"),
        name="rwkv_step",
    )(state_wkv, *vecs, ej, ei, rd)
    return s_new[0], y


def _rw_post_kernel(y_ref, r_ref, k_ref, v_ref, g_ref, lng_ref, lnb_ref, rk_ref, seg_ref, o_ref):
    seg = seg_ref[...]
    y = y_ref[...]
    mu = _dot_sel_r(y, seg) / RW_N
    d = y - mu
    var = _dot_sel_r(d * d, seg) / RW_N
    yn = (d * lax.rsqrt(var + GN_EPS)) * lng_ref[...] + lnb_ref[...]
    bonus = _dot_sel_r(r_ref[...] * k_ref[...] * rk_ref[...], seg) * v_ref[...]
    o_ref[...] = ((yn + bonus) * g_ref[...]).astype(o_ref.dtype)


def _rw_post(y, r, k, v, g, lw, tm_pref=256):
    M = y.shape[0]
    tm = _tile(M, tm_pref)
    seg = _seg_matrix(RW_W, RW_N)
    row = pl.BlockSpec((tm, RW_W), lambda i: (i, 0))
    const = lambda a: pl.BlockSpec(a.shape, lambda i: (0,) * a.ndim)
    weights = (lw["rw_lnx_g"], lw["rw_lnx_b"], lw["rw_r_k"], seg)
    return _pallas(
        _rw_post_kernel,
        out_shape=jax.ShapeDtypeStruct((M, RW_W), BF16),
        grid=(M // tm,),
        in_specs=[row] * 5 + [const(a) for a in weights],
        out_specs=row,
        compiler_params=_cp("parallel"),
        name="rwkv_post",
    )(y, r, k, v, g, *weights)


def _pool_windows(x_ext, halo_rows, count_of, wp_ref, ps_ref, o_ref):
    G = len(POOL_WINDOWS)
    gw = x_ext.shape[1] // G
    for gi, win in enumerate(POOL_WINDOWS):
        xg = x_ext[:, gi * gw:(gi + 1) * gw]
        s = xg
        step = 1
        while step < win:
            s = s + pltpu.roll(s, step, 0)
            step *= 2
        diff = s[halo_rows:] / count_of(win) - xg[halo_rows:]
        o = _dot(diff.astype(BF16), wp_ref[gi]) * ps_ref[:, gi * gw:(gi + 1) * gw]
        o_ref[:, gi * gw:(gi + 1) * gw] = o.astype(o_ref.dtype)


def _pool_seq_kernel(pc_ref, halo_ref, wp_ref, ps_ref, o_ref, *, tm, tpg):
    i = pl.program_id(0)
    t0 = (i % tpg) * tm
    hrow = lax.broadcasted_iota(jnp.int32, (halo_ref.shape[0], 1), 0)
    halo = jnp.where(hrow < jnp.where(t0 == 0, halo_ref.shape[0], 0), 0.0, halo_ref[...])
    x_ext = jnp.concatenate([halo, pc_ref[...]], axis=0)
    pos = t0 + lax.broadcasted_iota(jnp.int32, (tm, 1), 0)
    count_of = lambda win: jnp.minimum(pos + 1, win).astype(F32)
    _pool_windows(x_ext, halo_ref.shape[0], count_of, wp_ref, ps_ref, o_ref)


def _pool_prompt(p, w_pool, pool_scale, rpg, tm_pref=256):
    M = p.shape[0]
    W = 512
    HALO = 16
    tm = _tile(rpg, tm_pref)
    tpg = rpg // tm
    return _pallas(
        functools.partial(_pool_seq_kernel, tm=tm, tpg=tpg),
        out_shape=jax.ShapeDtypeStruct((M, W), BF16),
        grid=(M // tm,),
        in_specs=[pl.BlockSpec((tm, W), lambda i: (i, PC_OFF // W)),
                  pl.BlockSpec((HALO, W), lambda i: (jnp.maximum(i * (tm // HALO) - 1, 0), PC_OFF // W)),
                  pl.BlockSpec(w_pool.shape, lambda i: (0, 0, 0)),
                  pl.BlockSpec(pool_scale.shape, lambda i: (0, 0))],
        out_specs=pl.BlockSpec((tm, W), lambda i: (i, 0)),
        compiler_params=_cp("parallel"),
        name="pool_prompt",
    )(p, p, w_pool, pool_scale)


def _pool_step_kernel(st_ref, pc_ref, wp_ref, ps_ref, o_ref, *, past):
    G = len(POOL_WINDOWS)
    pc = pc_ref[...]
    gw = pc.shape[1] // G
    nbuf = st_ref.shape[0]
    for gi, win in enumerate(POOL_WINDOWS):
        sl = slice(gi * gw, (gi + 1) * gw)
        xg = pc[:, sl]
        total = xg
        for j in range(1, win):
            total = total + st_ref[nbuf - j][:, sl]
        diff = total / float(min(past + 1, win)) - xg
        o = _dot(diff.astype(BF16), wp_ref[gi]) * ps_ref[:, sl]
        o_ref[:, sl] = o.astype(o_ref.dtype)


def _pool_step(state_t, p, w_pool, pool_scale, past):
    nbuf, DB, W = state_t.shape
    return _pallas(
        functools.partial(_pool_step_kernel, past=past),
        out_shape=jax.ShapeDtypeStruct((DB, W), BF16),
        grid=(1,),
        in_specs=[pl.BlockSpec((nbuf, DB, W), lambda i: (0, 0, 0)),
                  pl.BlockSpec((DB, W), lambda i: (0, PC_OFF // W)),
                  pl.BlockSpec(w_pool.shape, lambda i: (0, 0, 0)),
                  pl.BlockSpec(pool_scale.shape, lambda i: (0, 0))],
        out_specs=pl.BlockSpec((DB, W), lambda i: (0, 0)),
        compiler_params=_cp("arbitrary"),
        name="pool_step",
    )(state_t, p, w_pool, pool_scale)


FFN_HALO = 16
FFN_SUB = 256


def _ffn_up_seq_kernel(x_ref, xh_ref, g_ref, sc_ref, sh_ref, wv_ref, wg_ref, cwv_ref, cwg_ref, cbv_ref, cbg_ref,
                       act_ref, tv_ref, tg_ref, h_ref, *, tm, tpg):
    i = pl.program_id(0)

    @pl.when(pl.program_id(1) == 0)
    def _():
        g, sc, sh = g_ref[...], sc_ref[0], sh_ref[0]
        h_ref[pl.ds(0, FFN_HALO), :] = _modnorm(xh_ref[...], g, sc, sh).astype(BF16)
        h_ref[pl.ds(FFN_HALO, tm), :] = _modnorm(x_ref[...], g, sc, sh).astype(BF16)

    h = h_ref[...]
    row = lax.broadcasted_iota(jnp.int32, (tm + FFN_HALO, 1), 0)
    keep = row >= jnp.where(i % tpg == 0, FFN_HALO, 0)

    def branch(w_ref, cw_ref, cb_ref, t_ref, cs):
        u = jnp.where(keep, _dot(h, w_ref[:, cs]), 0.0)
        t_ref[0, :, cs] = u[tm + FFN_HALO - 8:, :]
        cw = cw_ref[:, cs]
        return (cb_ref[:, cs] + pltpu.roll(u, 2, 0)[FFN_HALO:] * cw[0:1] + pltpu.roll(u, 1, 0)[FFN_HALO:] * cw[1:2]
                + u[FFN_HALO:] * cw[2:3])

    tn = act_ref.shape[1]
    for c0 in range(0, tn, FFN_SUB):
        cs = slice(c0, min(c0 + FFN_SUB, tn))
        val = branch(wv_ref, cwv_ref, cbv_ref, tv_ref, cs)
        gate = branch(wg_ref, cwg_ref, cbg_ref, tg_ref, cs)
        act_ref[:, cs] = ((gate * _sigmoid(gate)) * val).astype(act_ref.dtype)


def _ffn_up_prompt(x, g, sc, sh, w_up, layer, conv_w, conv_b, B, T, tm_pref=1024, tn_pref=704):
    M, D = x.shape
    F = w_up.shape[2] // 2
    tm = _tile(T, tm_pref)
    tpg = T // tm
    tn = max(t for t in range(LANE, tn_pref + 1, LANE) if F % t == 0)
    nj = F // tn
    lo = lambda i, j: (0, j)
    hi = lambda i, j: (0, nj + j)
    tail = jax.ShapeDtypeStruct((M // tm, 8, F), F32)
    tspec = pl.BlockSpec((1, 8, tn), lambda i, j: (i, 0, j))
    return _pallas(
        functools.partial(_ffn_up_seq_kernel, tm=tm, tpg=tpg),
        out_shape=(jax.ShapeDtypeStruct((M, F), BF16), tail, tail),
        grid=(M // tm, nj),
        in_specs=[pl.BlockSpec((tm, D), lambda i, j: (i, 0)),
                  pl.BlockSpec((FFN_HALO, D), lambda i, j: (jnp.maximum(i * (tm // FFN_HALO) - 1, 0), 0)),
                  pl.BlockSpec((1, D), lambda i, j: (0, 0)),
                  _mod_spec(tm, D, tpg, 1, False),
                  _mod_spec(tm, D, tpg, 1, False),
                  pl.BlockSpec((None, D, tn), lambda i, j: (layer, 0, j)),
                  pl.BlockSpec((None, D, tn), lambda i, j: (layer, 0, nj + j)),
                  pl.BlockSpec((3, tn), lo), pl.BlockSpec((3, tn), hi),
                  pl.BlockSpec((1, tn), lo), pl.BlockSpec((1, tn), hi)],
        out_specs=(pl.BlockSpec((tm, tn), lambda i, j: (i, j)), tspec, tspec),
        scratch_shapes=[pltpu.VMEM((tm + FFN_HALO, D), BF16)],
        compiler_params=_cp("arbitrary", "arbitrary"),
        name="ffn_up_prompt",
    )(x, x, g, sc, sh, w_up, w_up, conv_w, conv_w, conv_b, conv_b)


def _ffn_up_step_kernel(x_ref, g_ref, sc_ref, sh_ref, wv_ref, wg_ref, cwv_ref, cwg_ref, cbv_ref, cbg_ref,
                        s0v_ref, s0g_ref, s1v_ref, s1g_ref, act_ref, uv_ref, ug_ref, h_ref):
    @pl.when(pl.program_id(0) == 0)
    def _():
        h_ref[...] = _modnorm(x_ref[...], g_ref[...], sc_ref[0], sh_ref[0]).astype(BF16)

    h = h_ref[...]

    def branch(w_ref, cw_ref, cb_ref, s0_ref, s1_ref, u_ref):
        u = _dot(h, w_ref[...])
        u_ref[...] = u
        cw = cw_ref[...]
        return cb_ref[...] + s0_ref[...] * cw[0:1] + s1_ref[...] * cw[1:2] + u * cw[2:3]

    val = branch(wv_ref, cwv_ref, cbv_ref, s0v_ref, s1v_ref, uv_ref)
    gate = branch(wg_ref, cwg_ref, cbg_ref, s0g_ref, s1g_ref, ug_ref)
    act_ref[...] = ((gate * _sigmoid(gate)) * val).astype(act_ref.dtype)


def _ffn_up_step(x, g, sc, sh, w_up, layer, conv_w, conv_b, st0, st1, tn_pref=512):
    M, D = x.shape
    F = w_up.shape[2] // 2
    tn = _tile(F, tn_pref)
    nj = F // tn
    lo = lambda j: (0, j)
    hi = lambda j: (0, nj + j)
    full = lambda a: pl.BlockSpec(a.shape, lambda j: (0,) * a.ndim)
    u = jax.ShapeDtypeStruct((M, F), F32)
    ospec = pl.BlockSpec((M, tn), lo)
    return _pallas(
        _ffn_up_step_kernel,
        out_shape=(jax.ShapeDtypeStruct((M, F), BF16), u, u),
        grid=(nj,),
        in_specs=[full(x), full(g), full(sc), full(sh),
                  pl.BlockSpec((None, D, tn), lambda j: (layer, 0, j)),
                  pl.BlockSpec((None, D, tn), lambda j: (layer, 0, nj + j)),
                  pl.BlockSpec((3, tn), lo), pl.BlockSpec((3, tn), hi),
                  pl.BlockSpec((1, tn), lo), pl.BlockSpec((1, tn), hi),
                  pl.BlockSpec((M, tn), lo), pl.BlockSpec((M, tn), hi),
                  pl.BlockSpec((M, tn), lo), pl.BlockSpec((M, tn), hi)],
        out_specs=(ospec, ospec, ospec),
        scratch_shapes=[pltpu.VMEM((M, D), BF16)],
        compiler_params=_cp("arbitrary"),
        name="ffn_up_step",
    )(x, g, sc, sh, w_up, w_up, conv_w, conv_w, conv_b, conv_b, st0, st0, st1, st1)


def _in_proj_columns():
    a_cols = 512 + 128 + QK_ROPE
    b0 = a_cols
    sizes = (RW_W, DECAY_LORA, RW_W, RW_W, ICLR_LORA, GATE_LORA)
    offs = np.concatenate([[0], np.cumsum(sizes)])[:-1] + b0
    r0, wd0, k0, v0, ad0, gd0 = [int(o) for o in offs]
    b_cols = int(sum(sizes))
    c0 = b0 + b_cols
    pad = lambda n: [-1] * n
    rng = lambda s, n: list(range(s, s + n))
    cols = (rng(r0, RW_W) + rng(k0, RW_W) + rng(v0, RW_W)
            + rng(wd0, DECAY_LORA) + pad(LANE - DECAY_LORA)
            + rng(ad0, ICLR_LORA) + pad(LANE - ICLR_LORA)
            + rng(gd0, GATE_LORA)
            + rng(0, 512)
            + rng(c0, 512)
            + rng(512, 128)
            + rng(640, QK_ROPE) + pad(LANE - QK_ROPE))
    cols = np.asarray(cols, np.int32)
    assert cols.shape[0] == P_COLS
    pb_src = cols[:PB_W] - b0
    return cols, pb_src, b_cols


def _gather_cols(w, cols):
    cols = [int(c) for c in cols]
    runs, i = [], 0
    while i < len(cols):
        j = i + 1
        if cols[i] < 0:
            while j < len(cols) and cols[j] < 0:
                j += 1
            runs.append(jnp.zeros(w.shape[:-1] + (j - i,), w.dtype))
        else:
            while j < len(cols) and cols[j] == cols[j - 1] + 1:
                j += 1
            runs.append(w[..., cols[i]:cols[i] + (j - i)])
        i = j
    return jnp.concatenate(runs, axis=-1)


def _uq_columns():
    cols = []
    for h in range(MLA_HEADS):
        base = h * (QK_NOPE + QK_ROPE)
        cols += list(range(base, base + QK_NOPE + QK_ROPE)) + [-1] * (QK_PAD - QK_NOPE - QK_ROPE)
    return np.asarray(cols, np.int32)


def _pad_lanes(a, n):
    return jnp.pad(a, [(0, 0)] * (a.ndim - 1) + [(0, n - a.shape[-1])])


def _pad_rows(a, n):
    return jnp.pad(a, [(0, 0)] * (a.ndim - 2) + [(0, n - a.shape[-2]), (0, 0)])


def _rope_tables(pos):
    half = QK_ROPE // 2
    inv = jnp.power(ROPE_THETA, -jnp.arange(half, dtype=F32) * 2.0 / QK_ROPE)
    ang = pos.astype(F32)[:, None] * inv[None, :]
    cos, sin = jnp.cos(ang), jnp.sin(ang)
    z = jnp.zeros((pos.shape[0], LANE - QK_ROPE), F32)
    return jnp.concatenate([cos, cos, z], axis=1), jnp.concatenate([sin, sin, z], axis=1)


def kernel(x_prompt, x_sample, cache_latent, cache_krope, state_wkv, state_shift, state_pool, state_conv, page_table, c_prompt, c_sample, ada_w, ada_b, norm_mix_g, norm_ffn_g, w_in, q_lora_g, w_uq, kv_lora_g, w_uk, w_uv, qn_g, qr_g, kn_g, kr_g, rw_mu, rw_w0, rw_w_up, rw_a0, rw_a_up, rw_g_up, rw_k_k, rw_k_a, rw_r_k, rw_lnx_g, rw_lnx_b, w_pool, pool_scale, w_out, ffn_w_up, ffn_conv_w, ffn_conv_b, ffn_w_down):
    B, T, D = x_prompt.shape
    DB, TS, _ = x_sample.shape
    assert TS == 1, "the sample group decodes one token per sequence"
    L = ada_w.shape[0]
    n_pages = page_table.shape[1]
    past = n_pages * PAGE
    H, N = RW_HEADS, RW_N
    F = ffn_w_down.shape[1]

    cols, pb_src, b_cols = _in_proj_columns()
    pb_valid = np.nonzero(pb_src >= 0)[0]
    pb_inverse = pb_valid[np.argsort(pb_src[pb_valid])]

    w_in_p = _gather_cols(w_in, cols).astype(BF16)
    w_uq_p = _gather_cols(w_uq, _uq_columns()).astype(BF16)
    w_uk_f = w_uk.reshape(L, w_uk.shape[1], MLA_HEADS * QK_NOPE)
    w_uk_b = w_uk_f.astype(BF16)
    w_uk_t = jnp.swapaxes(w_uk_b, 1, 2)
    cache_krope_t = jnp.swapaxes(cache_krope, 2, 3)
    w_uv_b = w_uv.reshape(L, w_uv.shape[1], MLA_HEADS * V_HEAD).astype(BF16)
    w_out_b = w_out.astype(BF16)
    w_up_b = ffn_w_up.astype(BF16)
    w_down_b = ffn_w_down.astype(BF16)
    w_pool_b = w_pool.astype(BF16)
    mu_p = _gather_cols(rw_mu, pb_src)
    rw_w_up_p = _pad_rows(rw_w_up, LANE).astype(BF16)
    rw_a_up_p = _pad_rows(rw_a_up, LANE).astype(BF16)
    rw_g_up_b = rw_g_up.astype(BF16)
    qr_g_p = _pad_lanes(qr_g, LANE)
    kr_g_p = _pad_lanes(kr_g, LANE)
    shift_p = _gather_cols(state_shift, pb_src)
    row2 = lambda a, l: a[l].reshape(1, -1)

    mc = DB + B
    c_all = _pad_rows(jnp.concatenate([c_sample, c_prompt], axis=0), mc + (-mc) % 8)
    mod = _ada_mod(c_all, ada_w, ada_b)

    cos_p, sin_p = _rope_tables(jnp.arange(T, dtype=jnp.int32))
    cos_s, sin_s = _rope_tables(jnp.full((DB,), past, jnp.int32))

    state_wkv_flat = state_wkv.reshape(L, DB, H * N * N)

    xp = x_prompt.reshape(B * T, D)
    xs = x_sample.reshape(DB, D)
    outs = [[] for _ in range(12)]

    for l in range(L):
        lw = dict(q_lora_g=row2(q_lora_g, l), w_uq=w_uq_p[l], kv_lora_g=row2(kv_lora_g, l), qn_g=row2(qn_g, l),
                  qr_g=qr_g_p[l].reshape(1, -1), kn_g=row2(kn_g, l), kr_g=kr_g_p[l].reshape(1, -1),
                  w_uk=w_uk_b[l], w_uv=w_uv_b[l],
                  rw_mu=mu_p[l].reshape(1, -1), rw_w0=row2(rw_w0, l), rw_w_up=rw_w_up_p[l], rw_a0=row2(rw_a0, l),
                  rw_a_up=rw_a_up_p[l], rw_g_up=rw_g_up_b[l], rw_k_k=row2(rw_k_k, l), rw_k_a=row2(rw_k_a, l),
                  rw_r_k=row2(rw_r_k, l), rw_lnx_g=row2(rw_lnx_g, l), rw_lnx_b=row2(rw_lnx_b, l))
        g_mix, g_ffn = row2(norm_mix_g, l), row2(norm_ffn_g, l)
        ps = row2(pool_scale, l)
        cw, cb = ffn_conv_w[l], row2(ffn_conv_b, l)
        mods_p = [mod[l, DB:DB + B, k * D:(k + 1) * D][:, None, :] for k in range(6)]
        mods_s = [mod[l, :DB, k * D:(k + 1) * D][None] for k in range(6)]

        sh1, sc1, g1, sh2, sc2, g2 = mods_p
        p = _norm_mod_matmul(xp, g_mix, sc1, sh1, w_in_p, l, T)
        q, k, v, lat, kr = _mla_proj(p, cos_p, sin_p, lw, T, BF16)
        ya = _attn_prompt(q, k, v, B, T)
        r_, wl_, k_, v_, kk_, b_, g_ = _rw_prep(p, None, lw, T)
        y_raw, st_t = _rw_chunked((r_, wl_, k_, v_, kk_, b_), B, T)
        yb = _rw_post(y_raw, r_, k_, v_, g_, lw)
        yc = _pool_prompt(p, w_pool_b[l], ps, T)
        xp = _matmul_gate_res([ya, yb, yc], w_out_b, l, g1, xp, T)
        act, tail_v, tail_g = _ffn_up_prompt(xp, g_ffn, sc2, sh2, w_up_b, l, cw, cb, B, T)
        xp = _matmul_gate_res([act], w_down_b, l, g2, xp, T)

        p3 = p.reshape(B, T, P_COLS)
        outs[0].append(lat.reshape(B, T, LANE))
        outs[2].append(kr.reshape(B, T, LANE)[:, :, :QK_ROPE])
        outs[4].append(jnp.swapaxes(st_t, -1, -2))
        outs[6].append(_gather_cols(p3[:, -1, :PB_W], pb_inverse))
        pool_rows = p3[:, max(T - POOL_BUF, 0):, PC_OFF:PC_OFF + 512]
        if T < POOL_BUF:
            pool_rows = jnp.concatenate([jnp.zeros((B, POOL_BUF - T, 512), F32), pool_rows], axis=1)
        outs[8].append(pool_rows)
        last = lambda t: t.reshape(B, -1, 8, F)[:, -1, 6:8]
        outs[10].append(jnp.concatenate([last(tail_v), last(tail_g)], axis=-1))

        sh1, sc1, g1, sh2, sc2, g2 = mods_s
        p = _norm_mod_matmul(xs, g_mix, sc1, sh1, w_in_p, l, DB)
        q, _, _, lat, kr = _mla_proj(p, cos_s, sin_s, lw, DB, F32)
        qt, qrp = _absorb_q(q, lw["kn_g"], w_uk_f[l])
        o_lat = _paged_attention(page_table, jnp.swapaxes(qt, 0, 1), jnp.swapaxes(qrp, 0, 1), lat, kr, w_uk_t[l],
                                 cache_latent, cache_krope_t, l)
        ya = _uv_out(jnp.swapaxes(o_lat, 0, 1), lw["w_uv"])
        r_, wl_, k_, v_, kk_, b_, g_ = _rw_prep(p, shift_p[l], lw, DB)
        heads = lambda a: jnp.swapaxes(a.reshape(DB, H, N), 0, 1)
        s_new, y_h = _rw_step(state_wkv_flat, l, [heads(a) for a in (r_, wl_, k_, v_, kk_, b_)])
        y_raw = jnp.swapaxes(y_h, 0, 1).reshape(DB, H * N)
        yb = _rw_post(y_raw, r_, k_, v_, g_, lw)
        yc = _pool_step(jnp.swapaxes(state_pool[l], 0, 1), p, w_pool_b[l], ps, past)
        xs = _matmul_gate_res([ya, yb, yc], w_out_b, l, g1, xs, DB)
        act, u_v, u_g = _ffn_up_step(xs, g_ffn, sc2, sh2, w_up_b, l, cw, cb, state_conv[l, :, 0], state_conv[l, :, 1])
        xs = _matmul_gate_res([act], w_down_b, l, g2, xs, DB)

        outs[1].append(lat.reshape(DB, 1, LANE))
        outs[3].append(kr.reshape(DB, 1, LANE)[:, :, :QK_ROPE])
        outs[5].append(s_new.reshape(DB, H, N, N))
        outs[7].append(_gather_cols(p[:, :PB_W], pb_inverse))
        outs[9].append(jnp.concatenate([state_pool[l][:, 1:], p[:, None, PC_OFF:PC_OFF + 512]], axis=1))
        outs[11].append(jnp.stack([state_conv[l, :, 1], jnp.concatenate([u_v, u_g], axis=-1)], axis=1))

    stacked = [jnp.stack(o) for o in outs]
    return (xp.reshape(B, T, D), xs.reshape(DB, 1, D), *stacked)
```

```python
import functools

import numpy as np
import jax
import jax.numpy as jnp
from jax import lax
from jax.experimental import pallas as pl
from jax.experimental.pallas import tpu as pltpu

F32 = jnp.float32
BF16 = jnp.bfloat16

MLA_HEADS = 8
QK_NOPE = 128
QK_ROPE = 64
V_HEAD = 128
ROPE_THETA = 10000.0
MLA_SCALE = (QK_NOPE + QK_ROPE) ** -0.5
RW_HEADS = 8
RW_N = 64
DECAY_LORA = 96
ICLR_LORA = 96
GATE_LORA = 256
GN_EPS = 64e-5
NORM_EPS = 1e-6
POOL_WINDOWS = (2, 4, 8, 16)
POOL_BUF = 15
PAGE = 128

LANE = 128
QK_PAD = 256
VMEM_LIMIT = 48 * 1024 * 1024

RW_W = RW_HEADS * RW_N
PB_W = 2048
CQ_OFF = PB_W
PC_OFF = CQ_OFF + 512
CKV_OFF = PC_OFF + 512
KR_OFF = CKV_OFF + 128
P_COLS = KR_OFF + 128

NN = (((1,), (0,)), ((), ()))
NT = (((1,), (1,)), ((), ()))
TN = (((0,), (0,)), ((), ()))


def _cp(*sem):
    return pltpu.CompilerParams(dimension_semantics=sem, vmem_limit_bytes=VMEM_LIMIT)


def _pallas(body, **kw):
    call = pl.pallas_call(body, **kw)

    def run(*args):
        if any(isinstance(a, jax.core.Tracer) for a in args):
            args = [a if a.dtype == jnp.int32 else pltpu.with_memory_space_constraint(a, pltpu.HBM) for a in args]
        return call(*args)
    return run


def _tile(n, pref):
    if n <= pref:
        return n
    t = pref - pref % 8
    while t >= 8:
        if n % t == 0:
            return t
        t -= 8
    return n


def _dot(a, b, dims=NN):
    return lax.dot_general(a, b, dims, preferred_element_type=F32)


def _split2(x):
    hi = x.astype(BF16)
    lo = (x - hi.astype(F32)).astype(BF16)
    return hi, lo


def _split3(x):
    hi = x.astype(BF16)
    r1 = x - hi.astype(F32)
    mid = r1.astype(BF16)
    lo = (r1 - mid.astype(F32)).astype(BF16)
    return hi, mid, lo


def _dot3(a, b, dims=NN):
    (ca,), (cb,) = dims[0]
    ah = a.astype(BF16).astype(F32)
    bh = b.astype(BF16).astype(F32)
    a3 = jnp.concatenate([ah, a - ah, ah], axis=ca).astype(BF16)
    b3 = jnp.concatenate([bh, bh, b - bh], axis=cb).astype(BF16)
    return _dot(a3, b3, dims)


def _dot_sel_l(sel, x, dims=NN):
    h, m, l = _split3(x)
    return _dot(sel, h, dims) + (_dot(sel, m, dims) + _dot(sel, l, dims))


def _dot_sel_r(x, sel, dims=NN):
    h, l = _split2(x)
    return _dot(h, sel, dims) + _dot(l, sel, dims)


def _sigmoid(x):
    return 1.0 / (1.0 + jnp.exp(-x))


def _rms(x, g, n):
    return (x * lax.rsqrt(jnp.sum(x * x, axis=-1, keepdims=True) / n + NORM_EPS)) * g


def _modnorm(x, g, sc, sh):
    return _rms(x, g, x.shape[-1]) * (1.0 + sc) + sh


def _mod_spec(tm, tn, tpg, rows, col_arg):
    if rows == 1:
        if col_arg:
            return pl.BlockSpec((1, 1, tn), lambda i, j: (i // tpg, 0, j))
        return pl.BlockSpec((1, 1, tn), lambda i, j: (i // tpg, 0, 0))
    if col_arg:
        return pl.BlockSpec((1, tm, tn), lambda i, j: (i // tpg, i % tpg, j))
    return pl.BlockSpec((1, tm, tn), lambda i, j: (i // tpg, i % tpg, 0))


def _ada_kernel(c_ref, w_ref, b_ref, o_ref):
    c = c_ref[...]
    s = c * _sigmoid(c)
    o_ref[0] = _dot(s.astype(BF16), w_ref[0].astype(BF16)) + b_ref[0]


def _ada_mod(c_all, ada_w, ada_b):
    L, D, N = ada_w.shape
    mc = c_all.shape[0]
    tn = _tile(N, 1024)
    return _pallas(
        _ada_kernel,
        out_shape=jax.ShapeDtypeStruct((L, mc, N), F32),
        grid=(L, N // tn),
        in_specs=[pl.BlockSpec((mc, D), lambda l, j: (0, 0)),
                  pl.BlockSpec((1, D, tn), lambda l, j: (l, 0, j)),
                  pl.BlockSpec((1, 1, tn), lambda l, j: (l, 0, j))],
        out_specs=pl.BlockSpec((1, mc, tn), lambda l, j: (l, 0, j)),
        compiler_params=_cp("parallel", "parallel"),
        name="ada_mod",
    )(c_all, ada_w, ada_b.reshape(L, 1, N))


def _nmm_kernel(x_ref, g_ref, sc_ref, sh_ref, w_ref, o_ref, h_ref):
    @pl.when(pl.program_id(1) == 0)
    def _():
        h_ref[...] = _modnorm(x_ref[...], g_ref[...], sc_ref[0], sh_ref[0]).astype(BF16)

    o_ref[...] = _dot(h_ref[...], w_ref[...])


def _norm_mod_matmul(x, g, sc, sh, w, layer, rpg, tm_pref=512, tn_pref=1664):
    M, D = x.shape
    N = w.shape[2]
    tm = _tile(rpg, tm_pref)
    tpg = rpg // tm
    tn = max(t for t in range(LANE, tn_pref + 1, LANE) if N % t == 0)
    rows = sc.shape[1]
    return _pallas(
        _nmm_kernel,
        out_shape=jax.ShapeDtypeStruct((M, N), F32),
        grid=(M // tm, N // tn),
        in_specs=[pl.BlockSpec((tm, D), lambda i, j: (i, 0)),
                  pl.BlockSpec((1, D), lambda i, j: (0, 0)),
                  _mod_spec(tm, D, tpg, rows, False),
                  _mod_spec(tm, D, tpg, rows, False),
                  pl.BlockSpec((None, D, tn), lambda i, j: (layer, 0, j))],
        out_specs=pl.BlockSpec((tm, tn), lambda i, j: (i, j)),
        scratch_shapes=[pltpu.VMEM((tm, D), BF16)],
        compiler_params=_cp("parallel", "arbitrary"),
        name="norm_mod_matmul",
    )(x, g, sc, sh, w)


def _mmres_kernel(*refs, widths):
    n = len(widths)
    a_refs, (w_ref, gate_ref, res_ref, o_ref) = refs[:n], refs[n:]
    acc, k0 = None, 0
    for a_ref, k in zip(a_refs, widths):
        part = _dot(a_ref[...], w_ref[k0:k0 + k, :])
        acc = part if acc is None else acc + part
        k0 += k
    o_ref[...] = res_ref[...] + gate_ref[0] * acc


def _matmul_gate_res(parts, w, layer, gate, res, rpg, tm_pref=1024, tn_pref=512):
    M = parts[0].shape[0]
    widths = tuple(a.shape[1] for a in parts)
    K, N = w.shape[1:]
    assert sum(widths) == K
    tm = _tile(rpg, tm_pref)
    tpg = rpg // tm
    tn = _tile(N, tn_pref)
    rows = gate.shape[1]
    return _pallas(
        functools.partial(_mmres_kernel, widths=widths),
        out_shape=jax.ShapeDtypeStruct((M, N), F32),
        grid=(M // tm, N // tn),
        in_specs=[pl.BlockSpec((tm, k), lambda i, j: (i, 0)) for k in widths]
        + [pl.BlockSpec((None, K, tn), lambda i, j: (layer, 0, j)),
           _mod_spec(tm, tn, tpg, rows, True),
           pl.BlockSpec((tm, tn), lambda i, j: (i, j))],
        out_specs=pl.BlockSpec((tm, tn), lambda i, j: (i, j)),
        compiler_params=_cp("parallel", "arbitrary"),
        name="matmul_gate_res",
    )(*parts, w, gate, res)


def _mla_proj_kernel(cq_ref, ckv_ref, kr_ref, cos_ref, sin_ref, qlg_ref, wuq_ref, kvg_ref, qng_ref, qrg_ref,
                     kng_ref, krg_ref, wuk_ref, wuv_ref, q_ref, k_ref, v_ref, lat_ref, kro_ref):
    cos = cos_ref[...]
    sin = sin_ref[...]
    lane = lax.broadcasted_iota(jnp.int32, cos.shape, 1)
    half = QK_ROPE // 2

    def rope(x):
        rot = jnp.where(lane < half, -pltpu.roll(x, LANE - half, 1), pltpu.roll(x, half, 1))
        return x * cos + rot * sin

    cqn = _rms(cq_ref[...], qlg_ref[...], cq_ref.shape[-1])
    q = _dot(cqn.astype(BF16), wuq_ref[...])
    lat = _rms(ckv_ref[...], kvg_ref[...], ckv_ref.shape[-1])
    lat_ref[...] = lat
    kr = rope(_rms(kr_ref[...], krg_ref[...], QK_ROPE))
    kro_ref[...] = kr
    latb = lat.astype(BF16)
    kx = _dot(latb, wuk_ref[...])
    v_ref[...] = _dot(latb, wuv_ref[...]).astype(v_ref.dtype)
    krb = kr.astype(k_ref.dtype)
    for h in range(MLA_HEADS):
        o = h * QK_PAD
        qn = _rms(q[:, o:o + QK_NOPE], qng_ref[...], QK_NOPE)
        qr = rope(_rms(q[:, o + QK_NOPE:o + QK_PAD], qrg_ref[...], QK_ROPE))
        q_ref[:, o:o + QK_NOPE] = qn.astype(q_ref.dtype)
        q_ref[:, o + QK_NOPE:o + QK_PAD] = qr.astype(q_ref.dtype)
        kn = _rms(kx[:, h * QK_NOPE:(h + 1) * QK_NOPE], kng_ref[...], QK_NOPE)
        k_ref[:, o:o + QK_NOPE] = kn.astype(k_ref.dtype)
        k_ref[:, o + QK_NOPE:o + QK_PAD] = krb


def _mla_proj(p, cos_t, sin_t, lw, rpg, q_dtype, tm_pref=256):
    M = p.shape[0]
    tm = _tile(rpg, tm_pref)
    tpg = rpg // tm
    H = MLA_HEADS
    row = lambda w: pl.BlockSpec((tm, w), lambda i: (i, 0))
    const = lambda a: pl.BlockSpec(a.shape, lambda i: (0,) * a.ndim)
    tab = pl.BlockSpec((tm, LANE), lambda i: (i % tpg, 0))
    weights = (lw["q_lora_g"], lw["w_uq"], lw["kv_lora_g"], lw["qn_g"], lw["qr_g"], lw["kn_g"], lw["kr_g"],
               lw["w_uk"], lw["w_uv"])
    return _pallas(
        _mla_proj_kernel,
        out_shape=(jax.ShapeDtypeStruct((M, H * QK_PAD), q_dtype),
                   jax.ShapeDtypeStruct((M, H * QK_PAD), BF16),
                   jax.ShapeDtypeStruct((M, H * V_HEAD), BF16),
                   jax.ShapeDtypeStruct((M, LANE), F32),
                   jax.ShapeDtypeStruct((M, LANE), F32)),
        grid=(M // tm,),
        in_specs=[pl.BlockSpec((tm, 512), lambda i: (i, CQ_OFF // 512)),
                  pl.BlockSpec((tm, LANE), lambda i: (i, CKV_OFF // LANE)),
                  pl.BlockSpec((tm, LANE), lambda i: (i, KR_OFF // LANE)),
                  tab, tab] + [const(a) for a in weights],
        out_specs=(row(H * QK_PAD), row(H * QK_PAD), row(H * V_HEAD), row(LANE), row(LANE)),
        compiler_params=_cp("parallel"),
        name="mla_proj",
    )(p, p, p, cos_t, sin_t, *weights)


ATTN_HEADS_PER_STEP = 4


def _attn_kernel(q_ref, k_ref, v_ref, o_ref, *, tq):
    qi = pl.program_id(2)
    hs = range(ATTN_HEADS_PER_STEP)
    q = [q_ref[:, h * QK_PAD:(h + 1) * QK_PAD] for h in hs]

    def block(kj, carry, causal):
        start = pl.multiple_of(kj * tq, tq)
        s = [_dot(q[h], k_ref[pl.ds(start, tq), h * QK_PAD:(h + 1) * QK_PAD], NT) * MLA_SCALE for h in hs]
        if causal:
            row = lax.broadcasted_iota(jnp.int32, (tq, tq), 0)
            col = lax.broadcasted_iota(jnp.int32, (tq, tq), 1)
            s = [jnp.where(col <= row, s[h], -jnp.inf) for h in hs]
        out = []
        for h in hs:
            m, l, acc = carry[h]
            m_new = jnp.maximum(m, jnp.max(s[h], axis=-1, keepdims=True))
            alpha = jnp.exp(m - m_new)
            e = jnp.exp(s[h] - m_new)
            l = alpha * l + jnp.sum(e, axis=-1, keepdims=True)
            v = v_ref[pl.ds(start, tq), h * V_HEAD:(h + 1) * V_HEAD]
            out.append((m_new, l, alpha * acc + _dot(e.astype(BF16), v)))
        return tuple(out)

    init = tuple((jnp.full((tq, 1), -jnp.inf, F32), jnp.zeros((tq, 1), F32), jnp.zeros((tq, V_HEAD), F32))
                 for _ in hs)
    carry = block(qi, init, True)
    carry = lax.fori_loop(0, qi, lambda kj, c: block(kj, c, False), carry)
    for h in hs:
        m, l, acc = carry[h]
        o_ref[:, h * V_HEAD:(h + 1) * V_HEAD] = (acc / l).astype(o_ref.dtype)


def _attn_prompt(q, k, v, B, T, tq_pref=512):
    HP = MLA_HEADS // ATTN_HEADS_PER_STEP
    qw, vw = ATTN_HEADS_PER_STEP * QK_PAD, ATTN_HEADS_PER_STEP * V_HEAD
    tq = _tile(T, tq_pref)
    nq = T // tq
    return _pallas(
        functools.partial(_attn_kernel, tq=tq),
        out_shape=jax.ShapeDtypeStruct((B * T, MLA_HEADS * V_HEAD), BF16),
        grid=(B, HP, nq),
        in_specs=[pl.BlockSpec((tq, qw), lambda b, h, i: (b * nq + i, h)),
                  pl.BlockSpec((T, qw), lambda b, h, i: (b, h)),
                  pl.BlockSpec((T, vw), lambda b, h, i: (b, h))],
        out_specs=pl.BlockSpec((tq, vw), lambda b, h, i: (b * nq + i, h)),
        compiler_params=_cp("parallel", "parallel", "arbitrary"),
        name="attn_prompt",
    )(q, k, v)


def _absorb_kernel(q_ref, kng_ref, wuk_ref, qt_ref, qr_ref):
    g = kng_ref[...]
    for h in range(MLA_HEADS):
        o = h * QK_PAD
        qn = q_ref[:, o:o + QK_NOPE] * g
        wh = wuk_ref[:, h * QK_NOPE:(h + 1) * QK_NOPE]
        qt_ref[h] = _dot3(qn, wh, NT)
        qr_ref[h] = q_ref[:, o + QK_NOPE:o + QK_PAD]


def _absorb_q(q, kn_g, w_uk):
    M = q.shape[0]
    H = MLA_HEADS
    return _pallas(
        _absorb_kernel,
        out_shape=(jax.ShapeDtypeStruct((H, M, LANE), F32), jax.ShapeDtypeStruct((H, M, LANE), F32)),
        compiler_params=pltpu.CompilerParams(vmem_limit_bytes=VMEM_LIMIT),
        name="absorb_q",
    )(q, kn_g, w_uk)


def _paged_kernel(pt_ref, qt_ref, qr_ref, latn_ref, krn_ref, wukt_ref, lat_hbm, kr_hbm, o_ref,
                  latbuf, krbuf, lhs_ref, s_ref, sem, *, layer, n_pages, chunk_pages):
    b = pl.program_id(0)
    nb = pl.num_programs(0)
    slot = b % 2
    H = MLA_HEADS
    HD = H * QK_NOPE
    n_chunks = n_pages // chunk_pages
    rows = chunk_pages * PAGE
    R = latbuf.shape[-1]

    def page_copies(seq, sl, c, j):
        p = c * chunk_pages + j
        pg = pt_ref[seq * n_pages + p]
        return (pltpu.make_async_copy(lat_hbm.at[layer, pg], latbuf.at[sl, p], sem.at[sl, 0, c]),
                pltpu.make_async_copy(kr_hbm.at[layer, pg],
                                      krbuf.at[sl, :, pl.ds(pl.multiple_of(p * PAGE, PAGE), PAGE)], sem.at[sl, 1, c]))

    def fetch_chunk(seq, sl, c):
        for j in range(chunk_pages):
            for cp in page_copies(seq, sl, c, j):
                cp.start()

    def wait_chunk(seq, sl, c):
        for j in range(chunk_pages):
            for cp in page_copies(seq, sl, c, j):
                cp.wait()

    @pl.when(b == 0)
    def _():
        def body(c, carry):
            fetch_chunk(0, 0, c)
            return carry
        lax.fori_loop(0, n_chunks, body, 0)

    lhs_ref[pl.ds(0, HD), :] = wukt_ref[...]
    lhs_ref[pl.ds(HD, 16), :] = jnp.concatenate([qt_ref[0], jnp.zeros_like(qt_ref[0])], axis=0).astype(BF16)
    lhs = lhs_ref[...]
    qr = qr_ref[0][:, :QK_ROPE].astype(BF16)

    def latent_scores(latb):
        xt = _dot(lhs, latb, NT)
        ms = []
        for h in range(H):
            acc = None
            for i in range(QK_NOPE // 8):
                blk = xt[h * QK_NOPE + 8 * i:h * QK_NOPE + 8 * i + 8]
                acc = blk * blk if acc is None else acc + blk * blk
            ms.append(jnp.sum(acc, axis=0, keepdims=True))
        ms = jnp.concatenate(ms, axis=0)
        return xt[HD:HD + H] * lax.rsqrt(ms / QK_NOPE + NORM_EPS)

    nxt = jnp.minimum(b + 1, nb - 1)

    def wait_body(c, carry):
        wait_chunk(b, slot, c)
        return carry
    lax.fori_loop(0, n_chunks, wait_body, 0)

    def score_chunk(c, carry):
        fetch_chunk(nxt, 1 - slot, c)

        p0 = pl.multiple_of(c * chunk_pages, chunk_pages)
        k0 = pl.multiple_of(c * rows, rows)
        s_rope = _dot(qr, krbuf[slot, :, pl.ds(k0, rows)].astype(BF16))
        latb = latbuf[slot, pl.ds(p0, chunk_pages)].reshape(rows, R).astype(BF16)
        s_ref[:, pl.ds(k0, rows)] = (latent_scores(latb) + s_rope) * MLA_SCALE
        return carry

    unroll = 4 if n_chunks % 4 == 0 else 1
    lax.fori_loop(0, n_chunks, score_chunk, 0, unroll=unroll)

    latn = latn_ref[0]
    latnb = jnp.concatenate([latn, jnp.zeros((PAGE - latn.shape[0], R), F32)], axis=0).astype(BF16)
    krn = krn_ref[0][0:1, :QK_ROPE].astype(BF16).astype(F32)
    s_rope = jnp.sum(qr.astype(F32) * krn, axis=-1, keepdims=True)
    s_new = (latent_scores(latnb) + s_rope) * MLA_SCALE
    s_new = jnp.where(lax.broadcasted_iota(jnp.int32, s_new.shape, 1) == 0, s_new, -jnp.inf)

    s_all = s_ref[...]
    m = jnp.maximum(jnp.max(s_all, axis=-1, keepdims=True), jnp.max(s_new, axis=-1, keepdims=True))
    e_new = jnp.exp(s_new - m)
    s_ref[...] = jnp.exp(s_all - m)
    l = jnp.sum(s_ref[...], axis=-1, keepdims=True) + jnp.sum(e_new, axis=-1, keepdims=True)

    def value_chunk(c, acc):
        p0 = pl.multiple_of(c * chunk_pages, chunk_pages)
        k0 = pl.multiple_of(c * rows, rows)
        latb = latbuf[slot, pl.ds(p0, chunk_pages)].reshape(rows, R).astype(BF16)
        return acc + _dot(s_ref[:, pl.ds(k0, rows)].astype(BF16), latb)

    acc = lax.fori_loop(0, n_chunks, value_chunk, _dot(e_new.astype(BF16), latnb), unroll=unroll)
    o_ref[0] = acc / l

    @pl.when(b == nb - 1)
    def _():
        def body(c, carry):
            wait_chunk(nxt, 1 - slot, c)
            return carry
        lax.fori_loop(0, n_chunks, body, 0)


def _paged_attention(page_table, qt, qr, lat_new, kr_new, w_uk_t, cache_latent, cache_krope_t, layer):
    DB, n_pages = page_table.shape
    H = MLA_HEADS
    R = cache_latent.shape[-1]
    chunk_pages = 8 if n_pages % 8 == 0 else 1
    pad8 = lambda a: jnp.pad(a[:, None, :], ((0, 0), (0, 7), (0, 0)))
    seq = lambda w: pl.BlockSpec((1, 8, w), lambda b, pt: (b, 0, 0))
    grid_spec = pltpu.PrefetchScalarGridSpec(
        num_scalar_prefetch=1,
        grid=(DB,),
        in_specs=[seq(LANE), seq(LANE), seq(LANE), seq(LANE),
                  pl.BlockSpec(w_uk_t.shape, lambda b, pt: (0, 0)),
                  pl.BlockSpec(memory_space=pl.ANY),
                  pl.BlockSpec(memory_space=pl.ANY)],
        out_specs=pl.BlockSpec((1, H, R), lambda b, pt: (b, 0, 0)),
        scratch_shapes=[pltpu.VMEM((2, n_pages, PAGE, R), F32),
                        pltpu.VMEM((2, QK_ROPE, n_pages * PAGE), F32),
                        pltpu.VMEM((w_uk_t.shape[0] + 16, R), BF16),
                        pltpu.VMEM((H, n_pages * PAGE), F32),
                        pltpu.SemaphoreType.DMA((2, 2, n_pages // chunk_pages))],
    )
    return _pallas(
        functools.partial(_paged_kernel, layer=layer, n_pages=n_pages, chunk_pages=chunk_pages),
        out_shape=jax.ShapeDtypeStruct((DB, H, R), F32),
        grid_spec=grid_spec,
        compiler_params=_cp("arbitrary"),
        name="paged_attention",
    )(page_table.reshape(-1), qt, qr, pad8(lat_new), pad8(kr_new), w_uk_t, cache_latent, cache_krope_t)


def _uv_kernel(o_ref, wuv_ref, y_ref):
    for h in range(MLA_HEADS):
        y_ref[:, h * V_HEAD:(h + 1) * V_HEAD] = _dot(
            o_ref[h].astype(BF16), wuv_ref[:, h * V_HEAD:(h + 1) * V_HEAD]).astype(y_ref.dtype)


def _uv_out(o_lat, w_uv):
    H, M, _ = o_lat.shape
    return _pallas(
        _uv_kernel,
        out_shape=jax.ShapeDtypeStruct((M, H * V_HEAD), BF16),
        compiler_params=pltpu.CompilerParams(vmem_limit_bytes=VMEM_LIMIT),
        name="uv_out",
    )(o_lat, w_uv)


def _rw_prep_body(pb, prev, mu_ref, w0_ref, wup_ref, a0_ref, aup_ref, gup_ref, kk_ref, ka_ref, seg_ref,
                  r_o, wl_o, k_o, v_o, kk_o, b_o, g_o):
    W = RW_W
    xs = pb + (prev - pb) * mu_ref[...]
    r = xs[:, 0:W]
    k = xs[:, W:2 * W]
    v = xs[:, 2 * W:3 * W]
    wd = xs[:, 3 * W:3 * W + LANE]
    ad = xs[:, 3 * W + LANE:3 * W + 2 * LANE]
    gd = xs[:, 3 * W + 2 * LANE:]
    z = -(w0_ref[...] + _dot(jnp.tanh(wd).astype(BF16), wup_ref[...]))
    softplus = jnp.maximum(z, 0.0) + jnp.log1p(jnp.exp(-jnp.abs(z)))
    w = -softplus - 0.5
    a = _sigmoid(a0_ref[...] + _dot(ad.astype(BF16), aup_ref[...]))
    g = _dot(_sigmoid(gd).astype(BF16), gup_ref[...])
    kk = k * kk_ref[...]
    k2 = k * (1.0 + (a - 1.0) * ka_ref[...])
    n2 = _dot_sel_r(kk * kk, seg_ref[...])
    kkn = kk / jnp.maximum(jnp.sqrt(n2), 1e-12)
    r_o[...] = r
    wl_o[...] = -jnp.exp(w)
    k_o[...] = k2
    v_o[...] = v
    kk_o[...] = kkn
    b_o[...] = kkn * a
    g_o[...] = g


def _rw_prep_seq_kernel(pb_ref, halo_ref, *rest, tpg):
    pb = pb_ref[...]
    i = pl.program_id(0)
    rolled = pltpu.roll(pb, 1, 0)
    row = lax.broadcasted_iota(jnp.int32, pb.shape, 0)
    first_row = jnp.where(i % tpg == 0, 0, -1)
    prev = jnp.where(row == 0, halo_ref[7:8, :], rolled)
    prev = jnp.where(row == first_row, 0.0, prev)
    _rw_prep_body(pb, prev, *rest)


def _rw_prep_step_kernel(pb_ref, prev_ref, *rest):
    _rw_prep_body(pb_ref[...], prev_ref[...], *rest)


def _seg_matrix(n, w):
    return jnp.asarray(np.kron(np.eye(n // w, dtype=np.float32), np.ones((w, w), np.float32)), BF16)


def _rw_prep(p, prev, lw, rpg, tm_pref=512):
    M = p.shape[0]
    tm = _tile(rpg, tm_pref)
    tpg = rpg // tm
    seg = _seg_matrix(RW_W, RW_N)
    weights = (lw["rw_mu"], lw["rw_w0"], lw["rw_w_up"], lw["rw_a0"], lw["rw_a_up"], lw["rw_g_up"], lw["rw_k_k"],
               lw["rw_k_a"], seg)
    const = lambda a: pl.BlockSpec(a.shape, lambda i: (0,) * a.ndim)
    out = jax.ShapeDtypeStruct((M, RW_W), F32)
    ospec = pl.BlockSpec((tm, RW_W), lambda i: (i, 0))
    main = pl.BlockSpec((tm, PB_W), lambda i: (i, 0))
    if prev is None:
        kern = functools.partial(_rw_prep_seq_kernel, tpg=tpg)
        second = pl.BlockSpec((8, PB_W), lambda i: (jnp.maximum(i * (tm // 8) - 1, 0), 0))
        second_arg = p
    else:
        kern = _rw_prep_step_kernel
        second = pl.BlockSpec((tm, PB_W), lambda i: (i, 0))
        second_arg = prev
    return _pallas(
        kern,
        out_shape=(out,) * 7,
        grid=(M // tm,),
        in_specs=[main, second] + [const(a) for a in weights],
        out_specs=(ospec,) * 7,
        compiler_params=_cp("parallel"),
        name="rwkv_prep",
    )(p, second_arg, *weights)


def _rw_chunk_kernel(r_ref, wl_ref, k_ref, v_ref, kk_ref, b_ref, y_ref, s_ref, st_ref, *, C, G):
    step = pl.program_id(1)

    @pl.when(step == 0)
    def _():
        st_ref[...] = jnp.zeros_like(st_ref)

    N = RW_N
    ri = lax.broadcasted_iota(jnp.int32, (C, C), 0)
    ci = lax.broadcasted_iota(jnp.int32, (C, C), 1)
    incl = ci <= ri
    strict = ci < ri
    eye_c = (ci == ri).astype(F32)
    rn = lax.broadcasted_iota(jnp.int32, (N, N), 0)
    cn = lax.broadcasted_iota(jnp.int32, (N, N), 1)
    eye_n = rn == cn
    rg = lax.broadcasted_iota(jnp.int32, (G * C, G * C), 0)
    cg = lax.broadcasted_iota(jnp.int32, (G * C, G * C), 1)
    same_chunk_incl = (cg <= rg) & (rg // C == cg // C)

    wl = wl_ref[...]
    cs = _dot_sel_l(same_chunk_incl.astype(BF16), wl)
    p_in = jnp.exp(cs)
    p_ex = jnp.exp(cs - wl)
    p_inv = jnp.exp(-cs)
    at = -kk_ref[...] * p_ex
    rp = r_ref[...] * p_in
    bp = b_ref[...] * p_inv
    kp = k_ref[...] * p_inv
    v = v_ref[...]

    items = [(g, h) for g in range(G) for h in range(RW_HEADS)]
    cut = lambda x: [x[g * C:(g + 1) * C, h * N:(h + 1) * N] for g, h in items]
    p_end = [p_in[(g + 1) * C - 1:(g + 1) * C, h * N:(h + 1) * N] for g, h in items]
    at_i, rp_i, bp_i, kp_i, v_i = (cut(x) for x in (at, rp, bp, kp, v))
    n = range(len(items))
    lhs = [jnp.concatenate([at_i[i], rp_i[i]], axis=0) for i in n]
    gb = [_dot3(lhs[i], bp_i[i], NT) for i in n]
    gk = [_dot3(lhs[i], kp_i[i], NT) for i in n]
    a_ab = [jnp.where(strict, gb[i][:C], 0.0) for i in n]
    a_rb = [jnp.where(incl, gb[i][C:], 0.0) for i in n]
    a_ak = [jnp.where(strict, gk[i][:C], 0.0) for i in n]
    a_rk = [jnp.where(incl, gk[i][C:], 0.0) for i in n]
    arkv = [_dot3(jnp.concatenate([a_ak[i], a_rk[i]], axis=0), v_i[i]) for i in n]
    akv = [x[:C] for x in arkv]
    rkv = [x[C:] for x in arkv]
    kv = [_dot3(kp_i[i] * p_end[i], v_i[i], TN) for i in n]
    t = None
    m = 1
    while m < C:
        sh = m.bit_length() - 1
        low_left = ((ri >> (sh + 1)) == (ci >> (sh + 1))) & (((ri >> sh) & 1) == 1) & (((ci >> sh) & 1) == 0)
        a_off = [jnp.where(low_left, a_ab[i], 0.0) for i in n]
        if m == 1:
            t = [eye_c + a_off[i] for i in n]
        else:
            inner = [_dot3(a_off[i], t[i]) for i in n]
            t = [t[i] + _dot3(t[i], inner[i]) for i in n]
        m *= 2
    w = [_dot3(t[i], jnp.concatenate([at_i[i], akv[i]], axis=1)) for i in n]
    gg = [_dot3(a_rb[i], w[i]) for i in n]
    mm = [_dot3(bp_i[i] * p_end[i], w[i], TN) for i in n]
    top = [jnp.concatenate([rp_i[i] + gg[i][:, :N], jnp.where(eye_n, p_end[i], 0.0) + mm[i][:, :N]], axis=0)
           for i in n]
    st = [st_ref[h] for h in range(RW_HEADS)]
    for g in range(G):
        idx = [g * RW_HEADS + h for h in range(RW_HEADS)]
        res = [_dot3(top[i], st[h]) for h, i in enumerate(idx)]
        st = [res[h][C:] + (mm[i][:, N:] + kv[i]) for h, i in enumerate(idx)]
        y_ref[pl.ds(g * C, C), :] = jnp.concatenate(
            [res[h][:C] + (gg[i][:, N:] + rkv[i]) for h, i in enumerate(idx)], axis=1)
    for h in range(RW_HEADS):
        st_ref[h] = st[h]

    @pl.when(step == pl.num_programs(1) - 1)
    def _():
        s_ref[0] = st_ref[...]


def _rw_chunked(ops, B, T):
    C = _tile(T, 64)
    nc = T // C
    G = 4 if nc % 4 == 0 else (2 if nc % 2 == 0 else 1)
    ns = nc // G
    blk = pl.BlockSpec((G * C, RW_W), lambda b, s: (b * ns + s, 0))
    return _pallas(
        functools.partial(_rw_chunk_kernel, C=C, G=G),
        out_shape=(jax.ShapeDtypeStruct((B * T, RW_W), F32),
                   jax.ShapeDtypeStruct((B, RW_HEADS, RW_N, RW_N), F32)),
        grid=(B, ns),
        in_specs=[blk] * 6,
        out_specs=(blk, pl.BlockSpec((1, RW_HEADS, RW_N, RW_N), lambda b, s: (b, 0, 0, 0))),
        scratch_shapes=[pltpu.VMEM((RW_HEADS, RW_N, RW_N), F32)],
        compiler_params=_cp("parallel", "arbitrary"),
        name="rwkv_chunked",
    )(*ops)


def _rw_step_kernel(s_ref, r_ref, wl_ref, k_ref, v_ref, kk_ref, b_ref, ej_ref, ei_ref, rd_ref, so_ref, y_ref):
    s = s_ref[0]
    ej = ej_ref[...]
    ei = ei_ref[...]
    rd = rd_ref[...]
    over_j = lambda x: _dot_sel_r(x, ej)
    sa = _dot_sel_r(s * over_j(kk_ref[0]), rd)
    s_new = (s * over_j(jnp.exp(wl_ref[0])) - _dot_sel_r(sa, ei) * over_j(b_ref[0])
             + _dot_sel_r(v_ref[0], ei) * over_j(k_ref[0]))
    so_ref[0] = s_new
    y_ref[0] = _dot_sel_r(s_new * over_j(r_ref[0]), rd)


def _rw_step(state_wkv, layer, vecs):
    _, DB, _ = state_wkv.shape
    H, N = RW_HEADS, RW_N
    bt = _tile(DB, 128)
    ej = jnp.asarray(np.kron(np.ones((1, N), np.float32), np.eye(N, dtype=np.float32)), BF16)
    ei = jnp.asarray(np.kron(np.eye(N, dtype=np.float32), np.ones((1, N), np.float32)), BF16)
    rd = ei.T
    sblk = pl.BlockSpec((1, bt, N * N), lambda i, h: (layer, i, h))
    oblk = pl.BlockSpec((1, bt, N * N), lambda i, h: (0, i, h))
    vblk = pl.BlockSpec((1, bt, N), lambda i, h: (h, i, 0))
    const = lambda a: pl.BlockSpec(a.shape, lambda i, h: (0, 0))
    s_new, y = _pallas(
        _rw_step_kernel,
        out_shape=(jax.ShapeDtypeStruct((1, DB, H * N * N), F32), jax.ShapeDtypeStruct((H, DB, N), F32)),
        grid=(DB // bt, H),
        in_specs=[sblk] + [vblk] * 6 + [const(ej), const(ei), const(rd)],
        out_specs=(oblk, vblk),
        compiler_params=_cp("parallel", "parallel"),
        name="rwkv_step",
    )(state_wkv, *vecs, ej, ei, rd)
    return s_new[0], y


def _rw_post_kernel(y_ref, r_ref, k_ref, v_ref, g_ref, lng_ref, lnb_ref, rk_ref, seg_ref, o_ref):
    seg = seg_ref[...]
    y = y_ref[...]
    mu = _dot_sel_r(y, seg) / RW_N
    d = y - mu
    var = _dot_sel_r(d * d, seg) / RW_N
    yn = (d * lax.rsqrt(var + GN_EPS)) * lng_ref[...] + lnb_ref[...]
    bonus = _dot_sel_r(r_ref[...] * k_ref[...] * rk_ref[...], seg) * v_ref[...]
    o_ref[...] = ((yn + bonus) * g_ref[...]).astype(o_ref.dtype)


def _rw_post(y, r, k, v, g, lw, tm_pref=512):
    M = y.shape[0]
    tm = _tile(M, tm_pref)
    seg = _seg_matrix(RW_W, RW_N)
    row = pl.BlockSpec((tm, RW_W), lambda i: (i, 0))
    const = lambda a: pl.BlockSpec(a.shape, lambda i: (0,) * a.ndim)
    weights = (lw["rw_lnx_g"], lw["rw_lnx_b"], lw["rw_r_k"], seg)
    return _pallas(
        _rw_post_kernel,
        out_shape=jax.ShapeDtypeStruct((M, RW_W), BF16),
        grid=(M // tm,),
        in_specs=[row] * 5 + [const(a) for a in weights],
        out_specs=row,
        compiler_params=_cp("parallel"),
        name="rwkv_post",
    )(y, r, k, v, g, *weights)


def _pool_windows(x_ext, halo_rows, count_of, wp_ref, ps_ref, o_ref):
    G = len(POOL_WINDOWS)
    gw = x_ext.shape[1] // G
    for gi, win in enumerate(POOL_WINDOWS):
        xg = x_ext[:, gi * gw:(gi + 1) * gw]
        s = xg
        step = 1
        while step < win:
            s = s + pltpu.roll(s, step, 0)
            step *= 2
        diff = s[halo_rows:] / count_of(win) - xg[halo_rows:]
        o = _dot(diff.astype(BF16), wp_ref[gi]) * ps_ref[:, gi * gw:(gi + 1) * gw]
        o_ref[:, gi * gw:(gi + 1) * gw] = o.astype(o_ref.dtype)


def _pool_seq_kernel(pc_ref, halo_ref, wp_ref, ps_ref, o_ref, *, tm, tpg):
    i = pl.program_id(0)
    t0 = (i % tpg) * tm
    hrow = lax.broadcasted_iota(jnp.int32, (halo_ref.shape[0], 1), 0)
    halo = jnp.where(hrow < jnp.where(t0 == 0, halo_ref.shape[0], 0), 0.0, halo_ref[...])
    x_ext = jnp.concatenate([halo, pc_ref[...]], axis=0)
    pos = t0 + lax.broadcasted_iota(jnp.int32, (tm, 1), 0)
    count_of = lambda win: jnp.minimum(pos + 1, win).astype(F32)
    _pool_windows(x_ext, halo_ref.shape[0], count_of, wp_ref, ps_ref, o_ref)


def _pool_prompt(p, w_pool, pool_scale, rpg, tm_pref=512):
    M = p.shape[0]
    W = 512
    HALO = 16
    tm = _tile(rpg, tm_pref)
    tpg = rpg // tm
    return _pallas(
        functools.partial(_pool_seq_kernel, tm=tm, tpg=tpg),
        out_shape=jax.ShapeDtypeStruct((M, W), BF16),
        grid=(M // tm,),
        in_specs=[pl.BlockSpec((tm, W), lambda i: (i, PC_OFF // W)),
                  pl.BlockSpec((HALO, W), lambda i: (jnp.maximum(i * (tm // HALO) - 1, 0), PC_OFF // W)),
                  pl.BlockSpec(w_pool.shape, lambda i: (0, 0, 0)),
                  pl.BlockSpec(pool_scale.shape, lambda i: (0, 0))],
        out_specs=pl.BlockSpec((tm, W), lambda i: (i, 0)),
        compiler_params=_cp("parallel"),
        name="pool_prompt",
    )(p, p, w_pool, pool_scale)


def _pool_step_kernel(st_ref, pc_ref, wp_ref, ps_ref, o_ref, *, past):
    G = len(POOL_WINDOWS)
    pc = pc_ref[...]
    gw = pc.shape[1] // G
    nbuf = st_ref.shape[0]
    for gi, win in enumerate(POOL_WINDOWS):
        sl = slice(gi * gw, (gi + 1) * gw)
        xg = pc[:, sl]
        total = xg
        for j in range(1, win):
            total = total + st_ref[nbuf - j][:, sl]
        diff = total / float(min(past + 1, win)) - xg
        o = _dot(diff.astype(BF16), wp_ref[gi]) * ps_ref[:, sl]
        o_ref[:, sl] = o.astype(o_ref.dtype)


def _pool_step(state_t, p, w_pool, pool_scale, past):
    nbuf, DB, W = state_t.shape
    return _pallas(
        functools.partial(_pool_step_kernel, past=past),
        out_shape=jax.ShapeDtypeStruct((DB, W), BF16),
        grid=(1,),
        in_specs=[pl.BlockSpec((nbuf, DB, W), lambda i: (0, 0, 0)),
                  pl.BlockSpec((DB, W), lambda i: (0, PC_OFF // W)),
                  pl.BlockSpec(w_pool.shape, lambda i: (0, 0, 0)),
                  pl.BlockSpec(pool_scale.shape, lambda i: (0, 0))],
        out_specs=pl.BlockSpec((DB, W), lambda i: (0, 0)),
        compiler_params=_cp("arbitrary"),
        name="pool_step",
    )(state_t, p, w_pool, pool_scale)


FFN_HALO = 16
FFN_SUB = 256


def _ffn_up_seq_kernel(x_ref, xh_ref, g_ref, sc_ref, sh_ref, wv_ref, wg_ref, cwv_ref, cwg_ref, cbv_ref, cbg_ref,
                       act_ref, tv_ref, tg_ref, h_ref, *, tm, tpg):
    i = pl.program_id(0)

    @pl.when(pl.program_id(1) == 0)
    def _():
        g, sc, sh = g_ref[...], sc_ref[0], sh_ref[0]
        h_ref[pl.ds(0, FFN_HALO), :] = _modnorm(xh_ref[...], g, sc, sh).astype(BF16)
        h_ref[pl.ds(FFN_HALO, tm), :] = _modnorm(x_ref[...], g, sc, sh).astype(BF16)

    h = h_ref[...]
    row = lax.broadcasted_iota(jnp.int32, (tm + FFN_HALO, 1), 0)
    keep = row >= jnp.where(i % tpg == 0, FFN_HALO, 0)

    def branch(w_ref, cw_ref, cb_ref, t_ref, cs):
        u = jnp.where(keep, _dot(h, w_ref[:, cs]), 0.0)
        t_ref[0, :, cs] = u[tm + FFN_HALO - 8:, :]
        cw = cw_ref[:, cs]
        return (cb_ref[:, cs] + pltpu.roll(u, 2, 0)[FFN_HALO:] * cw[0:1] + pltpu.roll(u, 1, 0)[FFN_HALO:] * cw[1:2]
                + u[FFN_HALO:] * cw[2:3])

    tn = act_ref.shape[1]
    for c0 in range(0, tn, FFN_SUB):
        cs = slice(c0, min(c0 + FFN_SUB, tn))
        val = branch(wv_ref, cwv_ref, cbv_ref, tv_ref, cs)
        gate = branch(wg_ref, cwg_ref, cbg_ref, tg_ref, cs)
        act_ref[:, cs] = ((gate * _sigmoid(gate)) * val).astype(act_ref.dtype)


def _ffn_up_prompt(x, g, sc, sh, w_up, layer, conv_w, conv_b, B, T, tm_pref=1024, tn_pref=704):
    M, D = x.shape
    F = w_up.shape[2] // 2
    tm = _tile(T, tm_pref)
    tpg = T // tm
    tn = max(t for t in range(LANE, tn_pref + 1, LANE) if F % t == 0)
    nj = F // tn
    lo = lambda i, j: (0, j)
    hi = lambda i, j: (0, nj + j)
    tail = jax.ShapeDtypeStruct((M // tm, 8, F), F32)
    tspec = pl.BlockSpec((1, 8, tn), lambda i, j: (i, 0, j))
    return _pallas(
        functools.partial(_ffn_up_seq_kernel, tm=tm, tpg=tpg),
        out_shape=(jax.ShapeDtypeStruct((M, F), BF16), tail, tail),
        grid=(M // tm, nj),
        in_specs=[pl.BlockSpec((tm, D), lambda i, j: (i, 0)),
                  pl.BlockSpec((FFN_HALO, D), lambda i, j: (jnp.maximum(i * (tm // FFN_HALO) - 1, 0), 0)),
                  pl.BlockSpec((1, D), lambda i, j: (0, 0)),
                  _mod_spec(tm, D, tpg, 1, False),
                  _mod_spec(tm, D, tpg, 1, False),
                  pl.BlockSpec((None, D, tn), lambda i, j: (layer, 0, j)),
                  pl.BlockSpec((None, D, tn), lambda i, j: (layer, 0, nj + j)),
                  pl.BlockSpec((3, tn), lo), pl.BlockSpec((3, tn), hi),
                  pl.BlockSpec((1, tn), lo), pl.BlockSpec((1, tn), hi)],
        out_specs=(pl.BlockSpec((tm, tn), lambda i, j: (i, j)), tspec, tspec),
        scratch_shapes=[pltpu.VMEM((tm + FFN_HALO, D), BF16)],
        compiler_params=_cp("arbitrary", "arbitrary"),
        name="ffn_up_prompt",
    )(x, x, g, sc, sh, w_up, w_up, conv_w, conv_w, conv_b, conv_b)


def _ffn_up_step_kernel(x_ref, g_ref, sc_ref, sh_ref, wv_ref, wg_ref, cwv_ref, cwg_ref, cbv_ref, cbg_ref,
                        s0v_ref, s0g_ref, s1v_ref, s1g_ref, act_ref, uv_ref, ug_ref, h_ref):
    @pl.when(pl.program_id(0) == 0)
    def _():
        h_ref[...] = _modnorm(x_ref[...], g_ref[...], sc_ref[0], sh_ref[0]).astype(BF16)

    h = h_ref[...]

    def branch(w_ref, cw_ref, cb_ref, s0_ref, s1_ref, u_ref):
        u = _dot(h, w_ref[...])
        u_ref[...] = u
        cw = cw_ref[...]
        return cb_ref[...] + s0_ref[...] * cw[0:1] + s1_ref[...] * cw[1:2] + u * cw[2:3]

    val = branch(wv_ref, cwv_ref, cbv_ref, s0v_ref, s1v_ref, uv_ref)
    gate = branch(wg_ref, cwg_ref, cbg_ref, s0g_ref, s1g_ref, ug_ref)
    act_ref[...] = ((gate * _sigmoid(gate)) * val).astype(act_ref.dtype)


def _ffn_up_step(x, g, sc, sh, w_up, layer, conv_w, conv_b, st0, st1, tn_pref=512):
    M, D = x.shape
    F = w_up.shape[2] // 2
    tn = _tile(F, tn_pref)
    nj = F // tn
    lo = lambda j: (0, j)
    hi = lambda j: (0, nj + j)
    full = lambda a: pl.BlockSpec(a.shape, lambda j: (0,) * a.ndim)
    u = jax.ShapeDtypeStruct((M, F), F32)
    ospec = pl.BlockSpec((M, tn), lo)
    return _pallas(
        _ffn_up_step_kernel,
        out_shape=(jax.ShapeDtypeStruct((M, F), BF16), u, u),
        grid=(nj,),
        in_specs=[full(x), full(g), full(sc), full(sh),
                  pl.BlockSpec((None, D, tn), lambda j: (layer, 0, j)),
                  pl.BlockSpec((None, D, tn), lambda j: (layer, 0, nj + j)),
                  pl.BlockSpec((3, tn), lo), pl.BlockSpec((3, tn), hi),
                  pl.BlockSpec((1, tn), lo), pl.BlockSpec((1, tn), hi),
                  pl.BlockSpec((M, tn), lo), pl.BlockSpec((M, tn), hi),
                  pl.BlockSpec((M, tn), lo), pl.BlockSpec((M, tn), hi)],
        out_specs=(ospec, ospec, ospec),
        scratch_shapes=[pltpu.VMEM((M, D), BF16)],
        compiler_params=_cp("arbitrary"),
        name="ffn_up_step",
    )(x, g, sc, sh, w_up, w_up, conv_w, conv_w, conv_b, conv_b, st0, st0, st1, st1)


def _in_proj_columns():
    a_cols = 512 + 128 + QK_ROPE
    b0 = a_cols
    sizes = (RW_W, DECAY_LORA, RW_W, RW_W, ICLR_LORA, GATE_LORA)
    offs = np.concatenate([[0], np.cumsum(sizes)])[:-1] + b0
    r0, wd0, k0, v0, ad0, gd0 = [int(o) for o in offs]
    b_cols = int(sum(sizes))
    c0 = b0 + b_cols
    pad = lambda n: [-1] * n
    rng = lambda s, n: list(range(s, s + n))
    cols = (rng(r0, RW_W) + rng(k0, RW_W) + rng(v0, RW_W)
            + rng(wd0, DECAY_LORA) + pad(LANE - DECAY_LORA)
            + rng(ad0, ICLR_LORA) + pad(LANE - ICLR_LORA)
            + rng(gd0, GATE_LORA)
            + rng(0, 512)
            + rng(c0, 512)
            + rng(512, 128)
            + rng(640, QK_ROPE) + pad(LANE - QK_ROPE))
    cols = np.asarray(cols, np.int32)
    assert cols.shape[0] == P_COLS
    pb_src = cols[:PB_W] - b0
    return cols, pb_src, b_cols


def _gather_cols(w, cols):
    cols = [int(c) for c in cols]
    runs, i = [], 0
    while i < len(cols):
        j = i + 1
        if cols[i] < 0:
            while j < len(cols) and cols[j] < 0:
                j += 1
            runs.append(jnp.zeros(w.shape[:-1] + (j - i,), w.dtype))
        else:
            while j < len(cols) and cols[j] == cols[j - 1] + 1:
                j += 1
            runs.append(w[..., cols[i]:cols[i] + (j - i)])
        i = j
    return jnp.concatenate(runs, axis=-1)


def _uq_columns():
    cols = []
    for h in range(MLA_HEADS):
        base = h * (QK_NOPE + QK_ROPE)
        cols += list(range(base, base + QK_NOPE + QK_ROPE)) + [-1] * (QK_PAD - QK_NOPE - QK_ROPE)
    return np.asarray(cols, np.int32)


def _pad_lanes(a, n):
    return jnp.pad(a, [(0, 0)] * (a.ndim - 1) + [(0, n - a.shape[-1])])


def _pad_rows(a, n):
    return jnp.pad(a, [(0, 0)] * (a.ndim - 2) + [(0, n - a.shape[-2]), (0, 0)])


def _rope_tables(pos):
    half = QK_ROPE // 2
    inv = jnp.power(ROPE_THETA, -jnp.arange(half, dtype=F32) * 2.0 / QK_ROPE)
    ang = pos.astype(F32)[:, None] * inv[None, :]
    cos, sin = jnp.cos(ang), jnp.sin(ang)
    z = jnp.zeros((pos.shape[0], LANE - QK_ROPE), F32)
    return jnp.concatenate([cos, cos, z], axis=1), jnp.concatenate([sin, sin, z], axis=1)


def kernel(x_prompt, x_sample, cache_latent, cache_krope, state_wkv, state_shift, state_pool, state_conv, page_table, c_prompt, c_sample, ada_w, ada_b, norm_mix_g, norm_ffn_g, w_in, q_lora_g, w_uq, kv_lora_g, w_uk, w_uv, qn_g, qr_g, kn_g, kr_g, rw_mu, rw_w0, rw_w_up, rw_a0, rw_a_up, rw_g_up, rw_k_k, rw_k_a, rw_r_k, rw_lnx_g, rw_lnx_b, w_pool, pool_scale, w_out, ffn_w_up, ffn_conv_w, ffn_conv_b, ffn_w_down):
    B, T, D = x_prompt.shape
    DB, TS, _ = x_sample.shape
    assert TS == 1, "the sample group decodes one token per sequence"
    L = ada_w.shape[0]
    n_pages = page_table.shape[1]
    past = n_pages * PAGE
    H, N = RW_HEADS, RW_N
    F = ffn_w_down.shape[1]

    cols, pb_src, b_cols = _in_proj_columns()
    pb_valid = np.nonzero(pb_src >= 0)[0]
    pb_inverse = pb_valid[np.argsort(pb_src[pb_valid])]

    w_in_p = _gather_cols(w_in, cols).astype(BF16)
    w_uq_p = _gather_cols(w_uq, _uq_columns()).astype(BF16)
    w_uk_f = w_uk.reshape(L, w_uk.shape[1], MLA_HEADS * QK_NOPE)
    w_uk_b = w_uk_f.astype(BF16)
    w_uk_t = jnp.swapaxes(w_uk_b, 1, 2)
    cache_krope_t = jnp.swapaxes(cache_krope, 2, 3)
    w_uv_b = w_uv.reshape(L, w_uv.shape[1], MLA_HEADS * V_HEAD).astype(BF16)
    w_out_b = w_out.astype(BF16)
    w_up_b = ffn_w_up.astype(BF16)
    w_down_b = ffn_w_down.astype(BF16)
    w_pool_b = w_pool.astype(BF16)
    mu_p = _gather_cols(rw_mu, pb_src)
    rw_w_up_p = _pad_rows(rw_w_up, LANE).astype(BF16)
    rw_a_up_p = _pad_rows(rw_a_up, LANE).astype(BF16)
    rw_g_up_b = rw_g_up.astype(BF16)
    qr_g_p = _pad_lanes(qr_g, LANE)
    kr_g_p = _pad_lanes(kr_g, LANE)
    shift_p = _gather_cols(state_shift, pb_src)
    row2 = lambda a, l: a[l].reshape(1, -1)

    mc = DB + B
    c_all = _pad_rows(jnp.concatenate([c_sample, c_prompt], axis=0), mc + (-mc) % 8)
    mod = _ada_mod(c_all, ada_w, ada_b)

    cos_p, sin_p = _rope_tables(jnp.arange(T, dtype=jnp.int32))
    cos_s, sin_s = _rope_tables(jnp.full((DB,), past, jnp.int32))

    state_wkv_flat = state_wkv.reshape(L, DB, H * N * N)

    xp = x_prompt.reshape(B * T, D)
    xs = x_sample.reshape(DB, D)
    outs = [[] for _ in range(12)]

    for l in range(L):
        lw = dict(q_lora_g=row2(q_lora_g, l), w_uq=w_uq_p[l], kv_lora_g=row2(kv_lora_g, l), qn_g=row2(qn_g, l),
                  qr_g=qr_g_p[l].reshape(1, -1), kn_g=row2(kn_g, l), kr_g=kr_g_p[l].reshape(1, -1),
                  w_uk=w_uk_b[l], w_uv=w_uv_b[l],
                  rw_mu=mu_p[l].reshape(1, -1), rw_w0=row2(rw_w0, l), rw_w_up=rw_w_up_p[l], rw_a0=row2(rw_a0, l),
                  rw_a_up=rw_a_up_p[l], rw_g_up=rw_g_up_b[l], rw_k_k=row2(rw_k_k, l), rw_k_a=row2(rw_k_a, l),
                  rw_r_k=row2(rw_r_k, l), rw_lnx_g=row2(rw_lnx_g, l), rw_lnx_b=row2(rw_lnx_b, l))
        g_mix, g_ffn = row2(norm_mix_g, l), row2(norm_ffn_g, l)
        ps = row2(pool_scale, l)
        cw, cb = ffn_conv_w[l], row2(ffn_conv_b, l)
        mods_p = [mod[l, DB:DB + B, k * D:(k + 1) * D][:, None, :] for k in range(6)]
        mods_s = [mod[l, :DB, k * D:(k + 1) * D][None] for k in range(6)]

        sh1, sc1, g1, sh2, sc2, g2 = mods_p
        p = _norm_mod_matmul(xp, g_mix, sc1, sh1, w_in_p, l, T)
        q, k, v, lat, kr = _mla_proj(p, cos_p, sin_p, lw, T, BF16)
        ya = _attn_prompt(q, k, v, B, T)
        r_, wl_, k_, v_, kk_, b_, g_ = _rw_prep(p, None, lw, T)
        y_raw, st_t = _rw_chunked((r_, wl_, k_, v_, kk_, b_), B, T)
        yb = _rw_post(y_raw, r_, k_, v_, g_, lw)
        yc = _pool_prompt(p, w_pool_b[l], ps, T)
        xp = _matmul_gate_res([ya, yb, yc], w_out_b, l, g1, xp, T)
        act, tail_v, tail_g = _ffn_up_prompt(xp, g_ffn, sc2, sh2, w_up_b, l, cw, cb, B, T)
        xp = _matmul_gate_res([act], w_down_b, l, g2, xp, T)

        p3 = p.reshape(B, T, P_COLS)
        outs[0].append(lat.reshape(B, T, LANE))
        outs[2].append(kr.reshape(B, T, LANE)[:, :, :QK_ROPE])
        outs[4].append(jnp.swapaxes(st_t, -1, -2))
        outs[6].append(_gather_cols(p3[:, -1, :PB_W], pb_inverse))
        pool_rows = p3[:, max(T - POOL_BUF, 0):, PC_OFF:PC_OFF + 512]
        if T < POOL_BUF:
            pool_rows = jnp.concatenate([jnp.zeros((B, POOL_BUF - T, 512), F32), pool_rows], axis=1)
        outs[8].append(pool_rows)
        last = lambda t: t.reshape(B, -1, 8, F)[:, -1, 6:8]
        outs[10].append(jnp.concatenate([last(tail_v), last(tail_g)], axis=-1))

        sh1, sc1, g1, sh2, sc2, g2 = mods_s
        p = _norm_mod_matmul(xs, g_mix, sc1, sh1, w_in_p, l, DB)
        q, _, _, lat, kr = _mla_proj(p, cos_s, sin_s, lw, DB, F32)
        qt, qrp = _absorb_q(q, lw["kn_g"], w_uk_f[l])
        o_lat = _paged_attention(page_table, jnp.swapaxes(qt, 0, 1), jnp.swapaxes(qrp, 0, 1), lat, kr, w_uk_t[l],
                                 cache_latent, cache_krope_t, l)
        ya = _uv_out(jnp.swapaxes(o_lat, 0, 1), lw["w_uv"])
        r_, wl_, k_, v_, kk_, b_, g_ = _rw_prep(p, shift_p[l], lw, DB)
        heads = lambda a: jnp.swapaxes(a.reshape(DB, H, N), 0, 1)
        s_new, y_h = _rw_step(state_wkv_flat, l, [heads(a) for a in (r_, wl_, k_, v_, kk_, b_)])
        y_raw = jnp.swapaxes(y_h, 0, 1).reshape(DB, H * N)
        yb = _rw_post(y_raw, r_, k_, v_, g_, lw)
        yc = _pool_step(jnp.swapaxes(state_pool[l], 0, 1), p, w_pool_b[l], ps, past)
        xs = _matmul_gate_res([ya, yb, yc], w_out_b, l, g1, xs, DB)
        act, u_v, u_g = _ffn_up_step(xs, g_ffn, sc2, sh2, w_up_b, l, cw, cb, state_conv[l, :, 0], state_conv[l, :, 1])
        xs = _matmul_gate_res([act], w_down_b, l, g2, xs, DB)

        outs[1].append(lat.reshape(DB, 1, LANE))
        outs[3].append(kr.reshape(DB, 1, LANE)[:, :, :QK_ROPE])
        outs[5].append(s_new.reshape(DB, H, N, N))
        outs[7].append(_gather_cols(p[:, :PB_W], pb_inverse))
        outs[9].append(jnp.concatenate([state_pool[l][:, 1:], p[:, None, PC_OFF:PC_OFF + 512]], axis=1))
        outs[11].append(jnp.stack([state_conv[l, :, 1], jnp.concatenate([u_v, u_g], axis=-1)], axis=1))

    stacked = [jnp.stack(o) for o in outs]
    return (xp.reshape(B, T, D), xs.reshape(DB, 1, D), *stacked)
```
